```python
import math
import jax
import jax.numpy as jnp
from jax import lax
import numpy as np

D_MODEL = 2048
BATCH = 2
SEQ = 16384
DEPTH = 2

GRID_W = 64
CTX_LEN = 256
NORM_EPS = 1e-6

RW_N = 64
RW_H = D_MODEL // RW_N
RW_D = RW_H * RW_N
W_LORA = 96
A_LORA = 96
G_LORA = 64
RW_GN_EPS = 64e-5

HY_D = D_MODEL
HY_ORDER = 2
HY_EMB = 33
HY_BANDS = (HY_EMB - 1) // 2
HY_HIDDEN = 64
HY_TARGET = 1e-2
HY_FAST_PCT = 0.3
HY_SLOW_PCT = 1.5

P_HEADS = 8
N_KEYS = 128
N_EXPERTS = N_KEYS * N_KEYS
P_TOPK = 16
D_KEY = 256
D_KEY_HALF = D_KEY // 2
PEER_BLOCK = 128

RW_STATE_COLS = 2 * RW_D + 2 * W_LORA + 2 * A_LORA
RWKV_COLS = RW_STATE_COLS + RW_D + G_LORA
HY_COLS = (HY_ORDER + 1) * HY_D
CONV_COLS = RWKV_COLS + HY_COLS
IN_COLS = CONV_COLS + 2 * D_MODEL

kernel_name = 'hybrid_rwkv7_hyena_peer_dit'


def rms_norm(x, g):
    xf = x.astype(jnp.float32)
    y = xf * lax.rsqrt(jnp.mean(xf * xf, axis=-1, keepdims=True) + NORM_EPS)
    return y.astype(x.dtype) * g


def modulate(x, g, shift, scale):
    return rms_norm(x, g) * (1.0 + scale) + shift


def short_conv(u, w, n_seg):
    b, L, C = u.shape
    us = u.reshape(b, n_seg, L // n_seg, C)
    up = jnp.pad(us, ((0, 0), (0, 0), (1, 1), (0, 0)))
    out = w[0] * up[:, :, :-2] + w[1] * up[:, :, 1:-1] + w[2] * up[:, :, 2:]
    return out.reshape(b, L, C)


def project_and_conv(h, w_in, conv_w, n_seg):
    p = h @ w_in
    return short_conv(p[..., :CONV_COLS], conv_w, n_seg), jax.nn.sigmoid(p[..., CONV_COLS:])


def rwkv_prepare(cols, w0, w2, a0, a2, k_k):
    b, L, _ = cols.shape
    heads = lambda t: t.reshape(b, L, RW_H, RW_N)
    k = heads(cols[..., :RW_D])
    v = heads(cols[..., RW_D:2 * RW_D])
    o = 2 * RW_D
    wd = (cols[..., o:o + W_LORA], cols[..., o + W_LORA:o + 2 * W_LORA])
    o += 2 * W_LORA
    ad = (cols[..., o:o + A_LORA], cols[..., o + A_LORA:o + 2 * A_LORA])
    kk = heads(cols[..., :RW_D] * k_k).astype(jnp.float32)
    kk = kk * lax.rsqrt(jnp.maximum(jnp.sum(kk * kk, axis=-1, keepdims=True), 1e-24))
    logw, a = [], []
    for d in range(2):
        wraw = (w0[d] + jnp.tanh(wd[d]) @ w2[d]).astype(jnp.float32)
        logw.append(heads(-jnp.exp(-jax.nn.softplus(-wraw) - 0.5)))
        a.append(heads(jax.nn.sigmoid((a0[d] + ad[d] @ a2[d]).astype(jnp.float32))))
    return k, v, kk, a, logw


def wkv_scan(k, v, kk, a, logw, k_a, s0, reverse, r):
    to_t = lambda t: jnp.moveaxis(t.astype(jnp.float32), 1, 0)
    xs = (to_t(k), to_t(v), to_t(kk), to_t(a), to_t(logw))
    if r is not None:
        xs = xs + (to_t(r),)

    def step(S, inp):
        k_t, v_t, kk_t, a_t, lw_t = inp[:5]
        kt = k_t * (1.0 + (a_t - 1.0) * k_a)
        s_kk = jnp.einsum('bhij,bhj->bhi', S, kk_t)
        S = (S * jnp.exp(lw_t)[:, :, None, :]
             - s_kk[..., :, None] * (kk_t * a_t)[:, :, None, :]
             + v_t[..., :, None] * kt[:, :, None, :])
        if r is None:
            return S, None
        return S, jnp.einsum('bhij,bhj->bhi', S, inp[5])

    s_fin, ys = lax.scan(step, s0, xs, reverse=reverse)
    return s_fin, (None if r is None else jnp.moveaxis(ys, 0, 1))


def rwkv_mixer(cols, rw_p, s0_f, s0_b, with_output):
    w0, w2, a0, a2, g2, k_k, k_a, r_k, ln_w, ln_b = rw_p
    k, v, kk, a, logw = rwkv_prepare(cols[..., :RW_STATE_COLS], w0, w2, a0, a2, k_k)
    k_a_h = k_a.reshape(RW_H, RW_N)
    if not with_output:
        s_f, _ = wkv_scan(k, v, kk, a[0], logw[0], k_a_h, s0_f, False, None)
        s_b, _ = wkv_scan(k, v, kk, a[1], logw[1], k_a_h, s0_b, True, None)
        return None, s_f, s_b
    b, L, _ = cols.shape
    r = cols[..., RW_STATE_COLS:RW_STATE_COLS + RW_D].reshape(b, L, RW_H, RW_N)
    gd = cols[..., RW_STATE_COLS + RW_D:RWKV_COLS]
    s_f, y_f = wkv_scan(k, v, kk, a[0], logw[0], k_a_h, s0_f, False, r)
    s_b, y_b = wkv_scan(k, v, kk, a[1], logw[1], k_a_h, s0_b, True, r)
    y = y_f + y_b
    mu = jnp.mean(y, axis=-1, keepdims=True)
    var = jnp.mean(jnp.square(y - mu), axis=-1, keepdims=True)
    yn = ((y - mu) * lax.rsqrt(var + RW_GN_EPS)).reshape(b, L, RW_D).astype(cols.dtype) * ln_w + ln_b
    bonus = (jnp.sum(r * k * r_k, axis=-1, keepdims=True) * v).reshape(b, L, RW_D)
    g = jax.nn.sigmoid(gd) @ g2
    return (yn + bonus) * g, s_f, s_b


def hyena_filter_spectrum(L, hy_p):
    w1, b1, w2, b2, w3, freq = hy_p
    f32 = jnp.float32
    t = jnp.linspace(0.0, 1.0, L, dtype=f32)[:, None]
    bands = jnp.linspace(1e-4, HY_BANDS - 1, HY_BANDS, dtype=f32)[None, :]
    ang = (2.0 * math.pi / L) * jnp.arange(L, dtype=f32)[:, None] * bands
    z = jnp.concatenate([t, jnp.cos(ang), -jnp.sin(ang)], axis=-1)
    hdn = jnp.sin(freq * (z @ w1 + b1))
    hdn = jnp.sin(freq * (hdn @ w2 + b2))
    h = (hdn @ w3).astype(f32).reshape(L, HY_ORDER, 2, HY_D)
    deltas = jnp.abs(jnp.linspace(math.log(HY_TARGET) / HY_SLOW_PCT, math.log(HY_TARGET) / HY_FAST_PCT, HY_D, dtype=f32))
    h = h * jnp.exp(-t * deltas)[:, None, None, :]
    h = h * lax.rsqrt(jnp.sum(h * h, axis=(0, 2), keepdims=True))
    fwd, bwd = h[:, :, 0], h[:, :, 1]
    g = jnp.concatenate([fwd[:1] + bwd[:1], fwd[1:], jnp.zeros_like(fwd[:1]), bwd[:0:-1]], axis=0)
    return jnp.fft.rfft(g, axis=0)


def fft_long_conv(u, spec):
    L = u.shape[1]
    U = jnp.fft.rfft(u.astype(jnp.float32), n=2 * L, axis=1)
    return jnp.fft.irfft(U * spec[None], n=2 * L, axis=1)[:, :L].astype(u.dtype)


def hyena_mixer(cols, spec, skip):
    v, x1, x2 = jnp.split(cols, 3, axis=-1)
    z = x1 * (fft_long_conv(v, spec[:, 0]) + v * skip[0])
    return x2 * (fft_long_conv(z, spec[:, 1]) + z * skip[1])


def token_mixing(h, w_in, conv_w, w_out, rw_p, hy_spec, hy_skip, n_seg, s0_f, s0_b):
    mix, gates = project_and_conv(h, w_in, conv_w, n_seg)
    y_rw, s_f, s_b = rwkv_mixer(mix[..., :RWKV_COLS], rw_p, s0_f, s0_b, True)
    y_hy = hyena_mixer(mix[..., RWKV_COLS:], hy_spec, hy_skip)
    y = gates[..., :D_MODEL] * y_rw + gates[..., D_MODEL:] * y_hy
    return y @ w_out, s_f, s_b


def peer(h, wq, keys, u_tab, v_tab):
    b, L, d = h.shape
    t = b * L
    hf = h.reshape(t, d)
    q = (hf @ wq).reshape(t, P_HEADS, 2, D_KEY_HALF)
    s = jnp.einsum('thpd,hpnd->thpn', q, keys).astype(jnp.float32)
    s1, i1 = lax.top_k(s[:, :, 0], P_TOPK)
    s2, i2 = lax.top_k(s[:, :, 1], P_TOPK)
    cand = (s1[..., :, None] + s2[..., None, :]).reshape(t, P_HEADS, P_TOPK * P_TOPK)
    cidx = (i1[..., :, None] * N_KEYS + i2[..., None, :]).reshape(t, P_HEADS, P_TOPK * P_TOPK)
    best, pos = lax.top_k(cand, P_TOPK)
    idx = jnp.take_along_axis(cidx, pos, axis=-1)
    gate = jax.nn.softmax(best, axis=-1).astype(h.dtype)
    nb = t // PEER_BLOCK

    def block(args):
        xb, ib, gb = args
        ub = jnp.take(u_tab, ib, axis=0)
        act = jax.nn.gelu(jnp.einsum('thkd,td->thk', ub, xb), approximate=False)
        vb = jnp.take(v_tab, ib, axis=0)
        return jnp.einsum('thk,thkd->td', gb * act, vb)

    out = lax.map(block, (hf.reshape(nb, PEER_BLOCK, d),
                          idx.reshape(nb, PEER_BLOCK, P_HEADS, P_TOPK),
                          gate.reshape(nb, PEER_BLOCK, P_HEADS, P_TOPK)))
    return out.reshape(b, L, d)


def setup_inputs(seed: int = 0) -> dict:
    key = jax.random.key(seed)
    ks = iter(jax.random.split(key, 48))
    nrm = lambda shape, s: s * jax.random.normal(next(ks), shape, jnp.float32)
    conv_base = jnp.array([0.25, 0.5, 0.25], dtype=jnp.float32)[None, :, None]
    return {
        'x': nrm((BATCH, SEQ, D_MODEL), 1.0),
        'c': nrm((BATCH, D_MODEL), 1.0),
        'ctx': nrm((BATCH, CTX_LEN, D_MODEL), 1.0),
        'c_ctx': nrm((D_MODEL,), 1.0),
        'ada_w': nrm((DEPTH, D_MODEL, 6 * D_MODEL), 0.5 * D_MODEL ** -0.5),
        'ada_b': nrm((DEPTH, 6 * D_MODEL), 0.01),
        'norm1_g': 1.0 + nrm((DEPTH, D_MODEL), 0.1),
        'norm2_g': 1.0 + nrm((DEPTH, D_MODEL), 0.1),
        'w_in': nrm((DEPTH, D_MODEL, IN_COLS), D_MODEL ** -0.5),
        'conv_w': conv_base + nrm((DEPTH, 3, CONV_COLS), 0.1),
        'rw_w0': nrm((DEPTH, 2, RW_D), 1.0) - 0.5,
        'rw_w2': nrm((DEPTH, 2, W_LORA, RW_D), 0.5 * W_LORA ** -0.5),
        'rw_a0': nrm((DEPTH, 2, RW_D), 0.1),
        'rw_a2': nrm((DEPTH, 2, A_LORA, RW_D), 0.5 * A_LORA ** -0.5),
        'rw_g2': nrm((DEPTH, G_LORA, RW_D), G_LORA ** -0.5),
        'rw_k_k': 0.85 + nrm((DEPTH, RW_D), 0.05),
        'rw_k_a': 1.0 + nrm((DEPTH, RW_D), 0.05),
        'rw_r_k': nrm((DEPTH, RW_H, RW_N), 0.1),
        'rw_ln_w': 1.0 + nrm((DEPTH, RW_D), 0.1),
        'rw_ln_b': nrm((DEPTH, RW_D), 0.01),
        'hy_w1': nrm((DEPTH, HY_EMB, HY_HIDDEN), HY_EMB ** -0.5),
        'hy_b1': nrm((DEPTH, HY_HIDDEN), 0.1),
        'hy_w2': nrm((DEPTH, HY_HIDDEN, HY_HIDDEN), HY_HIDDEN ** -0.5),
        'hy_b2': nrm((DEPTH, HY_HIDDEN), 0.1),
        'hy_w3': nrm((DEPTH, HY_HIDDEN, HY_ORDER * 2 * HY_D), HY_HIDDEN ** -0.5),
        'hy_freq': 1.0 + nrm((DEPTH, HY_HIDDEN), 0.1),
        'hy_skip': nrm((DEPTH, HY_ORDER, HY_D), 0.1),
        'w_out': nrm((DEPTH, D_MODEL, D_MODEL), D_MODEL ** -0.5),
        'peer_wq': nrm((DEPTH, D_MODEL, P_HEADS * D_KEY), D_MODEL ** -0.5),
        'peer_keys': nrm((DEPTH, P_HEADS, 2, N_KEYS, D_KEY_HALF), D_KEY_HALF ** -0.5),
        'peer_u': nrm((DEPTH, N_EXPERTS, D_MODEL), D_MODEL ** -0.5),
        'peer_v': nrm((DEPTH, N_EXPERTS, D_MODEL), 0.5),
        'final_g': 1.0 + nrm((D_MODEL,), 0.1),
    }


def reference(x, c, ctx, c_ctx, ada_w, ada_b, norm1_g, norm2_g, w_in, conv_w,
              rw_w0, rw_w2, rw_a0, rw_a2, rw_g2, rw_k_k, rw_k_a, rw_r_k, rw_ln_w, rw_ln_b,
              hy_w1, hy_b1, hy_w2, hy_b2, hy_w3, hy_freq, hy_skip, w_out,
              peer_wq, peer_keys, peer_u, peer_v, final_g):
    b, L, _ = x.shape
    rows = L // GRID_W
    ctx_len = ctx.shape[1]
    s_zero = jnp.zeros((b, RW_H, RW_N, RW_N), jnp.float32)
    for l in range(DEPTH):
        last = l == DEPTH - 1
        rw_p = (rw_w0[l], rw_w2[l], rw_a0[l], rw_a2[l], rw_g2[l], rw_k_k[l], rw_k_a[l],
                rw_r_k[l], rw_ln_w[l], rw_ln_b[l])
        hy_p = (hy_w1[l], hy_b1[l], hy_w2[l], hy_b2[l], hy_w3[l], hy_freq[l])
        mod_x = (jax.nn.silu(c) @ ada_w[l] + ada_b[l])[:, None, :]
        mod_c = (jax.nn.silu(c_ctx) @ ada_w[l] + ada_b[l])[None, None, :]
        sh1x, sc1x, ga1x, sh2x, sc2x, ga2x = jnp.split(mod_x, 6, axis=-1)
        sh1c, sc1c, ga1c, sh2c, sc2c, ga2c = jnp.split(mod_c, 6, axis=-1)

        hc = modulate(ctx, norm1_g[l], sh1c, sc1c)
        if last:
            cols_c = short_conv(hc @ w_in[l][:, :RW_STATE_COLS], conv_w[l][:, :RW_STATE_COLS], 1)
            _, s_f, s_b = rwkv_mixer(cols_c, rw_p, s_zero, s_zero, False)
        else:
            y_c, s_f, s_b = token_mixing(hc, w_in[l], conv_w[l], w_out[l], rw_p,
                                         hyena_filter_spectrum(ctx_len, hy_p), hy_skip[l], 1,
                                         s_zero, s_zero)
            ctx = ctx + ga1c * y_c
            ctx = ctx + ga2c * peer(modulate(ctx, norm2_g[l], sh2c, sc2c),
                                    peer_wq[l], peer_keys[l], peer_u[l], peer_v[l])

        hx = modulate(x, norm1_g[l], sh1x, sc1x)
        y_x, _, _ = token_mixing(hx, w_in[l], conv_w[l], w_out[l], rw_p,
                                 hyena_filter_spectrum(L, hy_p), hy_skip[l], rows, s_f, s_b)
        x = x + ga1x * y_x
        x = x + ga2x * peer(modulate(x, norm2_g[l], sh2x, sc2x),
                            peer_wq[l], peer_keys[l], peer_u[l], peer_v[l])
    return rms_norm(x, final_g)
```

```python
import functools
import math

import numpy as np
import jax
import jax.numpy as jnp
from jax import lax
from jax.experimental import pallas as pl
from jax.experimental.pallas import tpu as pltpu

F32 = jnp.float32
BF16 = jnp.bfloat16
HIGHEST = lax.Precision.HIGHEST

D_MODEL = 2048
GRID_W = 64
NORM_EPS = 1e-6
RW_N = 64
RW_H = D_MODEL // RW_N
RW_D = D_MODEL
W_LORA = 96
A_LORA = 96
G_LORA = 64
RW_GN_EPS = 64e-5
HY_D = D_MODEL
HY_ORDER = 2
HY_EMB = 33
HY_BANDS = (HY_EMB - 1) // 2
HY_HIDDEN = 64
HY_TARGET = 1e-2
HY_FAST_PCT = 0.3
HY_SLOW_PCT = 1.5
P_HEADS = 8
N_KEYS = 128
N_EXPERTS = N_KEYS * N_KEYS
P_TOPK = 16
D_KEY = 256

LANES = 128
MXU_DIM = 256
VMEM_LIMIT = 48 * 1024 * 1024

COL_K, COL_V, COL_R = 0, RW_D, 2 * RW_D
COL_HV, COL_X1, COL_X2 = 3 * RW_D, 4 * RW_D, 5 * RW_D
COL_GATE = 6 * RW_D
COL_SMALL = COL_GATE + 2 * D_MODEL
SMALL_W = 1024
N_PACKED = COL_SMALL + SMALL_W
PROJ_TN = 1024

CHUNK = 64
HEADS_PER_GROUP = MXU_DIM // RW_N
GROUP_W = HEADS_PER_GROUP * RW_N
N_GROUPS = RW_D // GROUP_W
STACK = HEADS_PER_GROUP * CHUNK


def _cparams(sem):
    return pltpu.CompilerParams(dimension_semantics=sem, vmem_limit_bytes=VMEM_LIMIT)


def _rwkv_consts(reverse):
    idx = np.arange(STACK)
    hb, t = idx // CHUNK, idx % CHUNK
    same = hb[:, None] == hb[None, :]
    tt, ss = t[:, None], t[None, :]
    strict = (tt < ss) if reverse else (tt > ss)
    eye = same & (tt == ss)
    ms = same & strict
    mi = ms | eye
    levels = []
    m = 1
    while m < CHUNK:
        levels.append(ms & ((tt // (2 * m)) == (ss // (2 * m))) & ((tt // m) != (ss // m)))
        m *= 2
    c = np.arange(CHUNK)
    tri = (c[:, None] <= c[None, :]) if reverse else (c[:, None] >= c[None, :])
    f = lambda a: jnp.asarray(a.astype(np.float32))
    return f(ms), f(mi), f(same), f(eye), f(np.stack(levels)), f(tri)


def _rwkv_kernel(k_ref, v_ref, r_ref, wd_ref, ad_ref, gd_ref, yf_ref,
                 w0_ref, w2_ref, a0_ref, a2_ref, g2_ref, kk_ref, ka_ref, rk_ref, lnw_ref, lnb_ref,
                 ms_ref, mi_ref, same_ref, eye_ref, lev_ref, tri_ref, s0_ref,
                 y_ref, sout_ref,
                 S_ref, lw_s, kt_s, kn_s, b_s, yacc_s, *, reverse, final, nchunk):
    step = pl.program_id(2)

    @pl.when(step == 0)
    def _():
        S_ref[...] = s0_ref[0, 0]

    same = same_ref[...]
    k = k_ref[0]
    wraw = w0_ref[...] + jnp.dot(jnp.tanh(wd_ref[0]).astype(BF16), w2_ref[...],
                                 preferred_element_type=F32)
    lw_s[...] = -jax.nn.sigmoid(wraw) * math.exp(-0.5)
    a = jax.nn.sigmoid(a0_ref[...] + jnp.dot(ad_ref[0].astype(BF16), a2_ref[...],
                                             preferred_element_type=F32))
    kt_s[...] = k * (1.0 + (a - 1.0) * ka_ref[...])
    kn = k * kk_ref[...]
    ss = jnp.dot(kn * kn, same, precision=HIGHEST, preferred_element_type=F32)
    kn = kn * lax.rsqrt(jnp.maximum(ss, 1e-24))
    kn_s[...] = kn
    b_s[...] = kn * a

    ms = ms_ref[...]
    mi = mi_ref[...]
    tri = tri_ref[...]
    mid = CHUNK // 2 if reverse else CHUNK // 2 - 1
    last = 0 if reverse else CHUNK - 1

    def stack(x):
        return (jnp.concatenate([x] * HEADS_PER_GROUP, axis=0) * same).astype(BF16)

    def tile(x):
        xb = x.astype(BF16)
        return jnp.concatenate([xb] * HEADS_PER_GROUP, axis=0)

    nt = (((1,), (1,)), ((), ()))
    tn = (((0,), (0,)), ((), ()))

    def chunk(ci, carry):
        c_idx = (nchunk - 1 - ci) if reverse else ci
        rows = pl.ds(pl.multiple_of(c_idx * CHUNK, CHUNK), CHUNK)
        lw = lw_s[rows, :]
        kt = kt_s[rows, :]
        kn = kn_s[rows, :]
        b = b_s[rows, :]
        r = r_ref[0, rows, :]
        v = v_ref[0, rows, :]
        c = jnp.dot(tri, lw, precision=HIGHEST, preferred_element_type=F32)
        cp = c - lw
        rho = c[mid:mid + 1, :]
        cend = c[last:last + 1, :]
        einv = jnp.exp(rho - c)
        eend = jnp.exp(cend - c)
        lhs = jnp.concatenate([stack(kn * jnp.exp(cp - rho)), stack(r * jnp.exp(c - rho))], axis=0)
        rhs = jnp.concatenate([tile(kt * einv), tile(b * einv)], axis=0)
        ag = lax.dot_general(lhs, rhs, nt, preferred_element_type=F32)
        ak = ag[:STACK, :STACK] * ms
        ab = ag[:STACK, STACK:] * ms
        gk = ag[STACK:, :STACK] * mi
        gb = ag[STACK:, STACK:] * mi
        tm = eye_ref[...] - ab * lev_ref[0]
        for lv in range(1, lev_ref.shape[0]):
            e = (ab * lev_ref[lv]).astype(BF16)
            tmb = tm.astype(BF16)
            te = jnp.dot(tmb, e, preferred_element_type=F32).astype(BF16)
            tm = tm - jnp.dot(te, tmb, preferred_element_type=F32)
        S = S_ref[...]
        lhs_s = jnp.concatenate([stack(kn * jnp.exp(cp)), stack(r * jnp.exp(c))], axis=0)
        qs = lax.dot_general(lhs_s, S.astype(BF16), nt, preferred_element_type=F32)
        vs = stack(v)
        rhs_u = qs[:STACK] + jnp.dot(ak.astype(BF16), vs, preferred_element_type=F32)
        us = jnp.dot(tm.astype(BF16), rhs_u.astype(BF16), preferred_element_type=F32)
        vu = jnp.concatenate([vs, (-us).astype(BF16)], axis=0)
        ys = qs[STACK:] + jnp.dot(jnp.concatenate([gk, gb], axis=1).astype(BF16), vu,
                                  preferred_element_type=F32)
        y = ys[0:CHUNK]
        for h in range(1, HEADS_PER_GROUP):
            y = y + ys[h * CHUNK:(h + 1) * CHUNK]
        if final:
            yacc_s[rows, :] = y
        else:
            y_ref[0, rows, :] = y
        kb = jnp.concatenate([tile(kt * eend), tile(b * eend)], axis=0)
        ds = lax.dot_general(vu, kb, tn, preferred_element_type=F32)
        S_ref[...] = S * jnp.exp(cend) + ds * same
        return carry

    lax.fori_loop(0, nchunk, chunk, 0)
    sout_ref[0, 0] = S_ref[...]

    if final:
        y = yf_ref[0] + yacc_s[...]
        inv_n = 1.0 / RW_N
        mu = jnp.dot(y, same, precision=HIGHEST, preferred_element_type=F32) * inv_n
        d = y - mu
        var = jnp.dot(d * d, same, precision=HIGHEST, preferred_element_type=F32) * inv_n
        yn = d * lax.rsqrt(var + RW_GN_EPS) * lnw_ref[...] + lnb_ref[...]
        r = r_ref[0]
        bonus = jnp.dot(r * k * rk_ref[...], same, precision=HIGHEST,
                        preferred_element_type=F32) * v_ref[0]
        g = jnp.dot(jax.nn.sigmoid(gd_ref[0]).astype(BF16), g2_ref[...], preferred_element_type=F32)
        y_ref[0] = (yn + bonus) * g


def _pad_rows(w, rows):
    return jnp.pad(w, ((0, rows - w.shape[0]), (0, 0)))


def _rwkv_direction(mix, yf, prm, s0, d, *, final):
    b, L, _ = mix.shape
    reverse = d == 1
    tb = min(L, 1024)
    nb = L // tb
    nchunk = tb // CHUNK
    ms, mi, same, eye, lev, tri = _rwkv_consts(reverse)
    blk = (lambda i: nb - 1 - i) if reverse else (lambda i: i)
    gcol = GROUP_W // LANES
    sm = COL_SMALL // LANES

    def col(c0):
        return pl.BlockSpec((1, tb, GROUP_W), lambda bi, g, i: (bi, blk(i), c0 // GROUP_W + g))

    def small(j):
        return pl.BlockSpec((1, tb, LANES), lambda bi, g, i: (bi, blk(i), sm + j))

    vec = pl.BlockSpec((1, GROUP_W), lambda bi, g, i: (0, g))
    lora = pl.BlockSpec((LANES, GROUP_W), lambda bi, g, i: (0, g))
    const2 = lambda shp: pl.BlockSpec(shp, lambda bi, g, i: (0,) * len(shp))
    state = pl.BlockSpec((1, 1, GROUP_W, GROUP_W), lambda bi, g, i: (bi, g, 0, 0))
    yspec = pl.BlockSpec((1, tb, GROUP_W), lambda bi, g, i: (bi, blk(i), g))

    kern = functools.partial(_rwkv_kernel, reverse=reverse, final=final, nchunk=nchunk)
    if yf is None:
        yf, yf_spec = mix, col(COL_K)
    else:
        yf_spec = yspec
    y, s_fin = pl.pallas_call(
        kern,
        grid=(b, N_GROUPS, nb),
        in_specs=[col(COL_K), col(COL_V), col(COL_R), small(d), small(2 + d), small(4), yf_spec,
                  vec, lora, vec, lora, lora, vec, vec, vec, vec, vec,
                  const2((STACK, STACK)), const2((STACK, STACK)), const2((STACK, STACK)),
                  const2((STACK, STACK)), const2(tuple(lev.shape)), const2((CHUNK, CHUNK)), state],
        out_specs=[yspec, state],
        out_shape=[jax.ShapeDtypeStruct((b, L, RW_D), F32),
                   jax.ShapeDtypeStruct((b, N_GROUPS, GROUP_W, GROUP_W), F32)],
        scratch_shapes=[pltpu.VMEM((GROUP_W, GROUP_W), F32)] + [pltpu.VMEM((tb, GROUP_W), F32)] * 5,
        compiler_params=_cparams(("arbitrary", "arbitrary", "arbitrary")),
        name="rwkv_bwd" if reverse else "rwkv_fwd",
    )(mix, mix, mix, mix, mix, mix, yf,
      prm["w0"][d], prm["w2"][d], prm["a0"][d], prm["a2"][d], prm["g2"], prm["k_k"], prm["k_a"],
      prm["r_k"], prm["ln_w"], prm["ln_b"],
      ms, mi, same, eye, lev, tri, s0)
    return y, s_fin


def _rwkv_params(l, rw_w0, rw_w2, rw_a0, rw_a2, rw_g2, rw_k_k, rw_k_a, rw_r_k, rw_ln_w, rw_ln_b):
    row = lambda v: v.reshape(1, RW_D)
    return {
        "w0": [row(rw_w0[l, d]) for d in range(2)],
        "w2": [_pad_rows(rw_w2[l, d], LANES).astype(BF16) for d in range(2)],
        "a0": [row(rw_a0[l, d]) for d in range(2)],
        "a2": [_pad_rows(rw_a2[l, d], LANES).astype(BF16) for d in range(2)],
        "g2": _pad_rows(rw_g2[l], LANES).astype(BF16),
        "k_k": row(rw_k_k[l]), "k_a": row(rw_k_a[l]), "r_k": row(rw_r_k[l]),
        "ln_w": row(rw_ln_w[l]), "ln_b": row(rw_ln_b[l]),
    }


def _rwkv_mixer(mix, prm, s0_f, s0_b):
    b, L, _ = mix.shape
    y_f, s_f = _rwkv_direction(mix, None, prm, s0_f, 0, final=False)
    y, s_b = _rwkv_direction(mix, y_f, prm, s0_b, 1, final=True)
    return y, s_f, s_b


def _matmul_kernel(a_ref, b_ref, o_ref):
    o_ref[...] = jnp.dot(a_ref[...].astype(BF16), b_ref[...].astype(BF16), preferred_element_type=F32)


def _matmul(a, b, tn):
    m, k = a.shape
    n = b.shape[1]
    return pl.pallas_call(
        _matmul_kernel,
        grid=(n // tn,),
        in_specs=[pl.BlockSpec((m, k), lambda j: (0, 0)), pl.BlockSpec((k, tn), lambda j: (0, j))],
        out_specs=pl.BlockSpec((m, tn), lambda j: (0, j)),
        out_shape=jax.ShapeDtypeStruct((m, n), F32),
        compiler_params=_cparams(("arbitrary",)),
        name="matmul",
    )(a, b)


def _modulate(x, g, scale, shift):
    y = x * lax.rsqrt(jnp.mean(x * x, axis=-1, keepdims=True) + NORM_EPS)
    return y * g * (1.0 + scale) + shift


def _pack_cols(m):
    z = lambda n: jnp.zeros(m.shape[:-1] + (n,), m.dtype)
    s = lambda a, b: m[..., a:b]
    o = 2 * RW_D
    lo = [s(o + j * W_LORA, o + (j + 1) * W_LORA) for j in range(4)]
    r0 = o + 2 * W_LORA + 2 * A_LORA
    g0 = r0 + RW_D
    h0 = g0 + G_LORA
    parts = [s(0, RW_D), s(RW_D, 2 * RW_D), s(r0, r0 + RW_D),
             s(h0, h0 + HY_D), s(h0 + HY_D, h0 + 2 * HY_D), s(h0 + 2 * HY_D, h0 + 3 * HY_D)]
    if m.shape[-1] > h0 + 3 * HY_D:
        parts.append(s(h0 + 3 * HY_D, h0 + 3 * HY_D + 2 * D_MODEL))
    else:
        parts.append(z(2 * D_MODEL))
    for p in lo:
        parts += [p, z(LANES - W_LORA)]
    parts += [s(g0, g0 + G_LORA), z(LANES - G_LORA), z(SMALL_W - 5 * LANES)]
    return jnp.concatenate(parts, axis=-1)


def _inproj_kernel(x_ref, g_ref, sc_ref, sh_ref, w_ref, cw_ref, o_ref, h_s, *, seg):
    j = pl.program_id(2)

    @pl.when(j == 0)
    def _():
        h_s[...] = _modulate(x_ref[0], g_ref[...], sc_ref[0], sh_ref[0]).astype(BF16)

    p = jnp.dot(h_s[...], w_ref[...], preferred_element_type=F32)
    gate_lo = COL_GATE // PROJ_TN
    gate_hi = COL_SMALL // PROJ_TN
    is_gate = jnp.logical_and(j >= gate_lo, j < gate_hi)

    @pl.when(is_gate)
    def _():
        o_ref[0] = jax.nn.sigmoid(p)

    @pl.when(jnp.logical_not(is_gate))
    def _():
        tm = p.shape[0]
        row = lax.broadcasted_iota(jnp.int32, p.shape, 0) % seg
        prev = jnp.where(row == 0, 0.0, pltpu.roll(p, 1, 0))
        nxt = jnp.where(row == seg - 1, 0.0, pltpu.roll(p, tm - 1, 0))
        o_ref[0] = cw_ref[0:1, :] * prev + cw_ref[1:2, :] * p + cw_ref[2:3, :] * nxt


def _in_proj(x, g, scale, shift, w_packed, cw_packed, seg):
    b, L, d = x.shape
    tm = min(L, 512)
    return pl.pallas_call(
        functools.partial(_inproj_kernel, seg=seg),
        grid=(b, L // tm, N_PACKED // PROJ_TN),
        in_specs=[pl.BlockSpec((1, tm, d), lambda bi, i, j: (bi, i, 0)),
                  pl.BlockSpec((1, d), lambda bi, i, j: (0, 0)),
                  pl.BlockSpec((1, 1, d), lambda bi, i, j: (bi, 0, 0)),
                  pl.BlockSpec((1, 1, d), lambda bi, i, j: (bi, 0, 0)),
                  pl.BlockSpec((d, PROJ_TN), lambda bi, i, j: (0, j)),
                  pl.BlockSpec((3, PROJ_TN), lambda bi, i, j: (0, j))],
        out_specs=pl.BlockSpec((1, tm, PROJ_TN), lambda bi, i, j: (bi, i, j)),
        out_shape=jax.ShapeDtypeStruct((b, L, N_PACKED), F32),
        scratch_shapes=[pltpu.VMEM((tm, d), BF16)],
        compiler_params=_cparams(("arbitrary", "arbitrary", "arbitrary")),
        name="in_proj",
    )(x, g, scale, shift, w_packed, cw_packed)


def _outproj_kernel(yrw_ref, yhy_ref, grw_ref, ghy_ref, x_ref, ga_ref, w_ref, o_ref):
    y = grw_ref[0] * yrw_ref[0] + ghy_ref[0] * yhy_ref[0]
    o_ref[0] = x_ref[0] + ga_ref[0] * jnp.dot(y.astype(BF16), w_ref[...], preferred_element_type=F32)


def _out_proj(y_rw, y_hy, mix, x, ga, w_out_bf16):
    b, L, d = x.shape
    tm = 256
    tok = pl.BlockSpec((1, tm, d), lambda bi, i: (bi, i, 0))
    gcol = COL_GATE // d
    return pl.pallas_call(
        _outproj_kernel,
        grid=(b, L // tm),
        in_specs=[tok, tok,
                  pl.BlockSpec((1, tm, d), lambda bi, i: (bi, i, gcol)),
                  pl.BlockSpec((1, tm, d), lambda bi, i: (bi, i, gcol + 1)),
                  tok,
                  pl.BlockSpec((1, 1, d), lambda bi, i: (bi, 0, 0)),
                  pl.BlockSpec((d, d), lambda bi, i: (0, 0))],
        out_specs=tok,
        out_shape=jax.ShapeDtypeStruct((b, L, d), F32),
        compiler_params=_cparams(("arbitrary", "arbitrary")),
        name="out_proj",
    )(y_rw, y_hy, mix, mix, x, ga, w_out_bf16)


PEER_TM = 512
PEER_TE = 512
STAT_ROWS = 8


def _top_rows(s, n):
    rows = lax.broadcasted_iota(jnp.int32, (n, s.shape[1]), 0)
    out = jnp.zeros((n, s.shape[1]), F32)
    cur = s
    for i in range(n):
        m = jnp.max(cur, axis=0, keepdims=True)
        out = jnp.where(rows == i, m, out)
        cur = jnp.where(cur == m, -jnp.inf, cur)
    return out


def _route_kernel(x_ref, g_ref, sc_ref, sh_ref, wqt_ref, keys_ref, h_ref, s1_ref, s2_ref, st_ref):
    hb = _modulate(x_ref[0], g_ref[...], sc_ref[0], sh_ref[0]).astype(BF16)
    h_ref[0] = hb
    nt = (((1,), (1,)), ((), ()))
    qt = lax.dot_general(wqt_ref[...], hb, nt, preferred_element_type=F32)
    half = D_KEY // 2
    for hd in range(P_HEADS):
        sc = []
        for p in range(2):
            r0 = (hd * 2 + p) * half
            sc.append(jnp.dot(keys_ref[hd, p], qt[r0:r0 + half, :].astype(BF16),
                              preferred_element_type=F32))
        s1_ref[0, hd] = sc[0]
        s2_ref[0, hd] = sc[1]
        t1 = _top_rows(sc[0], P_TOPK)
        t2 = _top_rows(sc[1], P_TOPK)
        r8 = lax.broadcasted_iota(jnp.int32, (8, t1.shape[1]), 0)
        tiles = [t1[0:1] + t2[0:8], t1[0:1] + t2[8:16], t1[1:2] + t2[0:8]]
        for i in range(2, 8):
            tiles.append(jnp.where(r8 < P_TOPK // (i + 1), t1[i:i + 1] + t2[0:8], -jnp.inf))
        tiles.append(t1[8:16] + t2[0:1])
        best = _top_rows(jnp.concatenate(tiles, axis=0), P_TOPK)
        z = jnp.sum(jnp.exp(best - best[0:1]), axis=0, keepdims=True)
        st = jnp.where(r8 == 0, best[P_TOPK - 1:P_TOPK], 0.0)
        st = jnp.where(r8 == 1, t1[0:1], st)
        st = jnp.where(r8 == 2, t2[0:1], st)
        st_ref[0, hd] = jnp.where(r8 == 3, 1.0 / z, st)


def _peer_route(x, g, scale, shift, wqt_bf16, keys_bf16):
    b, L, d = x.shape
    tm = min(L, PEER_TM)
    tok = lambda: pl.BlockSpec((1, tm, d), lambda bi, i: (bi, i, 0))
    rowv = pl.BlockSpec((1, 1, d), lambda bi, i: (bi, 0, 0))
    sspec = pl.BlockSpec((1, P_HEADS, N_KEYS, tm), lambda bi, i: (bi, 0, 0, i))
    return pl.pallas_call(
        _route_kernel,
        grid=(b, L // tm),
        in_specs=[tok(), pl.BlockSpec((1, d), lambda bi, i: (0, 0)), rowv, rowv,
                  pl.BlockSpec(wqt_bf16.shape, lambda bi, i: (0, 0)),
                  pl.BlockSpec(keys_bf16.shape, lambda bi, i: (0, 0, 0, 0))],
        out_specs=[tok(), sspec, sspec,
                   pl.BlockSpec((1, P_HEADS, STAT_ROWS, tm), lambda bi, i: (bi, 0, 0, i))],
        out_shape=[jax.ShapeDtypeStruct((b, L, d), BF16),
                   jax.ShapeDtypeStruct((b, P_HEADS, N_KEYS, L), F32),
                   jax.ShapeDtypeStruct((b, P_HEADS, N_KEYS, L), F32),
                   jax.ShapeDtypeStruct((b, P_HEADS, STAT_ROWS, L), F32)],
        compiler_params=_cparams(("arbitrary", "arbitrary")),
        name="peer_route",
    )(x, g, scale, shift, wqt_bf16, keys_bf16)


def _peer_dense_kernel(h_ref, u_ref, vt_ref, s1_ref, s2_ref, st_ref, x_ref, ga_ref, o_ref, acc_s, e2_s):
    e = pl.program_id(2)
    ne = pl.num_programs(2)

    @pl.when(e == 0)
    def _():
        acc_s[...] = jnp.zeros_like(acc_s)
        for hd in range(P_HEADS):
            e2_s[hd] = jnp.exp(s2_ref[0, hd] - st_ref[0, hd, 2:3, :])

    nt = (((1,), (1,)), ((), ()))
    act = lax.dot_general(u_ref[...], h_ref[0], nt, preferred_element_type=F32)
    ge = 0.5 * act * (1.0 + lax.erf(act * (1.0 / math.sqrt(2.0))))
    n1 = PEER_TE // N_KEYS
    ws = []
    for i in range(n1):
        i1 = e * n1 + i
        gsum = None
        for hd in range(P_HEADS):
            s1row = s1_ref[0, hd, pl.ds(i1, 1), :]
            thr = st_ref[0, hd, 0:1, :]
            c1 = jnp.exp(s1row - st_ref[0, hd, 1:2, :]) * st_ref[0, hd, 3:4, :]
            sel = (s2_ref[0, hd] + s1row) >= thr
            term = jnp.where(sel, e2_s[hd] * c1, 0.0)
            gsum = term if gsum is None else gsum + term
        ws.append((gsum * ge[i * N_KEYS:(i + 1) * N_KEYS, :]).astype(BF16))
    w = jnp.concatenate(ws, axis=0)
    acc_s[...] += jnp.dot(vt_ref[...], w, preferred_element_type=F32)

    @pl.when(e == ne - 1)
    def _():
        o_ref[0] = x_ref[0] + ga_ref[0] * acc_s[...].T


def _peer_dense(h2, u_bf16, vt_bf16, s1, s2, st, x, ga):
    b, L, d = x.shape
    tm = min(L, PEER_TM)
    tok = lambda: pl.BlockSpec((1, tm, d), lambda bi, i, e: (bi, i, 0))
    sspec = lambda: pl.BlockSpec((1, P_HEADS, N_KEYS, tm), lambda bi, i, e: (bi, 0, 0, i))
    return pl.pallas_call(
        _peer_dense_kernel,
        grid=(b, L // tm, N_EXPERTS // PEER_TE),
        in_specs=[tok(),
                  pl.BlockSpec((PEER_TE, d), lambda bi, i, e: (e, 0)),
                  pl.BlockSpec((d, PEER_TE), lambda bi, i, e: (0, e)),
                  sspec(), sspec(),
                  pl.BlockSpec((1, P_HEADS, STAT_ROWS, tm), lambda bi, i, e: (bi, 0, 0, i)),
                  tok(),
                  pl.BlockSpec((1, 1, d), lambda bi, i, e: (bi, 0, 0))],
        out_specs=tok(),
        out_shape=jax.ShapeDtypeStruct((b, L, d), F32),
        scratch_shapes=[pltpu.VMEM((d, tm), F32), pltpu.VMEM((P_HEADS, N_KEYS, tm), F32)],
        compiler_params=pltpu.CompilerParams(
            dimension_semantics=("arbitrary", "arbitrary", "arbitrary"),
            vmem_limit_bytes=56 * 1024 * 1024),
        name="peer_dense",
    )(h2, u_bf16, vt_bf16, s1, s2, st, x, ga)


def _peer_block(x, g, scale, shift, ga, pw):
    h2, s1, s2, st = _peer_route(x, g, scale, shift, pw["wqt"], pw["keys"])
    return _peer_dense(h2, pw["u"], pw["vt"], s1, s2, st, x, ga)


def _final_norm_kernel(x_ref, g_ref, o_ref):
    x = x_ref[0]
    o_ref[0] = x * lax.rsqrt(jnp.mean(x * x, axis=-1, keepdims=True) + NORM_EPS) * g_ref[...]


def _final_norm(x, g):
    b, L, d = x.shape
    tm = 512
    tok = pl.BlockSpec((1, tm, d), lambda bi, i: (bi, i, 0))
    return pl.pallas_call(
        _final_norm_kernel,
        grid=(b, L // tm),
        in_specs=[tok, pl.BlockSpec((1, d), lambda bi, i: (0, 0))],
        out_specs=tok,
        out_shape=jax.ShapeDtypeStruct((b, L, d), F32),
        compiler_params=_cparams(("arbitrary", "arbitrary")),
        name="final_norm",
    )(x, g.reshape(1, d))


def _hyena_filter_spectrum(L, hy_p):
    w1, b1, w2, b2, w3, freq = hy_p
    t = jnp.linspace(0.0, 1.0, L, dtype=F32)[:, None]
    bands = jnp.linspace(1e-4, HY_BANDS - 1, HY_BANDS, dtype=F32)[None, :]
    ang = (2.0 * math.pi / L) * jnp.arange(L, dtype=F32)[:, None] * bands
    z = jnp.concatenate([t, jnp.cos(ang), -jnp.sin(ang)], axis=-1)
    hdn = jnp.sin(freq * (z @ w1 + b1))
    hdn = jnp.sin(freq * (hdn @ w2 + b2))
    h = (hdn @ w3).astype(F32).reshape(L, HY_ORDER, 2, HY_D)
    deltas = jnp.abs(jnp.linspace(math.log(HY_TARGET) / HY_SLOW_PCT, math.log(HY_TARGET) / HY_FAST_PCT,
                                  HY_D, dtype=F32))
    h = h * jnp.exp(-t * deltas)[:, None, None, :]
    h = h * lax.rsqrt(jnp.sum(h * h, axis=(0, 2), keepdims=True))
    fwd, bwd = h[:, :, 0], h[:, :, 1]
    g = jnp.concatenate([fwd[:1] + bwd[:1], fwd[1:], jnp.zeros_like(fwd[:1]), bwd[:0:-1]], axis=0)
    return jnp.fft.rfft(g, axis=0)


def _fft_long_conv(u, spec):
    L = u.shape[1]
    U = jnp.fft.rfft(u, n=2 * L, axis=1)
    return jnp.fft.irfft(U * spec[None], n=2 * L, axis=1)[:, :L]


def _hyena_mixer(mix, spec, skip):
    v = mix[..., COL_HV:COL_HV + HY_D]
    x1 = mix[..., COL_X1:COL_X1 + HY_D]
    x2 = mix[..., COL_X2:COL_X2 + HY_D]
    z = x1 * (_fft_long_conv(v, spec[:, 0]) + v * skip[0])
    return x2 * (_fft_long_conv(z, spec[:, 1]) + z * skip[1])


def _token_mixing(x, g, scale, shift, ga, lw, n_seg, s0_f, s0_b, spec):
    b, L, _ = x.shape
    mix = _in_proj(x, g, scale, shift, lw["w_in"], lw["conv_w"], L // n_seg)
    y_rw, s_f, s_b = _rwkv_mixer(mix, lw["rw"], s0_f, s0_b)
    if spec is None:
        return None, s_f, s_b
    y_hy = _hyena_mixer(mix, spec, lw["hy_skip"])
    return _out_proj(y_rw, y_hy, mix, x, ga, lw["w_out"]), s_f, s_b


def kernel(x, c, ctx, c_ctx, ada_w, ada_b, norm1_g, norm2_g, w_in, conv_w, rw_w0, rw_w2, rw_a0, rw_a2, rw_g2, rw_k_k, rw_k_a, rw_r_k, rw_ln_w, rw_ln_b, hy_w1, hy_b1, hy_w2, hy_b2, hy_w3, hy_freq, hy_skip, w_out, peer_wq, peer_keys, peer_u, peer_v, final_g):
    b, L, d = x.shape
    depth = ada_w.shape[0]
    ctx_len = ctx.shape[1]
    s_zero = jnp.zeros((b, N_GROUPS, GROUP_W, GROUP_W), F32)
    cond = jnp.concatenate([c, c_ctx[None, :]], axis=0)
    cond = jnp.pad(jax.nn.silu(cond), ((0, 8 - (b + 1)), (0, 0)))
    for l in range(depth):
        last = l == depth - 1
        mod = _matmul(cond, ada_w[l], 1024) + ada_b[l]
        mx = [m[:, None, :] for m in jnp.split(mod[:b], 6, axis=-1)]
        mc = [jnp.broadcast_to(m[None, :, :], (b, 1, d)) for m in jnp.split(mod[b:b + 1], 6, axis=-1)]
        lw = {
            "w_in": _pack_cols(w_in[l]).astype(BF16),
            "conv_w": _pack_cols(conv_w[l]),
            "rw": _rwkv_params(l, rw_w0, rw_w2, rw_a0, rw_a2, rw_g2, rw_k_k, rw_k_a, rw_r_k, rw_ln_w, rw_ln_b),
            "hy_skip": hy_skip[l],
            "w_out": w_out[l].astype(BF16),
        }
        hy_p = (hy_w1[l], hy_b1[l], hy_w2[l], hy_b2[l], hy_w3[l], hy_freq[l])
        pw = {
            "wqt": peer_wq[l].T.astype(BF16),
            "keys": peer_keys[l].astype(BF16),
            "u": peer_u[l].astype(BF16),
            "vt": peer_v[l].T.astype(BF16),
        }
        g1 = norm1_g[l].reshape(1, d)
        g2 = norm2_g[l].reshape(1, d)

        spec_c = None if last else _hyena_filter_spectrum(ctx_len, hy_p)
        ctx_new, s_f, s_b = _token_mixing(ctx, g1, mc[1], mc[0], mc[2], lw, 1, s_zero, s_zero, spec_c)
        if not last:
            ctx = _peer_block(ctx_new, g2, mc[4], mc[3], mc[5], pw)

        spec_x = _hyena_filter_spectrum(L, hy_p)
        x, _, _ = _token_mixing(x, g1, mx[1], mx[0], mx[2], lw, L // GRID_W, s_f, s_b, spec_x)
        x = _peer_block(x, g2, mx[4], mx[3], mx[5], pw)
    return _final_norm(x, final_g)
```

```python
import functools
import math

import numpy as np
import jax
import jax.numpy as jnp
from jax import lax
from jax.experimental import pallas as pl
from jax.experimental.pallas import tpu as pltpu

F32 = jnp.float32
BF16 = jnp.bfloat16
HIGHEST = lax.Precision.HIGHEST

D_MODEL = 2048
GRID_W = 64
NORM_EPS = 1e-6
RW_N = 64
RW_H = D_MODEL // RW_N
RW_D = D_MODEL
W_LORA = 96
A_LORA = 96
G_LORA = 64
RW_GN_EPS = 64e-5
HY_D = D_MODEL
HY_ORDER = 2
HY_EMB = 33
HY_BANDS = (HY_EMB - 1) // 2
HY_HIDDEN = 64
HY_TARGET = 1e-2
HY_FAST_PCT = 0.3
HY_SLOW_PCT = 1.5
P_HEADS = 8
N_KEYS = 128
N_EXPERTS = N_KEYS * N_KEYS
P_TOPK = 16
D_KEY = 256

LANES = 128
MXU_DIM = 256
VMEM_LIMIT = 48 * 1024 * 1024

COL_K, COL_V, COL_R = 0, RW_D, 2 * RW_D
COL_GATE = 3 * RW_D
COL_SMALL = COL_GATE + 2 * D_MODEL
SMALL_W = 1024
COL_HY = COL_SMALL + SMALL_W
N_PACKED = COL_HY + 3 * HY_D
PROJ_TN = 1024

CHUNK = 64
HEADS_PER_GROUP = MXU_DIM // RW_N
GROUP_W = HEADS_PER_GROUP * RW_N
N_GROUPS = RW_D // GROUP_W
STACK = HEADS_PER_GROUP * CHUNK
GROUPS_PER_STEP = 4
STEP_W = GROUPS_PER_STEP * GROUP_W
RWKV_TB = 256


def _cparams(sem):
    return pltpu.CompilerParams(dimension_semantics=sem, vmem_limit_bytes=VMEM_LIMIT)


def _rwkv_consts(reverse):
    idx = np.arange(STACK)
    hb, t = idx // CHUNK, idx % CHUNK
    same = hb[:, None] == hb[None, :]
    tt, ss = t[:, None], t[None, :]
    strict = (tt < ss) if reverse else (tt > ss)
    eye = same & (tt == ss)
    ms = same & strict
    mi = ms | eye
    levels = []
    m = 1
    while m < CHUNK:
        levels.append(ms & ((tt // (2 * m)) == (ss // (2 * m))) & ((tt // m) != (ss // m)))
        m *= 2
    c = np.arange(CHUNK)
    tri = (c[:, None] <= c[None, :]) if reverse else (c[:, None] >= c[None, :])
    f = lambda a: jnp.asarray(a.astype(np.float32))
    return f(ms), f(mi), f(same), f(eye), f(np.stack(levels)), f(tri)


def _rwkv_kernel(k_ref, v_ref, r_ref, wd_ref, ad_ref, gd_ref, yf_ref,
                 w0_ref, w2_ref, a0_ref, a2_ref, g2_ref, kk_ref, ka_ref, rk_ref, lnw_ref, lnb_ref,
                 ms_ref, mi_ref, same_ref, eye_ref, lev_ref, tri_ref, s0_ref,
                 y_ref, sout_ref,
                 S_ref, lw_s, kt_s, kn_s, b_s, yacc_s, *, reverse, final, nchunk):
    step = pl.program_id(2)

    @pl.when(step == 0)
    def _():
        S_ref[...] = s0_ref[0]

    same = same_ref[...]
    gsl = [slice(g * GROUP_W, (g + 1) * GROUP_W) for g in range(GROUPS_PER_STEP)]

    def head_sum(x):
        return jnp.concatenate([jnp.dot(x[:, s], same, precision=HIGHEST, preferred_element_type=F32)
                                for s in gsl], axis=1)

    k = k_ref[0]
    wraw = w0_ref[...] + jnp.dot(jnp.tanh(wd_ref[0]).astype(BF16), w2_ref[...],
                                 preferred_element_type=F32)
    lw_s[...] = -jax.nn.sigmoid(wraw) * math.exp(-0.5)
    a = jax.nn.sigmoid(a0_ref[...] + jnp.dot(ad_ref[0].astype(BF16), a2_ref[...],
                                             preferred_element_type=F32))
    kt_s[...] = k * (1.0 + (a - 1.0) * ka_ref[...])
    kn = k * kk_ref[...]
    kn = kn * lax.rsqrt(jnp.maximum(head_sum(kn * kn), 1e-24))
    kn_s[...] = kn
    b_s[...] = kn * a

    ms = ms_ref[...]
    mi = mi_ref[...]
    tri = tri_ref[...]
    mid = CHUNK // 2 if reverse else CHUNK // 2 - 1
    last = 0 if reverse else CHUNK - 1

    def stack(x):
        return (jnp.concatenate([x] * HEADS_PER_GROUP, axis=0) * same).astype(BF16)

    def tile(x):
        xb = x.astype(BF16)
        return jnp.concatenate([xb] * HEADS_PER_GROUP, axis=0)

    nt = (((1,), (1,)), ((), ()))
    tn = (((0,), (0,)), ((), ()))

    def chunk_group(rows, g):
        cols = gsl[g]
        lw = lw_s[rows, cols]
        kt = kt_s[rows, cols]
        kn = kn_s[rows, cols]
        b = b_s[rows, cols]
        r = r_ref[0, rows, cols]
        v = v_ref[0, rows, cols]
        c = jnp.dot(tri, lw, precision=HIGHEST, preferred_element_type=F32)
        yield
        cp = c - lw
        rho = c[mid:mid + 1, :]
        cend = c[last:last + 1, :]
        einv = jnp.exp(rho - c)
        eend = jnp.exp(cend - c)
        lhs = jnp.concatenate([stack(kn * jnp.exp(cp - rho)), stack(r * jnp.exp(c - rho))], axis=0)
        rhs = jnp.concatenate([tile(kt * einv), tile(b * einv)], axis=0)
        ag = lax.dot_general(lhs, rhs, nt, preferred_element_type=F32)
        yield
        ak = ag[:STACK, :STACK] * ms
        ab = ag[:STACK, STACK:] * ms
        gk = ag[STACK:, :STACK] * mi
        gb = ag[STACK:, STACK:] * mi
        tm = eye_ref[...] - ab * lev_ref[0]
        for lv in range(1, lev_ref.shape[0]):
            e = (ab * lev_ref[lv]).astype(BF16)
            tmb = tm.astype(BF16)
            te = jnp.dot(tmb, e, preferred_element_type=F32).astype(BF16)
            yield
            tm = tm - jnp.dot(te, tmb, preferred_element_type=F32)
            yield
        S = S_ref[g]
        lhs_s = jnp.concatenate([stack(kn * jnp.exp(cp)), stack(r * jnp.exp(c))], axis=0)
        qs = lax.dot_general(lhs_s, S.astype(BF16), nt, preferred_element_type=F32)
        vs = stack(v)
        rhs_u = qs[:STACK] + jnp.dot(ak.astype(BF16), vs, preferred_element_type=F32)
        yield
        us = jnp.dot(tm.astype(BF16), rhs_u.astype(BF16), preferred_element_type=F32)
        yield
        vu = jnp.concatenate([vs, (-us).astype(BF16)], axis=0)
        ys = qs[STACK:] + jnp.dot(jnp.concatenate([gk, gb], axis=1).astype(BF16), vu,
                                  preferred_element_type=F32)
        yield
        y = ys[0:CHUNK]
        for h in range(1, HEADS_PER_GROUP):
            y = y + ys[h * CHUNK:(h + 1) * CHUNK]
        if final:
            yacc_s[rows, cols] = y
        else:
            y_ref[0, rows, cols] = y
        kb = jnp.concatenate([tile(kt * eend), tile(b * eend)], axis=0)
        ds = lax.dot_general(vu, kb, tn, preferred_element_type=F32)
        S_ref[g] = S * jnp.exp(cend) + ds * same

    def chunk(ci, carry):
        c_idx = (nchunk - 1 - ci) if reverse else ci
        rows = pl.ds(pl.multiple_of(c_idx * CHUNK, CHUNK), CHUNK)
        pending = [chunk_group(rows, g) for g in range(GROUPS_PER_STEP)]
        while pending:
            pending = [gen for gen in pending if next(gen, True) is None]
        return carry

    lax.fori_loop(0, nchunk, chunk, 0)
    sout_ref[0] = S_ref[...]

    if final:
        y = yf_ref[0] + yacc_s[...]
        inv_n = 1.0 / RW_N
        mu = head_sum(y) * inv_n
        d = y - mu
        var = head_sum(d * d) * inv_n
        yn = d * lax.rsqrt(var + RW_GN_EPS) * lnw_ref[...] + lnb_ref[...]
        bonus = head_sum(r_ref[0] * k * rk_ref[...]) * v_ref[0]
        g = jnp.dot(jax.nn.sigmoid(gd_ref[0]).astype(BF16), g2_ref[...], preferred_element_type=F32)
        y_ref[0] = (yn + bonus) * g


def _pad_rows(w, rows):
    return jnp.pad(w, ((0, rows - w.shape[0]), (0, 0)))


def _rwkv_direction(mix, yf, prm, s0, d, *, final):
    b, L, _ = mix.shape
    reverse = d == 1
    tb = min(L, RWKV_TB)
    nb = L // tb
    nchunk = tb // CHUNK
    ms, mi, same, eye, lev, tri = _rwkv_consts(reverse)
    blk = (lambda i: nb - 1 - i) if reverse else (lambda i: i)
    sm = COL_SMALL // LANES

    def col(c0):
        return pl.BlockSpec((1, tb, STEP_W), lambda bi, g, i: (bi, blk(i), c0 // STEP_W + g))

    def small(j):
        return pl.BlockSpec((1, tb, LANES), lambda bi, g, i: (bi, blk(i), sm + j))

    vec = pl.BlockSpec((1, STEP_W), lambda bi, g, i: (0, g))
    lora = pl.BlockSpec((LANES, STEP_W), lambda bi, g, i: (0, g))
    const2 = lambda shp: pl.BlockSpec(shp, lambda bi, g, i: (0,) * len(shp))
    state = pl.BlockSpec((1, GROUPS_PER_STEP, GROUP_W, GROUP_W), lambda bi, g, i: (bi, g, 0, 0))
    yspec = pl.BlockSpec((1, tb, STEP_W), lambda bi, g, i: (bi, blk(i), g))

    kern = functools.partial(_rwkv_kernel, reverse=reverse, final=final, nchunk=nchunk)
    if yf is None:
        yf, yf_spec = mix, col(COL_K)
    else:
        yf_spec = yspec
    y, s_fin = pl.pallas_call(
        kern,
        grid=(b, N_GROUPS // GROUPS_PER_STEP, nb),
        in_specs=[col(COL_K), col(COL_V), col(COL_R), small(d), small(2 + d), small(4), yf_spec,
                  vec, lora, vec, lora, lora, vec, vec, vec, vec, vec,
                  const2((STACK, STACK)), const2((STACK, STACK)), const2((STACK, STACK)),
                  const2((STACK, STACK)), const2(tuple(lev.shape)), const2((CHUNK, CHUNK)), state],
        out_specs=[yspec, state],
        out_shape=[jax.ShapeDtypeStruct((b, L, RW_D), F32),
                   jax.ShapeDtypeStruct((b, N_GROUPS, GROUP_W, GROUP_W), F32)],
        scratch_shapes=[pltpu.VMEM((GROUPS_PER_STEP, GROUP_W, GROUP_W), F32)]
        + [pltpu.VMEM((tb, STEP_W), F32)] * 5,
        compiler_params=_cparams(("arbitrary", "arbitrary", "arbitrary")),
        name="rwkv_bwd" if reverse else "rwkv_fwd",
    )(mix, mix, mix, mix, mix, mix, yf,
      prm["w0"][d], prm["w2"][d], prm["a0"][d], prm["a2"][d], prm["g2"], prm["k_k"], prm["k_a"],
      prm["r_k"], prm["ln_w"], prm["ln_b"],
      ms, mi, same, eye, lev, tri, s0)
    return y, s_fin


def _rwkv_params(l, rw_w0, rw_w2, rw_a0, rw_a2, rw_g2, rw_k_k, rw_k_a, rw_r_k, rw_ln_w, rw_ln_b):
    row = lambda v: v.reshape(1, RW_D)
    return {
        "w0": [row(rw_w0[l, d]) for d in range(2)],
        "w2": [_pad_rows(rw_w2[l, d], LANES).astype(BF16) for d in range(2)],
        "a0": [row(rw_a0[l, d]) for d in range(2)],
        "a2": [_pad_rows(rw_a2[l, d], LANES).astype(BF16) for d in range(2)],
        "g2": _pad_rows(rw_g2[l], LANES).astype(BF16),
        "k_k": row(rw_k_k[l]), "k_a": row(rw_k_a[l]), "r_k": row(rw_r_k[l]),
        "ln_w": row(rw_ln_w[l]), "ln_b": row(rw_ln_b[l]),
    }


def _rwkv_mixer(mix, prm, s0_f, s0_b):
    b, L, _ = mix.shape
    y_f, s_f = _rwkv_direction(mix, None, prm, s0_f, 0, final=False)
    y, s_b = _rwkv_direction(mix, y_f, prm, s0_b, 1, final=True)
    return y, s_f, s_b


def _matmul_kernel(a_ref, b_ref, o_ref):
    o_ref[...] = jnp.dot(a_ref[...].astype(BF16), b_ref[...].astype(BF16), preferred_element_type=F32)


def _matmul(a, b, tn):
    m, k = a.shape
    n = b.shape[1]
    return pl.pallas_call(
        _matmul_kernel,
        grid=(n // tn,),
        in_specs=[pl.BlockSpec((m, k), lambda j: (0, 0)), pl.BlockSpec((k, tn), lambda j: (0, j))],
        out_specs=pl.BlockSpec((m, tn), lambda j: (0, j)),
        out_shape=jax.ShapeDtypeStruct((m, n), F32),
        compiler_params=_cparams(("arbitrary",)),
        name="matmul",
    )(a, b)


def _modulate(x, g, scale, shift):
    y = x * lax.rsqrt(jnp.mean(x * x, axis=-1, keepdims=True) + NORM_EPS)
    return y * g * (1.0 + scale) + shift


def _pack_cols(m):
    z = lambda n: jnp.zeros(m.shape[:-1] + (n,), m.dtype)
    s = lambda a, b: m[..., a:b]
    o = 2 * RW_D
    lo = [s(o + j * W_LORA, o + (j + 1) * W_LORA) for j in range(4)]
    r0 = o + 2 * W_LORA + 2 * A_LORA
    g0 = r0 + RW_D
    h0 = g0 + G_LORA
    parts = [s(0, RW_D), s(RW_D, 2 * RW_D), s(r0, r0 + RW_D)]
    if m.shape[-1] > h0 + 3 * HY_D:
        parts.append(s(h0 + 3 * HY_D, h0 + 3 * HY_D + 2 * D_MODEL))
    else:
        parts.append(z(2 * D_MODEL))
    for p in lo:
        parts += [p, z(LANES - W_LORA)]
    parts += [s(g0, g0 + G_LORA), z(LANES - G_LORA), z(SMALL_W - 5 * LANES)]
    parts.append(s(h0, h0 + 3 * HY_D))
    return jnp.concatenate(parts, axis=-1)


def _inproj_kernel(x_ref, g_ref, sc_ref, sh_ref, w_ref, cw_ref, o_ref, h_s, *, seg, gate_tiles):
    j = pl.program_id(2)

    @pl.when(j == 0)
    def _():
        h_s[...] = _modulate(x_ref[0], g_ref[...], sc_ref[0], sh_ref[0]).astype(BF16)

    p = jnp.dot(h_s[...], w_ref[...], preferred_element_type=F32)
    is_gate = jnp.logical_and(j >= gate_tiles[0], j < gate_tiles[1])
    out_blk = (0,) * (len(o_ref.shape) - 2)

    @pl.when(is_gate)
    def _():
        o_ref[out_blk] = jax.nn.sigmoid(p)

    @pl.when(jnp.logical_not(is_gate))
    def _():
        tm = p.shape[0]
        row = lax.broadcasted_iota(jnp.int32, p.shape, 0) % seg
        prev = jnp.where(row == 0, 0.0, pltpu.roll(p, 1, 0))
        nxt = jnp.where(row == seg - 1, 0.0, pltpu.roll(p, tm - 1, 0))
        o_ref[out_blk] = cw_ref[0:1, :] * prev + cw_ref[1:2, :] * p + cw_ref[2:3, :] * nxt


def _in_proj(x, g, scale, shift, w_packed, cw_packed, seg, hyena):
    b, L, d = x.shape
    tm = min(L, 512)
    if hyena:
        t0, nt = COL_HY // PROJ_TN, 3 * HY_D // PROJ_TN
        per = HY_D // PROJ_TN
        out_spec = pl.BlockSpec((1, 1, tm, PROJ_TN), lambda bi, i, j: (j // per, bi, i, j % per))
        out_shape = jax.ShapeDtypeStruct((3, b, L, HY_D), F32)
        gate_tiles = (0, 0)
    else:
        t0, nt = 0, COL_HY // PROJ_TN
        out_spec = pl.BlockSpec((1, tm, PROJ_TN), lambda bi, i, j: (bi, i, j))
        out_shape = jax.ShapeDtypeStruct((b, L, COL_HY), F32)
        gate_tiles = (COL_GATE // PROJ_TN, COL_SMALL // PROJ_TN)
    return pl.pallas_call(
        functools.partial(_inproj_kernel, seg=seg, gate_tiles=gate_tiles),
        grid=(b, L // tm, nt),
        in_specs=[pl.BlockSpec((1, tm, d), lambda bi, i, j: (bi, i, 0)),
                  pl.BlockSpec((1, d), lambda bi, i, j: (0, 0)),
                  pl.BlockSpec((1, 1, d), lambda bi, i, j: (bi, 0, 0)),
                  pl.BlockSpec((1, 1, d), lambda bi, i, j: (bi, 0, 0)),
                  pl.BlockSpec((d, PROJ_TN), lambda bi, i, j: (0, t0 + j)),
                  pl.BlockSpec((3, PROJ_TN), lambda bi, i, j: (0, t0 + j))],
        out_specs=out_spec,
        out_shape=out_shape,
        scratch_shapes=[pltpu.VMEM((tm, d), BF16)],
        compiler_params=_cparams(("arbitrary", "arbitrary", "arbitrary")),
        name="in_proj_hy" if hyena else "in_proj",
    )(x, g, scale, shift, w_packed, cw_packed)


def _outproj_kernel(yrw_ref, yhy_ref, grw_ref, ghy_ref, x_ref, ga_ref, w_ref, o_ref):
    y = grw_ref[0] * yrw_ref[0] + ghy_ref[0] * yhy_ref[0]
    o_ref[0] = x_ref[0] + ga_ref[0] * jnp.dot(y.astype(BF16), w_ref[...], preferred_element_type=F32)


def _out_proj(y_rw, y_hy, mix, x, ga, w_out_bf16):
    b, L, d = x.shape
    tm = 256
    tok = pl.BlockSpec((1, tm, d), lambda bi, i: (bi, i, 0))
    gcol = COL_GATE // d
    return pl.pallas_call(
        _outproj_kernel,
        grid=(b, L // tm),
        in_specs=[tok, tok,
                  pl.BlockSpec((1, tm, d), lambda bi, i: (bi, i, gcol)),
                  pl.BlockSpec((1, tm, d), lambda bi, i: (bi, i, gcol + 1)),
                  tok,
                  pl.BlockSpec((1, 1, d), lambda bi, i: (bi, 0, 0)),
                  pl.BlockSpec((d, d), lambda bi, i: (0, 0))],
        out_specs=tok,
        out_shape=jax.ShapeDtypeStruct((b, L, d), F32),
        compiler_params=_cparams(("arbitrary", "arbitrary")),
        name="out_proj",
    )(y_rw, y_hy, mix, mix, x, ga, w_out_bf16)


PEER_TM = 512
PEER_TE = 512
PEER_SUB = 64
STAT_ROWS = 8


def _top_rows(s, n):
    rows = lax.broadcasted_iota(jnp.int32, (n, s.shape[1]), 0)
    out = jnp.zeros((n, s.shape[1]), F32)
    cur = s
    for i in range(n):
        m = jnp.max(cur, axis=0, keepdims=True)
        out = jnp.where(rows == i, m, out)
        cur = jnp.where(cur == m, -jnp.inf, cur)
    return out


def _route_kernel(x_ref, g_ref, sc_ref, sh_ref, wqt_ref, keys_ref, h_ref, s1_ref, c1_ref, s2_ref, st_ref):
    hb = _modulate(x_ref[0], g_ref[...], sc_ref[0], sh_ref[0]).astype(BF16)
    h_ref[0] = hb
    nt = (((1,), (1,)), ((), ()))
    qt = lax.dot_general(wqt_ref[...], hb, nt, preferred_element_type=F32)
    half = D_KEY // 2
    for hd in range(P_HEADS):
        sc = []
        for p in range(2):
            r0 = (hd * 2 + p) * half
            sc.append(jnp.dot(keys_ref[hd, p], qt[r0:r0 + half, :].astype(BF16),
                              preferred_element_type=F32))
        s1_ref[0, hd] = sc[0]
        s2_ref[0, hd] = sc[1]
        t1 = _top_rows(sc[0], P_TOPK)
        t2 = _top_rows(sc[1], P_TOPK)
        r8 = lax.broadcasted_iota(jnp.int32, (8, t1.shape[1]), 0)
        tiles = [t1[0:1] + t2[0:8], t1[0:1] + t2[8:16], t1[1:2] + t2[0:8]]
        for i in range(2, 8):
            tiles.append(jnp.where(r8 < P_TOPK // (i + 1), t1[i:i + 1] + t2[0:8], -jnp.inf))
        tiles.append(t1[8:16] + t2[0:1])
        best = _top_rows(jnp.concatenate(tiles, axis=0), P_TOPK)
        z = jnp.sum(jnp.exp(best - best[0:1]), axis=0, keepdims=True)
        c1_ref[0, hd] = jnp.exp(sc[0] - t1[0:1]) * (1.0 / z)
        st = jnp.where(r8 == 0, best[P_TOPK - 1:P_TOPK], 0.0)
        st_ref[0, hd] = jnp.where(r8 == 1, t2[0:1], st)


def _peer_route(x, g, scale, shift, wqt_bf16, keys_bf16):
    b, L, d = x.shape
    tm = min(L, PEER_TM)
    tok = lambda: pl.BlockSpec((1, tm, d), lambda bi, i: (bi, i, 0))
    rowv = pl.BlockSpec((1, 1, d), lambda bi, i: (bi, 0, 0))
    sspec = pl.BlockSpec((1, P_HEADS, N_KEYS, tm), lambda bi, i: (bi, 0, 0, i))
    return pl.pallas_call(
        _route_kernel,
        grid=(b, L // tm),
        in_specs=[tok(), pl.BlockSpec((1, d), lambda bi, i: (0, 0)), rowv, rowv,
                  pl.BlockSpec(wqt_bf16.shape, lambda bi, i: (0, 0)),
                  pl.BlockSpec(keys_bf16.shape, lambda bi, i: (0, 0, 0, 0))],
        out_specs=[tok(), sspec, sspec, sspec,
                   pl.BlockSpec((1, P_HEADS, STAT_ROWS, tm), lambda bi, i: (bi, 0, 0, i))],
        out_shape=[jax.ShapeDtypeStruct((b, L, d), BF16),
                   jax.ShapeDtypeStruct((b, P_HEADS, N_KEYS, L), F32),
                   jax.ShapeDtypeStruct((b, P_HEADS, N_KEYS, L), F32),
                   jax.ShapeDtypeStruct((b, P_HEADS, N_KEYS, L), F32),
                   jax.ShapeDtypeStruct((b, P_HEADS, STAT_ROWS, L), F32)],
        compiler_params=_cparams(("arbitrary", "arbitrary")),
        name="peer_route",
    )(x, g, scale, shift, wqt_bf16, keys_bf16)


def _peer_dense_kernel(h_ref, u_ref, vt_ref, s1_ref, c1_ref, s2_ref, st_ref, x_ref, ga_ref, o_ref,
                       acc_s, e2_s, ge_s, w_s):
    e = pl.program_id(2)
    ne = pl.num_programs(2)
    tm = h_ref.shape[1]

    @pl.when(e == 0)
    def _():
        acc_s[...] = jnp.zeros_like(acc_s)
        for hd in range(P_HEADS):
            e2_s[hd] = jnp.exp(s2_ref[0, hd] - st_ref[0, hd, 1:2, :])

    nt = (((1,), (1,)), ((), ()))
    act = lax.dot_general(u_ref[...], h_ref[0], nt, preferred_element_type=F32)
    ge_s[...] = 0.5 * act * (1.0 + lax.erf(act * (1.0 / math.sqrt(2.0))))
    n1 = PEER_TE // N_KEYS
    for i in range(n1):
        i1 = e * n1 + i
        es = slice(i * N_KEYS, (i + 1) * N_KEYS)
        s1rows = [s1_ref[0, hd, pl.ds(i1, 1), :] for hd in range(P_HEADS)]
        c1rows = [c1_ref[0, hd, pl.ds(i1, 1), :] for hd in range(P_HEADS)]
        for strip in range(tm // LANES):
            cs = slice(strip * LANES, (strip + 1) * LANES)
            for sub in range(N_KEYS // PEER_SUB):
                ks = slice(sub * PEER_SUB, (sub + 1) * PEER_SUB)
                gsum = None
                for hd in range(P_HEADS):
                    sel = (s2_ref[0, hd, ks, cs] + s1rows[hd][:, cs]) >= st_ref[0, hd, 0:1, cs]
                    term = jnp.where(sel, e2_s[hd, ks, cs] * c1rows[hd][:, cs], 0.0)
                    gsum = term if gsum is None else gsum + term
                ws = slice(i * N_KEYS + sub * PEER_SUB, i * N_KEYS + (sub + 1) * PEER_SUB)
                w_s[ws, cs] = (gsum * ge_s[ws, cs]).astype(BF16)
    acc_s[...] += jnp.dot(vt_ref[...], w_s[...], preferred_element_type=F32)

    @pl.when(e == ne - 1)
    def _():
        o_ref[0] = x_ref[0] + ga_ref[0] * acc_s[...].T


def _peer_dense(h2, u_bf16, vt_bf16, s1, c1, s2, st, x, ga):
    b, L, d = x.shape
    tm = min(L, PEER_TM)
    tok = lambda: pl.BlockSpec((1, tm, d), lambda bi, i, e: (bi, i, 0))
    sspec = lambda: pl.BlockSpec((1, P_HEADS, N_KEYS, tm), lambda bi, i, e: (bi, 0, 0, i))
    return pl.pallas_call(
        _peer_dense_kernel,
        grid=(b, L // tm, N_EXPERTS // PEER_TE),
        in_specs=[tok(),
                  pl.BlockSpec((PEER_TE, d), lambda bi, i, e: (e, 0)),
                  pl.BlockSpec((d, PEER_TE), lambda bi, i, e: (0, e)),
                  sspec(), sspec(), sspec(),
                  pl.BlockSpec((1, P_HEADS, STAT_ROWS, tm), lambda bi, i, e: (bi, 0, 0, i)),
                  tok(),
                  pl.BlockSpec((1, 1, d), lambda bi, i, e: (bi, 0, 0))],
        out_specs=tok(),
        out_shape=jax.ShapeDtypeStruct((b, L, d), F32),
        scratch_shapes=[pltpu.VMEM((d, tm), F32), pltpu.VMEM((P_HEADS, N_KEYS, tm), F32),
                        pltpu.VMEM((PEER_TE, tm), F32), pltpu.VMEM((PEER_TE, tm), BF16)],
        compiler_params=pltpu.CompilerParams(
            dimension_semantics=("arbitrary", "arbitrary", "arbitrary"),
            vmem_limit_bytes=56 * 1024 * 1024),
        name="peer_dense",
    )(h2, u_bf16, vt_bf16, s1, c1, s2, st, x, ga)


def _peer_block(x, g, scale, shift, ga, pw):
    h2, s1, c1, s2, st = _peer_route(x, g, scale, shift, pw["wqt"], pw["keys"])
    return _peer_dense(h2, pw["u"], pw["vt"], s1, c1, s2, st, x, ga)


def _final_norm_kernel(x_ref, g_ref, o_ref):
    x = x_ref[0]
    o_ref[0] = x * lax.rsqrt(jnp.mean(x * x, axis=-1, keepdims=True) + NORM_EPS) * g_ref[...]


def _final_norm(x, g):
    b, L, d = x.shape
    tm = 512
    tok = pl.BlockSpec((1, tm, d), lambda bi, i: (bi, i, 0))
    return pl.pallas_call(
        _final_norm_kernel,
        grid=(b, L // tm),
        in_specs=[tok, pl.BlockSpec((1, d), lambda bi, i: (0, 0))],
        out_specs=tok,
        out_shape=jax.ShapeDtypeStruct((b, L, d), F32),
        compiler_params=_cparams(("arbitrary", "arbitrary")),
        name="final_norm",
    )(x, g.reshape(1, d))


HY_N2 = 256
HY_CT = 1024
HY_LT = 8192


def _hyena_filter(L, hy_p):
    w1, b1, w2, b2, w3, freq = hy_p
    t = jnp.linspace(0.0, 1.0, L, dtype=F32)[:, None]
    bands = jnp.linspace(1e-4, HY_BANDS - 1, HY_BANDS, dtype=F32)[None, :]
    ang = (2.0 * math.pi / L) * jnp.arange(L, dtype=F32)[:, None] * bands
    z = jnp.concatenate([t, jnp.cos(ang), -jnp.sin(ang)], axis=-1)
    hdn = jnp.sin(freq * (z @ w1 + b1))
    hdn = jnp.sin(freq * (hdn @ w2 + b2))
    h = jnp.einsum("lh,hosc->oslc", hdn, w3.reshape(HY_HIDDEN, HY_ORDER, 2, HY_D))
    deltas = jnp.abs(jnp.linspace(math.log(HY_TARGET) / HY_SLOW_PCT, math.log(HY_TARGET) / HY_FAST_PCT,
                                  HY_D, dtype=F32))
    h = h * jnp.exp(-t * deltas)[None, None, :, :]
    h = h * lax.rsqrt(jnp.sum(h * h, axis=(1, 2), keepdims=True))
    fwd, bwd = h[:, 0], h[:, 1]
    return jnp.concatenate([fwd[:, :1] + bwd[:, :1], fwd[:, 1:], jnp.zeros_like(fwd[:, :1]),
                            bwd[:, :0:-1]], axis=1)


def _cis(idx, n):
    ang = (2.0 * math.pi / n) * (idx % n).astype(F32)
    return jnp.cos(ang), -jnp.sin(ang)


def _block_complex(re, im):
    return jnp.concatenate([jnp.concatenate([re, -im], axis=-1),
                            jnp.concatenate([im, re], axis=-1)], axis=-2)


def _dft_mats(n1, n2, kin):
    k = jnp.arange(n1, dtype=jnp.int32)
    c1, s1 = _cis(k[:, None] * k[None, :], n1)
    half = max(n1 // 2, 1)
    outer = _block_complex(c1[:, :half], s1[:, :half]).astype(BF16)
    outer_real = jnp.concatenate([c1, s1], axis=0).astype(BF16)
    m2 = jnp.arange(n2, dtype=jnp.int32)
    freq = k[:, None, None] + n1 * m2[None, :, None]
    gr, gi = _cis(m2[None, None, :] * freq, n1 * n2)
    inner = _block_complex(gr[:, :, :kin], gi[:, :, :kin]).astype(BF16)
    inner_real = jnp.concatenate([gr, gi], axis=1).astype(BF16)
    return {"outer": outer, "outer_real": outer_real, "inner": inner, "inner_real": inner_real}


def _hy_outer_fwd_kernel(x_ref, m_ref, o_ref):
    groups = x_ref.shape[1]
    x = jnp.concatenate([x_ref[0, g] for g in range(groups)], axis=0) if groups > 1 else x_ref[0, 0]
    f = jnp.dot(m_ref[...], x.astype(BF16), preferred_element_type=F32)
    n1 = f.shape[0] // 2
    o_ref[0] = f[:n1].astype(o_ref.dtype)
    o_ref[1] = f[n1:].astype(o_ref.dtype)


def _hy_outer_fwd(x, which, mat):
    _, groups, r, cols = x.shape
    n1 = mat.shape[0] // 2
    lt = min(cols, HY_LT)
    return pl.pallas_call(
        _hy_outer_fwd_kernel,
        grid=(cols // lt,),
        in_specs=[pl.BlockSpec((1, groups, r, lt), lambda j: (which, 0, 0, j)),
                  pl.BlockSpec(mat.shape, lambda j: (0, 0))],
        out_specs=pl.BlockSpec((2, n1, lt), lambda j: (0, 0, j)),
        out_shape=jax.ShapeDtypeStruct((2, n1, cols), BF16),
        compiler_params=_cparams(("arbitrary",)),
        name="hy_outer_fwd",
    )(x, mat)


def _hy_inner_kernel(*refs, conv, scale):
    if conv:
        a_ref, g_ref, s_ref, o_ref = refs
    else:
        a_ref, g_ref, o_ref = refs
    parts = [a_ref[p, 0] for p in range(a_ref.shape[0])]
    a = (jnp.concatenate(parts, axis=0) if len(parts) > 1 else parts[0]).astype(BF16)
    g = g_ref[0]
    f = jnp.dot(g, a, preferred_element_type=F32)
    ko = f.shape[0] // 2
    fr, fi = f[:ko], f[ko:]
    if not conv:
        o_ref[0, 0] = fr * scale
        o_ref[1, 0] = fi * scale
    else:
        sr, si = s_ref[0, 0], s_ref[1, 0]
        p = jnp.concatenate([fr * sr - fi * si, fr * si + fi * sr], axis=0).astype(BF16)
        d = lax.dot_general(g, p, (((0,), (0,)), ((), ())), preferred_element_type=F32)
        kin = d.shape[0] // 2
        o_ref[0, 0] = d[:kin].astype(o_ref.dtype)
        o_ref[1, 0] = d[kin:].astype(o_ref.dtype)


def _hy_inner(a, gmat, spec, out_dtype, scale=1.0):
    p, n1, kin, c = a.shape
    ko = gmat.shape[1] // 2
    conv = spec is not None
    ct = min(c, HY_CT)
    in_specs = [pl.BlockSpec((p, 1, kin, ct), lambda k, j: (0, k, 0, j)),
                pl.BlockSpec((1,) + gmat.shape[1:], lambda k, j: (k, 0, 0))]
    args = [a, gmat]
    if conv:
        in_specs.append(pl.BlockSpec((2, 1, ko, ct), lambda k, j: (0, k, 0, j)))
        args.append(spec)
    rows = kin if conv else ko
    return pl.pallas_call(
        functools.partial(_hy_inner_kernel, conv=conv, scale=scale),
        grid=(n1, c // ct),
        in_specs=in_specs,
        out_specs=pl.BlockSpec((2, 1, rows, ct), lambda k, j: (0, k, 0, j)),
        out_shape=jax.ShapeDtypeStruct((2, n1, rows, c), out_dtype),
        compiler_params=_cparams(("arbitrary", "arbitrary")),
        name="hy_inner_conv" if conv else "hy_inner_spec",
    )(*args)


def _hy_outer_inv_kernel(d_ref, m_ref, v_ref, x_ref, sk_ref, o_ref):
    d = jnp.concatenate([d_ref[0], d_ref[1]], axis=0)
    y = lax.dot_general(m_ref[...], d, (((0,), (0,)), ((), ())), preferred_element_type=F32)
    half = y.shape[0] // 2
    for b in range(2):
        o_ref[0, b] = x_ref[0, b] * (y[b * half:(b + 1) * half] + v_ref[0, b] * sk_ref[...])


def _hy_outer_inv(d, mat, src, si, gate, gi, skip_row):
    _, n1, cols = d.shape
    half = n1 // 2
    lt = min(cols, HY_LT)
    blk = lambda which: pl.BlockSpec((1, 2, half, lt), lambda j: (which, 0, 0, j))
    return pl.pallas_call(
        _hy_outer_inv_kernel,
        grid=(cols // lt,),
        in_specs=[pl.BlockSpec((2, n1, lt), lambda j: (0, 0, j)),
                  pl.BlockSpec(mat.shape, lambda j: (0, 0)),
                  blk(si), blk(gi),
                  pl.BlockSpec((1, lt), lambda j: (0, 0))],
        out_specs=blk(0),
        out_shape=jax.ShapeDtypeStruct((1, 2, half, cols), F32),
        compiler_params=_cparams(("arbitrary",)),
        name="hy_outer_inv",
    )(d, mat, src, gate, skip_row)


def _hyena_spectrum(g, mats):
    order, n, c = g.shape
    n1 = mats["outer_real"].shape[1]
    n2 = n // n1
    out = []
    for o in range(order):
        if n1 > 1:
            a = _hy_outer_fwd(g.reshape(order, 1, n1, n2 * c), o, mats["outer_real"])
            out.append(_hy_inner(a.reshape(2, n1, n2, c), mats["inner"], None, F32, 1.0 / n))
        else:
            out.append(_hy_inner(g[o].reshape(1, 1, n, c), mats["inner_real"], None, F32, 1.0 / n))
    return out


def _hy_gate_kernel(y_ref, v_ref, x_ref, sk_ref, o_ref):
    o_ref[...] = x_ref[...] * (y_ref[...] + v_ref[...] * sk_ref[...])


def _hy_gate(y, v, x, skip_row):
    b, L, c = y.shape
    tok = pl.BlockSpec((1, L, c), lambda bi: (bi, 0, 0))
    return pl.pallas_call(
        _hy_gate_kernel,
        grid=(b,),
        in_specs=[tok, tok, tok, pl.BlockSpec((1, c), lambda bi: (0, 0))],
        out_specs=tok,
        out_shape=jax.ShapeDtypeStruct((b, L, c), F32),
        compiler_params=_cparams(("arbitrary",)),
        name="hy_gate",
    )(y, v, x, skip_row)


def _hyena_mixer(hy, spec, skip, mats):
    _, b, L, c = hy.shape
    assert b == 2, "the two batch entries are packed as one complex signal"
    n1 = mats["outer_real"].shape[1]
    if n1 == 1:
        z = hy[0]
        for o in range(HY_ORDER):
            y = _hy_inner(z.reshape(2, 1, L, c), mats["inner"], spec[o], F32).reshape(b, L, c)
            z = _hy_gate(y, z, hy[o + 1], skip[o].reshape(1, c))
        return z
    n2 = 2 * L // n1
    half, cols = n1 // 2, n2 * c
    lt = min(cols, HY_LT)
    src, si = hy.reshape(3, b, half, cols), 0
    gate = src
    for o in range(HY_ORDER):
        a = _hy_outer_fwd(src, si, mats["outer"])
        d = _hy_inner(a.reshape(2, n1, n2, c), mats["inner"], spec[o], BF16)
        skip_row = jnp.tile(skip[o], lt // c).reshape(1, lt)
        src, si = _hy_outer_inv(d.reshape(2, n1, cols), mats["outer"], src, si, gate, o + 1, skip_row), 0
    return src.reshape(b, L, c)


def _token_mixing(x, g, scale, shift, ga, lw, n_seg, s0_f, s0_b, hy_p):
    b, L, _ = x.shape
    seg = L // n_seg
    mix = _in_proj(x, g, scale, shift, lw["w_in"], lw["conv_w"], seg, hyena=False)
    y_rw, s_f, s_b = _rwkv_mixer(mix, lw["rw"], s0_f, s0_b)
    if hy_p is None:
        return None, s_f, s_b
    n = 2 * L
    n1 = n // HY_N2 if n // HY_N2 >= 16 else 1
    mats = _dft_mats(n1, n // n1, L if n1 == 1 else n // n1)
    spec = _hyena_spectrum(_hyena_filter(L, hy_p), mats)
    hy = _in_proj(x, g, scale, shift, lw["w_in"], lw["conv_w"], seg, hyena=True)
    y_hy = _hyena_mixer(hy, spec, lw["hy_skip"], mats)
    return _out_proj(y_rw, y_hy, mix, x, ga, lw["w_out"]), s_f, s_b


def kernel(x, c, ctx, c_ctx, ada_w, ada_b, norm1_g, norm2_g, w_in, conv_w, rw_w0, rw_w2, rw_a0, rw_a2, rw_g2, rw_k_k, rw_k_a, rw_r_k, rw_ln_w, rw_ln_b, hy_w1, hy_b1, hy_w2, hy_b2, hy_w3, hy_freq, hy_skip, w_out, peer_wq, peer_keys, peer_u, peer_v, final_g):
    b, L, d = x.shape
    depth = ada_w.shape[0]
    ctx_len = ctx.shape[1]
    s_zero = jnp.zeros((b, N_GROUPS, GROUP_W, GROUP_W), F32)
    cond = jnp.concatenate([c, c_ctx[None, :]], axis=0)
    cond = jnp.pad(jax.nn.silu(cond), ((0, 8 - (b + 1)), (0, 0)))
    for l in range(depth):
        last = l == depth - 1
        mod = _matmul(cond, ada_w[l], 1024) + ada_b[l]
        mx = [m[:, None, :] for m in jnp.split(mod[:b], 6, axis=-1)]
        mc = [jnp.broadcast_to(m[None, :, :], (b, 1, d)) for m in jnp.split(mod[b:b + 1], 6, axis=-1)]
        lw = {
            "w_in": _pack_cols(w_in[l]).astype(BF16),
            "conv_w": _pack_cols(conv_w[l]),
            "rw": _rwkv_params(l, rw_w0, rw_w2, rw_a0, rw_a2, rw_g2, rw_k_k, rw_k_a, rw_r_k, rw_ln_w, rw_ln_b),
            "hy_skip": hy_skip[l],
            "w_out": w_out[l].astype(BF16),
        }
        hy_p = (hy_w1[l], hy_b1[l], hy_w2[l], hy_b2[l], hy_w3[l], hy_freq[l])
        pw = {
            "wqt": peer_wq[l].T.astype(BF16),
            "keys": peer_keys[l].astype(BF16),
            "u": peer_u[l].astype(BF16),
            "vt": peer_v[l].T.astype(BF16),
        }
        g1 = norm1_g[l].reshape(1, d)
        g2 = norm2_g[l].reshape(1, d)

        ctx_new, s_f, s_b = _token_mixing(ctx, g1, mc[1], mc[0], mc[2], lw, 1, s_zero, s_zero,
                                          None if last else hy_p)
        if not last:
            ctx = _peer_block(ctx_new, g2, mc[4], mc[3], mc[5], pw)

        x, _, _ = _token_mixing(x, g1, mx[1], mx[0], mx[2], lw, L // GRID_W, s_f, s_b, hy_p)
        x = _peer_block(x, g2, mx[4], mx[3], mx[5], pw)
    return _final_norm(x, final_g)
```

```python
import functools
import math

import numpy as np
import jax
import jax.numpy as jnp
from jax import lax
from jax.experimental import pallas as pl
from jax.experimental.pallas import tpu as pltpu

F32 = jnp.float32
BF16 = jnp.bfloat16

D_MODEL = 2048
GRID_W = 64
NORM_EPS = 1e-6
RW_N = 64
RW_H = D_MODEL // RW_N
RW_D = D_MODEL
W_LORA = 96
A_LORA = 96
G_LORA = 64
RW_GN_EPS = 64e-5
HY_D = D_MODEL
HY_ORDER = 2
HY_EMB = 33
HY_BANDS = (HY_EMB - 1) // 2
HY_HIDDEN = 64
HY_TARGET = 1e-2
HY_FAST_PCT = 0.3
HY_SLOW_PCT = 1.5
P_HEADS = 8
N_KEYS = 128
N_EXPERTS = N_KEYS * N_KEYS
P_TOPK = 16
D_KEY = 256

LANES = 128
MXU_DIM = 256
VMEM_LIMIT = 48 * 1024 * 1024

COL_K, COL_V, COL_R = 0, RW_D, 2 * RW_D
COL_SMALL = 3 * RW_D
SMALL_W = 1024
COL_GATE = COL_SMALL + SMALL_W
COL_HY = COL_GATE + 2 * D_MODEL
N_PACKED = COL_HY + 3 * HY_D
PROJ_TN = 512
PROJ_RC = 256

CHUNK = 64
HEADS_PER_GROUP = MXU_DIM // RW_N
GROUP_W = HEADS_PER_GROUP * RW_N
N_GROUPS = RW_D // GROUP_W
STACK = HEADS_PER_GROUP * CHUNK
GROUPS_PER_STEP = 4
STEP_W = GROUPS_PER_STEP * GROUP_W
RWKV_TB = 256


def _cparams(sem):
    return pltpu.CompilerParams(dimension_semantics=sem, vmem_limit_bytes=VMEM_LIMIT)


def _rwkv_consts(reverse):
    idx = np.arange(STACK)
    hb, t = idx // CHUNK, idx % CHUNK
    same = hb[:, None] == hb[None, :]
    tt, ss = t[:, None], t[None, :]
    strict = (tt < ss) if reverse else (tt > ss)
    eye = same & (tt == ss)
    ms = same & strict
    mi = ms | eye
    levels = []
    m = 1
    while m < CHUNK:
        levels.append(ms & ((tt // (2 * m)) == (ss // (2 * m))) & ((tt // m) != (ss // m)))
        m *= 2
    c = np.arange(CHUNK)
    tri = (c[:, None] <= c[None, :]) if reverse else (c[:, None] >= c[None, :])
    f = lambda a: jnp.asarray(a.astype(np.float32))
    return f(ms), f(mi), f(same), f(eye), f(np.stack(levels)), f(tri)


def _rwkv_kernel(k_ref, v_ref, r_ref, wd_ref, ad_ref, gd_ref, yf_ref,
                 w0_ref, w2_ref, a0_ref, a2_ref, g2_ref, kk_ref, ka_ref, rk_ref, lnw_ref, lnb_ref,
                 ms_ref, mi_ref, same_ref, eye_ref, lev_ref, tri_ref, s0_ref,
                 y_ref, sout_ref,
                 S_ref, lw_s, kt_s, kn_s, b_s, yacc_s, *, reverse, final, nchunk):
    step = pl.program_id(2)

    @pl.when(step == 0)
    def _():
        S_ref[...] = s0_ref[0]

    same = same_ref[...]
    gsl = [slice(g * GROUP_W, (g + 1) * GROUP_W) for g in range(GROUPS_PER_STEP)]

    same_b = same.astype(BF16)

    def split(x, pieces):
        out = []
        for _ in range(pieces):
            p = x.astype(BF16)
            out.append(p)
            x = x - p.astype(F32)
        return out

    def head_sum(x):
        cols = []
        for s in gsl:
            hi, lo = split(x[:, s], 2)
            cols.append(jnp.dot(hi, same_b, preferred_element_type=F32)
                        + jnp.dot(lo, same_b, preferred_element_type=F32))
        return jnp.concatenate(cols, axis=1)

    k = k_ref[0]
    wraw = w0_ref[...] + jnp.dot(jnp.tanh(wd_ref[0]).astype(BF16), w2_ref[...],
                                 preferred_element_type=F32)
    lw_s[...] = -jax.nn.sigmoid(wraw) * math.exp(-0.5)
    a = jax.nn.sigmoid(a0_ref[...] + jnp.dot(ad_ref[0].astype(BF16), a2_ref[...],
                                             preferred_element_type=F32))
    kt_s[...] = k * (1.0 + (a - 1.0) * ka_ref[...])
    kn = k * kk_ref[...]
    kn = kn * lax.rsqrt(jnp.maximum(head_sum(kn * kn), 1e-24))
    kn_s[...] = kn
    b_s[...] = kn * a

    ms = ms_ref[...]
    mi = mi_ref[...]
    tri = tri_ref[...].astype(BF16)
    mid = CHUNK // 2 if reverse else CHUNK // 2 - 1
    last = 0 if reverse else CHUNK - 1

    def stack(x):
        return (jnp.concatenate([x] * HEADS_PER_GROUP, axis=0) * same).astype(BF16)

    def tile(x):
        xb = x.astype(BF16)
        return jnp.concatenate([xb] * HEADS_PER_GROUP, axis=0)

    nt = (((1,), (1,)), ((), ()))
    tn = (((0,), (0,)), ((), ()))

    def chunk_group(rows, g):
        cols = gsl[g]
        lw = lw_s[rows, cols]
        kt = kt_s[rows, cols]
        kn = kn_s[rows, cols]
        b = b_s[rows, cols]
        r = r_ref[0, rows, cols]
        v = v_ref[0, rows, cols]
        c = sum(jnp.dot(tri, p, preferred_element_type=F32) for p in split(lw, 3))
        yield
        cp = c - lw
        rho = c[mid:mid + 1, :]
        cend = c[last:last + 1, :]
        einv = jnp.exp(rho - c)
        eend = jnp.exp(cend - c)
        lhs = jnp.concatenate([stack(kn * jnp.exp(cp - rho)), stack(r * jnp.exp(c - rho))], axis=0)
        rhs = jnp.concatenate([tile(kt * einv), tile(b * einv)], axis=0)
        ag = lax.dot_general(lhs, rhs, nt, preferred_element_type=F32)
        yield
        ak = ag[:STACK, :STACK] * ms
        ab = ag[:STACK, STACK:] * ms
        gk = ag[STACK:, :STACK] * mi
        gb = ag[STACK:, STACK:] * mi
        tm = eye_ref[...] - ab * lev_ref[0]
        for lv in range(1, lev_ref.shape[0]):
            e = (ab * lev_ref[lv]).astype(BF16)
            tmb = tm.astype(BF16)
            te = jnp.dot(tmb, e, preferred_element_type=F32).astype(BF16)
            yield
            tm = tm - jnp.dot(te, tmb, preferred_element_type=F32)
            yield
        S = S_ref[g]
        lhs_s = jnp.concatenate([stack(kn * jnp.exp(cp)), stack(r * jnp.exp(c))], axis=0)
        qs = lax.dot_general(lhs_s, S.astype(BF16), nt, preferred_element_type=F32)
        vs = stack(v)
        rhs_u = qs[:STACK] + jnp.dot(ak.astype(BF16), vs, preferred_element_type=F32)
        yield
        us = jnp.dot(tm.astype(BF16), rhs_u.astype(BF16), preferred_element_type=F32)
        yield
        vu = jnp.concatenate([vs, (-us).astype(BF16)], axis=0)
        ys = qs[STACK:] + jnp.dot(jnp.concatenate([gk, gb], axis=1).astype(BF16), vu,
                                  preferred_element_type=F32)
        yield
        y = ys[0:CHUNK]
        for h in range(1, HEADS_PER_GROUP):
            y = y + ys[h * CHUNK:(h + 1) * CHUNK]
        if final:
            yacc_s[rows, cols] = y
        else:
            y_ref[0, rows, cols] = y
        kb = jnp.concatenate([tile(kt * eend), tile(b * eend)], axis=0)
        ds = lax.dot_general(vu, kb, tn, preferred_element_type=F32)
        S_ref[g] = S * jnp.exp(cend) + ds * same

    def chunk(ci, carry):
        c_idx = (nchunk - 1 - ci) if reverse else ci
        rows = pl.ds(pl.multiple_of(c_idx * CHUNK, CHUNK), CHUNK)
        pending = [chunk_group(rows, g) for g in range(GROUPS_PER_STEP)]
        while pending:
            pending = [gen for gen in pending if next(gen, True) is None]
        return carry

    lax.fori_loop(0, nchunk, chunk, 0)
    sout_ref[0] = S_ref[...]

    if final:
        y = yf_ref[0] + yacc_s[...]
        inv_n = 1.0 / RW_N
        mu = head_sum(y) * inv_n
        d = y - mu
        var = head_sum(d * d) * inv_n
        yn = d * lax.rsqrt(var + RW_GN_EPS) * lnw_ref[...] + lnb_ref[...]
        bonus = head_sum(r_ref[0] * k * rk_ref[...]) * v_ref[0]
        g = jnp.dot(jax.nn.sigmoid(gd_ref[0]).astype(BF16), g2_ref[...], preferred_element_type=F32)
        y_ref[0] = (yn + bonus) * g


def _pad_rows(w, rows):
    return jnp.pad(w, ((0, rows - w.shape[0]), (0, 0)))


def _rwkv_direction(mix, yf, prm, s0, d, *, final):
    b, L, _ = mix.shape
    reverse = d == 1
    tb = min(L, RWKV_TB)
    nb = L // tb
    nchunk = tb // CHUNK
    ms, mi, same, eye, lev, tri = _rwkv_consts(reverse)
    blk = (lambda i: nb - 1 - i) if reverse else (lambda i: i)
    sm = COL_SMALL // LANES

    def col(c0):
        return pl.BlockSpec((1, tb, STEP_W), lambda bi, g, i: (bi, blk(i), c0 // STEP_W + g))

    def small(j):
        return pl.BlockSpec((1, tb, LANES), lambda bi, g, i: (bi, blk(i), sm + j))

    vec = pl.BlockSpec((1, STEP_W), lambda bi, g, i: (0, g))
    lora = pl.BlockSpec((LANES, STEP_W), lambda bi, g, i: (0, g))
    const2 = lambda shp: pl.BlockSpec(shp, lambda bi, g, i: (0,) * len(shp))
    state = pl.BlockSpec((1, GROUPS_PER_STEP, GROUP_W, GROUP_W), lambda bi, g, i: (bi, g, 0, 0))
    yspec = pl.BlockSpec((1, tb, STEP_W), lambda bi, g, i: (bi, blk(i), g))

    kern = functools.partial(_rwkv_kernel, reverse=reverse, final=final, nchunk=nchunk)
    if yf is None:
        yf, yf_spec = mix, col(COL_K)
    else:
        yf_spec = yspec
    y, s_fin = pl.pallas_call(
        kern,
        grid=(b, N_GROUPS // GROUPS_PER_STEP, nb),
        in_specs=[col(COL_K), col(COL_V), col(COL_R), small(d), small(2 + d), small(4), yf_spec,
                  vec, lora, vec, lora, lora, vec, vec, vec, vec, vec,
                  const2((STACK, STACK)), const2((STACK, STACK)), const2((STACK, STACK)),
                  const2((STACK, STACK)), const2(tuple(lev.shape)), const2((CHUNK, CHUNK)), state],
        out_specs=[yspec, state],
        out_shape=[jax.ShapeDtypeStruct((b, L, RW_D), F32),
                   jax.ShapeDtypeStruct((b, N_GROUPS, GROUP_W, GROUP_W), F32)],
        scratch_shapes=[pltpu.VMEM((GROUPS_PER_STEP, GROUP_W, GROUP_W), F32)]
        + [pltpu.VMEM((tb, STEP_W), F32)] * 5,
        compiler_params=_cparams(("arbitrary", "arbitrary", "arbitrary")),
        name="rwkv_bwd" if reverse else "rwkv_fwd",
    )(mix, mix, mix, mix, mix, mix, yf,
      prm["w0"][d], prm["w2"][d], prm["a0"][d], prm["a2"][d], prm["g2"], prm["k_k"], prm["k_a"],
      prm["r_k"], prm["ln_w"], prm["ln_b"],
      ms, mi, same, eye, lev, tri, s0)
    return y, s_fin


def _rwkv_params(l, rw_w0, rw_w2, rw_a0, rw_a2, rw_g2, rw_k_k, rw_k_a, rw_r_k, rw_ln_w, rw_ln_b):
    row = lambda v: v.reshape(1, RW_D)
    return {
        "w0": [row(rw_w0[l, d]) for d in range(2)],
        "w2": [_pad_rows(rw_w2[l, d], LANES).astype(BF16) for d in range(2)],
        "a0": [row(rw_a0[l, d]) for d in range(2)],
        "a2": [_pad_rows(rw_a2[l, d], LANES).astype(BF16) for d in range(2)],
        "g2": _pad_rows(rw_g2[l], LANES).astype(BF16),
        "k_k": row(rw_k_k[l]), "k_a": row(rw_k_a[l]), "r_k": row(rw_r_k[l]),
        "ln_w": row(rw_ln_w[l]), "ln_b": row(rw_ln_b[l]),
    }


def _rwkv_mixer(mix, prm, s0_f, s0_b):
    b, L, _ = mix.shape
    y_f, s_f = _rwkv_direction(mix, None, prm, s0_f, 0, final=False)
    y, s_b = _rwkv_direction(mix, y_f, prm, s0_b, 1, final=True)
    return y, s_f, s_b


def _matmul_kernel(a_ref, b_ref, o_ref):
    o_ref[...] = jnp.dot(a_ref[...].astype(BF16), b_ref[...].astype(BF16), preferred_element_type=F32)


def _matmul(a, b, tn):
    m, k = a.shape
    n = b.shape[1]
    return pl.pallas_call(
        _matmul_kernel,
        grid=(n // tn,),
        in_specs=[pl.BlockSpec((m, k), lambda j: (0, 0)), pl.BlockSpec((k, tn), lambda j: (0, j))],
        out_specs=pl.BlockSpec((m, tn), lambda j: (0, j)),
        out_shape=jax.ShapeDtypeStruct((m, n), F32),
        compiler_params=_cparams(("arbitrary",)),
        name="matmul",
    )(a, b)


def _modulate(x, g, scale, shift):
    y = x * lax.rsqrt(jnp.mean(x * x, axis=-1, keepdims=True) + NORM_EPS)
    return y * g * (1.0 + scale) + shift


def _pack_cols(m):
    z = lambda n: jnp.zeros(m.shape[:-1] + (n,), m.dtype)
    s = lambda a, b: m[..., a:b]
    o = 2 * RW_D
    lo = [s(o + j * W_LORA, o + (j + 1) * W_LORA) for j in range(4)]
    r0 = o + 2 * W_LORA + 2 * A_LORA
    g0 = r0 + RW_D
    h0 = g0 + G_LORA
    parts = [s(0, RW_D), s(RW_D, 2 * RW_D), s(r0, r0 + RW_D)]
    for p in lo:
        parts += [p, z(LANES - W_LORA)]
    parts += [s(g0, g0 + G_LORA), z(LANES - G_LORA), z(SMALL_W - 5 * LANES)]
    if m.shape[-1] > h0 + 3 * HY_D:
        parts.append(s(h0 + 3 * HY_D, h0 + 3 * HY_D + 2 * D_MODEL))
    else:
        parts.append(z(2 * D_MODEL))
    parts.append(s(h0, h0 + 3 * HY_D))
    return jnp.concatenate(parts, axis=-1)


def _inproj_kernel(x_ref, g_ref, sc_ref, sh_ref, w_ref, cw_ref, o_ref, h_s, *, seg, gate):
    j = pl.program_id(2)

    @pl.when(j == 0)
    def _():
        h_s[...] = _modulate(x_ref[0], g_ref[...], sc_ref[0], sh_ref[0]).astype(BF16)

    out_blk = (0,) * (len(o_ref.shape) - 2)
    tm = h_s.shape[0]
    rc = min(tm, max(seg, PROJ_RC))
    for r0 in range(0, tm, rc):
        p = jnp.dot(h_s[r0:r0 + rc, :], w_ref[...], preferred_element_type=F32)
        if gate:
            res = jax.nn.sigmoid(p)
        else:
            row = lax.broadcasted_iota(jnp.int32, p.shape, 0) % seg
            prev = jnp.where(row == 0, 0.0, pltpu.roll(p, 1, 0))
            nxt = jnp.where(row == seg - 1, 0.0, pltpu.roll(p, rc - 1, 0))
            res = cw_ref[0:1, :] * prev + cw_ref[1:2, :] * p + cw_ref[2:3, :] * nxt
        o_ref[out_blk + (slice(r0, r0 + rc), slice(None))] = res


def _in_proj(x, g, scale, shift, w_packed, cw_packed, seg, part):
    b, L, d = x.shape
    tm = min(L, 1024)
    c0, c1 = {"rw": (0, COL_GATE), "gate": (COL_GATE, COL_HY), "hy": (COL_HY, N_PACKED)}[part]
    t0, nt = c0 // PROJ_TN, (c1 - c0) // PROJ_TN
    if part == "hy":
        per = HY_D // PROJ_TN
        out_spec = pl.BlockSpec((1, 1, tm, PROJ_TN), lambda bi, i, j: (j // per, bi, i, j % per))
        out_shape = jax.ShapeDtypeStruct((3, b, L, HY_D), F32)
    else:
        out_spec = pl.BlockSpec((1, tm, PROJ_TN), lambda bi, i, j: (bi, i, j))
        out_shape = jax.ShapeDtypeStruct((b, L, c1 - c0), F32)
    return pl.pallas_call(
        functools.partial(_inproj_kernel, seg=seg, gate=part == "gate"),
        grid=(b, L // tm, nt),
        in_specs=[pl.BlockSpec((1, tm, d), lambda bi, i, j: (bi, i, 0)),
                  pl.BlockSpec((1, d), lambda bi, i, j: (0, 0)),
                  pl.BlockSpec((1, 1, d), lambda bi, i, j: (bi, 0, 0)),
                  pl.BlockSpec((1, 1, d), lambda bi, i, j: (bi, 0, 0)),
                  pl.BlockSpec((d, PROJ_TN), lambda bi, i, j: (0, t0 + j)),
                  pl.BlockSpec((3, PROJ_TN), lambda bi, i, j: (0, t0 + j))],
        out_specs=out_spec,
        out_shape=out_shape,
        scratch_shapes=[pltpu.VMEM((tm, d), BF16)],
        compiler_params=_cparams(("arbitrary", "arbitrary", "arbitrary")),
        name="in_proj_" + part,
    )(x, g, scale, shift, w_packed, cw_packed)


def _outproj_kernel(yrw_ref, yhy_ref, grw_ref, ghy_ref, x_ref, ga_ref, w_ref, o_ref):
    y = grw_ref[0] * yrw_ref[0] + ghy_ref[0] * yhy_ref[0]
    o_ref[0] = x_ref[0] + ga_ref[0] * jnp.dot(y.astype(BF16), w_ref[...], preferred_element_type=F32)


def _out_proj(y_rw, y_hy, gates, x, ga, w_out_bf16):
    b, L, d = x.shape
    tm = 256
    tok = pl.BlockSpec((1, tm, d), lambda bi, i: (bi, i, 0))
    return pl.pallas_call(
        _outproj_kernel,
        grid=(b, L // tm),
        in_specs=[tok, tok,
                  tok,
                  pl.BlockSpec((1, tm, d), lambda bi, i: (bi, i, 1)),
                  tok,
                  pl.BlockSpec((1, 1, d), lambda bi, i: (bi, 0, 0)),
                  pl.BlockSpec((d, d), lambda bi, i: (0, 0))],
        out_specs=tok,
        out_shape=jax.ShapeDtypeStruct((b, L, d), F32),
        compiler_params=_cparams(("arbitrary", "arbitrary")),
        name="out_proj",
    )(y_rw, y_hy, gates, gates, x, ga, w_out_bf16)


PEER_TM = 512
PEER_TE = 512
PEER_SUB = 64
PEER_PAIR = 2
STAT_ROWS = 8


def _top_rows(s, n):
    rows = lax.broadcasted_iota(jnp.int32, (n, s.shape[1]), 0)
    out = jnp.zeros((n, s.shape[1]), F32)
    cur = s
    for i in range(n):
        m = jnp.max(cur, axis=0, keepdims=True)
        out = jnp.where(rows == i, m, out)
        cur = jnp.where(cur == m, -jnp.inf, cur)
    return out


def _route_kernel(x_ref, g_ref, sc_ref, sh_ref, wqt_ref, keys_ref, h_ref, s1_ref, c1_ref, s2_ref, st_ref):
    hb = _modulate(x_ref[0], g_ref[...], sc_ref[0], sh_ref[0]).astype(BF16)
    h_ref[0] = hb
    nt = (((1,), (1,)), ((), ()))
    qt = lax.dot_general(wqt_ref[...], hb, nt, preferred_element_type=F32)
    half = D_KEY // 2
    for hd in range(P_HEADS):
        sc = []
        for p in range(2):
            r0 = (hd * 2 + p) * half
            sc.append(jnp.dot(keys_ref[hd, p], qt[r0:r0 + half, :].astype(BF16),
                              preferred_element_type=F32))
        s1_ref[0, hd] = sc[0]
        s2_ref[0, hd] = sc[1]
        t1 = _top_rows(sc[0], P_TOPK)
        t2 = _top_rows(sc[1], P_TOPK)
        r8 = lax.broadcasted_iota(jnp.int32, (8, t1.shape[1]), 0)
        tiles = [t1[0:1] + t2[0:8], t1[0:1] + t2[8:16], t1[1:2] + t2[0:8]]
        for i in range(2, 8):
            tiles.append(jnp.where(r8 < P_TOPK // (i + 1), t1[i:i + 1] + t2[0:8], -jnp.inf))
        tiles.append(t1[8:16] + t2[0:1])
        best = _top_rows(jnp.concatenate(tiles, axis=0), P_TOPK)
        z = jnp.sum(jnp.exp(best - best[0:1]), axis=0, keepdims=True)
        c1_ref[0, hd] = jnp.exp(sc[0] - t1[0:1]) * (1.0 / z)
        st = jnp.where(r8 == 0, best[P_TOPK - 1:P_TOPK], 0.0)
        st_ref[0, hd] = jnp.where(r8 == 1, t2[0:1], st)


def _peer_route(x, g, scale, shift, wqt_bf16, keys_bf16):
    b, L, d = x.shape
    tm = min(L, PEER_TM)
    tok = lambda: pl.BlockSpec((1, tm, d), lambda bi, i: (bi, i, 0))
    rowv = pl.BlockSpec((1, 1, d), lambda bi, i: (bi, 0, 0))
    sspec = pl.BlockSpec((1, P_HEADS, N_KEYS, tm), lambda bi, i: (bi, 0, 0, i))
    return pl.pallas_call(
        _route_kernel,
        grid=(b, L // tm),
        in_specs=[tok(), pl.BlockSpec((1, d), lambda bi, i: (0, 0)), rowv, rowv,
                  pl.BlockSpec(wqt_bf16.shape, lambda bi, i: (0, 0)),
                  pl.BlockSpec(keys_bf16.shape, lambda bi, i: (0, 0, 0, 0))],
        out_specs=[tok(), sspec, sspec, sspec,
                   pl.BlockSpec((1, P_HEADS, STAT_ROWS, tm), lambda bi, i: (bi, 0, 0, i))],
        out_shape=[jax.ShapeDtypeStruct((b, L, d), BF16),
                   jax.ShapeDtypeStruct((b, P_HEADS, N_KEYS, L), F32),
                   jax.ShapeDtypeStruct((b, P_HEADS, N_KEYS, L), F32),
                   jax.ShapeDtypeStruct((b, P_HEADS, N_KEYS, L), F32),
                   jax.ShapeDtypeStruct((b, P_HEADS, STAT_ROWS, L), F32)],
        compiler_params=_cparams(("arbitrary", "arbitrary")),
        name="peer_route",
    )(x, g, scale, shift, wqt_bf16, keys_bf16)


def _peer_dense_kernel(h_ref, u_ref, vt_ref, s1_ref, c1_ref, s2_ref, st_ref, x_ref, ga_ref, o_ref,
                       acc_s, e2_s, ge_s, wa_s, wb_s, row_s):
    e = pl.program_id(2)
    last = pl.num_programs(2) - 1
    tm = h_ref.shape[1]
    n1 = PEER_TE // N_KEYS

    @pl.when(e == 0)
    def _():
        acc_s[...] = jnp.zeros_like(acc_s)
        wb_s[...] = jnp.zeros_like(wb_s)
        for hd in range(P_HEADS):
            e2_s[hd] = jnp.exp(s2_ref[0, hd] - st_ref[0, hd, 1:2, :])

    def step(w_prev, w_cur):
        acc_s[...] += jnp.dot(vt_ref[...], w_prev[...], preferred_element_type=F32)
        nt = (((1,), (1,)), ((), ()))
        act = lax.dot_general(u_ref[...], h_ref[0], nt, preferred_element_type=F32)
        ge_s[...] = 0.5 * act * (1.0 + lax.erf(act * (1.0 / math.sqrt(2.0))))
        tile = jnp.minimum(e, last - 1)
        for i in range(n1):
            for hd in range(P_HEADS):
                row_s[0, i, hd] = s1_ref[0, hd, pl.ds(tile * n1 + i, 1), :]
                row_s[1, i, hd] = c1_ref[0, hd, pl.ds(tile * n1 + i, 1), :]
        for ip in range(0, n1, PEER_PAIR):
            pair = range(ip, ip + PEER_PAIR)
            for strip in range(tm // LANES):
                cs = slice(strip * LANES, (strip + 1) * LANES)
                for sub in range(N_KEYS // PEER_SUB):
                    ks = slice(sub * PEER_SUB, (sub + 1) * PEER_SUB)
                    gsum = [None] * PEER_PAIR
                    for hd in range(P_HEADS):
                        s2t = s2_ref[0, hd, ks, cs]
                        e2t = e2_s[hd, ks, cs]
                        thr = st_ref[0, hd, 0:1, cs]
                        for q, i in enumerate(pair):
                            term = jnp.where((s2t + row_s[0, i, hd, :, cs]) >= thr,
                                             e2t * row_s[1, i, hd, :, cs], 0.0)
                            gsum[q] = term if gsum[q] is None else gsum[q] + term
                    for q, i in enumerate(pair):
                        ws = slice(i * N_KEYS + sub * PEER_SUB, i * N_KEYS + (sub + 1) * PEER_SUB)
                        w_cur[ws, cs] = (gsum[q] * ge_s[ws, cs]).astype(BF16)

    @pl.when(e % 2 == 0)
    def _():
        step(wb_s, wa_s)

    @pl.when(e % 2 == 1)
    def _():
        step(wa_s, wb_s)

    @pl.when(e == last)
    def _():
        o_ref[0] = x_ref[0] + ga_ref[0] * acc_s[...].T


def _peer_dense(h2, u_bf16, vt_bf16, s1, c1, s2, st, x, ga):
    b, L, d = x.shape
    tm = min(L, PEER_TM)
    tok = lambda: pl.BlockSpec((1, tm, d), lambda bi, i, e: (bi, i, 0))
    sspec = lambda: pl.BlockSpec((1, P_HEADS, N_KEYS, tm), lambda bi, i, e: (bi, 0, 0, i))
    ne = N_EXPERTS // PEER_TE
    return pl.pallas_call(
        _peer_dense_kernel,
        grid=(b, L // tm, ne + 1),
        in_specs=[tok(),
                  pl.BlockSpec((PEER_TE, d), lambda bi, i, e: (jnp.minimum(e, ne - 1), 0)),
                  pl.BlockSpec((d, PEER_TE), lambda bi, i, e: (0, jnp.maximum(e - 1, 0))),
                  sspec(), sspec(), sspec(),
                  pl.BlockSpec((1, P_HEADS, STAT_ROWS, tm), lambda bi, i, e: (bi, 0, 0, i)),
                  tok(),
                  pl.BlockSpec((1, 1, d), lambda bi, i, e: (bi, 0, 0))],
        out_specs=tok(),
        out_shape=jax.ShapeDtypeStruct((b, L, d), F32),
        scratch_shapes=[pltpu.VMEM((d, tm), F32), pltpu.VMEM((P_HEADS, N_KEYS, tm), F32),
                        pltpu.VMEM((PEER_TE, tm), F32), pltpu.VMEM((PEER_TE, tm), BF16),
                        pltpu.VMEM((PEER_TE, tm), BF16), pltpu.VMEM((2, PEER_TE // N_KEYS, P_HEADS, 1, tm), F32)],
        compiler_params=pltpu.CompilerParams(
            dimension_semantics=("arbitrary", "arbitrary", "arbitrary"),
            vmem_limit_bytes=56 * 1024 * 1024),
        name="peer_dense",
    )(h2, u_bf16, vt_bf16, s1, c1, s2, st, x, ga)


def _peer_block(x, g, scale, shift, ga, pw):
    h2, s1, c1, s2, st = _peer_route(x, g, scale, shift, pw["wqt"], pw["keys"])
    return _peer_dense(h2, pw["u"], pw["vt"], s1, c1, s2, st, x, ga)


def _final_norm_kernel(x_ref, g_ref, o_ref):
    x = x_ref[0]
    o_ref[0] = x * lax.rsqrt(jnp.mean(x * x, axis=-1, keepdims=True) + NORM_EPS) * g_ref[...]


def _final_norm(x, g):
    b, L, d = x.shape
    tm = 512
    tok = pl.BlockSpec((1, tm, d), lambda bi, i: (bi, i, 0))
    return pl.pallas_call(
        _final_norm_kernel,
        grid=(b, L // tm),
        in_specs=[tok, pl.BlockSpec((1, d), lambda bi, i: (0, 0))],
        out_specs=tok,
        out_shape=jax.ShapeDtypeStruct((b, L, d), F32),
        compiler_params=_cparams(("arbitrary", "arbitrary")),
        name="final_norm",
    )(x, g.reshape(1, d))


HY_N2 = 256
HY_CT = 1024
HY_S = 8
HY_OT = 512


def _hyena_filter(L, hy_p):
    w1, b1, w2, b2, w3, freq = hy_p
    t = jnp.linspace(0.0, 1.0, L, dtype=F32)[:, None]
    bands = jnp.linspace(1e-4, HY_BANDS - 1, HY_BANDS, dtype=F32)[None, :]
    ang = (2.0 * math.pi / L) * jnp.arange(L, dtype=F32)[:, None] * bands
    z = jnp.concatenate([t, jnp.cos(ang), -jnp.sin(ang)], axis=-1)
    hdn = jnp.sin(freq * (z @ w1 + b1))
    hdn = jnp.sin(freq * (hdn @ w2 + b2))
    pos = jnp.arange(L)
    posb = (L - pos) % L
    hdn2 = jnp.stack([hdn, hdn[posb]])
    t2 = jnp.stack([t, t[posb]])
    deltas = jnp.abs(jnp.linspace(math.log(HY_TARGET) / HY_SLOW_PCT, math.log(HY_TARGET) / HY_FAST_PCT,
                                  HY_D, dtype=F32))
    w3r = w3.reshape(HY_HIDDEN, HY_ORDER, 2, HY_D)
    first = (jnp.arange(L) == 0)[None, :, None]
    side = jnp.arange(2)[:, None, None]
    out = []
    for o in range(HY_ORDER):
        h = jnp.einsum("slh,hsc->slc", hdn2, w3r[:, o]) * jnp.exp(-t2 * deltas)
        h = h * lax.rsqrt(jnp.sum(h * h, axis=(0, 1), keepdims=True))
        g = h + jnp.where(first & (side == 0), h[1:2, 0:1, :], 0.0)
        g = jnp.where(first & (side == 1), 0.0, g)
        out.append(g.reshape(2 * L, HY_D))
    return out


def _cis(idx, n):
    ang = (2.0 * math.pi / n) * (idx % n).astype(F32)
    return jnp.cos(ang), -jnp.sin(ang)


def _block_complex(re, im):
    return jnp.concatenate([jnp.concatenate([re, -im], axis=-1),
                            jnp.concatenate([im, re], axis=-1)], axis=-2)


def _dft_mats(n1, n2, kin):
    k = jnp.arange(n1, dtype=jnp.int32)
    c1, s1 = _cis(k[:, None] * k[None, :], n1)
    half = max(n1 // 2, 1)
    eye = jnp.eye(HY_S, dtype=F32)
    outer = jnp.kron(_block_complex(c1[:, :half], s1[:, :half]), eye).astype(BF16)
    outer_real = jnp.kron(jnp.concatenate([c1, s1], axis=0), eye).astype(BF16)
    m2 = jnp.arange(n2, dtype=jnp.int32)
    freq = k[:, None, None] + n1 * m2[None, :, None]
    gr, gi = _cis(m2[None, None, :] * freq, n1 * n2)
    inner = _block_complex(gr[:, :, :kin], gi[:, :, :kin]).astype(BF16)
    inner_real = jnp.concatenate([gr, gi], axis=1).astype(BF16)
    return {"n1": n1, "outer": outer, "outer_real": outer_real, "inner": inner, "inner_real": inner_real}


def _hy_outer_fwd_kernel(x_ref, m_ref, o_ref):
    ct = x_ref.shape[-1]
    x = x_ref[0].reshape(-1, ct).astype(BF16)
    f = jnp.dot(m_ref[...], x, preferred_element_type=F32)
    o_ref[...] = f.reshape(o_ref.shape)


def _hy_outer_fwd(x, which, mat8):
    _, groups, r, n2, c = x.shape
    n1 = mat8.shape[0] // (2 * HY_S)
    ct = min(c, HY_OT)
    return pl.pallas_call(
        _hy_outer_fwd_kernel,
        grid=(n2 // HY_S, c // ct),
        in_specs=[pl.BlockSpec((1, groups, r, HY_S, ct), lambda q, j: (which, 0, 0, q, j)),
                  pl.BlockSpec(mat8.shape, lambda q, j: (0, 0))],
        out_specs=pl.BlockSpec((2, n1, HY_S, ct), lambda q, j: (0, 0, q, j)),
        out_shape=jax.ShapeDtypeStruct((2, n1, n2, c), F32),
        compiler_params=_cparams(("arbitrary", "arbitrary")),
        name="hy_outer_fwd",
    )(x, mat8)


def _hy_inner_kernel(*refs, conv, scale):
    if conv:
        a_ref, g_ref, s_ref, o_ref = refs
    else:
        a_ref, g_ref, o_ref = refs
    parts = [a_ref[p, 0] for p in range(a_ref.shape[0])]
    a = (jnp.concatenate(parts, axis=0) if len(parts) > 1 else parts[0]).astype(BF16)
    g = g_ref[0]
    f = jnp.dot(g, a, preferred_element_type=F32)
    ko = f.shape[0] // 2
    fr, fi = f[:ko], f[ko:]
    if not conv:
        o_ref[0, 0] = fr * scale
        o_ref[1, 0] = fi * scale
    else:
        sr, si = s_ref[0, 0], s_ref[1, 0]
        p = jnp.concatenate([fr * sr - fi * si, fr * si + fi * sr], axis=0).astype(BF16)
        d = lax.dot_general(g, p, (((0,), (0,)), ((), ())), preferred_element_type=F32)
        kin = d.shape[0] // 2
        o_ref[0, 0] = d[:kin].astype(o_ref.dtype)
        o_ref[1, 0] = d[kin:].astype(o_ref.dtype)


def _hy_inner(a, gmat, spec, out_dtype, scale=1.0):
    p, n1, kin, c = a.shape
    ko = gmat.shape[1] // 2
    conv = spec is not None
    ct = min(c, HY_CT)
    in_specs = [pl.BlockSpec((p, 1, kin, ct), lambda k, j: (0, k, 0, j)),
                pl.BlockSpec((1,) + gmat.shape[1:], lambda k, j: (k, 0, 0))]
    args = [a, gmat]
    if conv:
        in_specs.append(pl.BlockSpec((2, 1, ko, ct), lambda k, j: (0, k, 0, j)))
        args.append(spec)
    rows = kin if conv else ko
    return pl.pallas_call(
        functools.partial(_hy_inner_kernel, conv=conv, scale=scale),
        grid=(n1, c // ct),
        in_specs=in_specs,
        out_specs=pl.BlockSpec((2, 1, rows, ct), lambda k, j: (0, k, 0, j)),
        out_shape=jax.ShapeDtypeStruct((2, n1, rows, c), out_dtype),
        compiler_params=_cparams(("arbitrary", "arbitrary")),
        name="hy_inner_conv" if conv else "hy_inner_spec",
    )(*args)


def _hy_outer_inv_kernel(d_ref, m_ref, v_ref, x_ref, sk_ref, o_ref):
    ct = d_ref.shape[-1]
    d = d_ref[...].reshape(-1, ct).astype(BF16)
    y = lax.dot_general(m_ref[...], d, (((0,), (0,)), ((), ())), preferred_element_type=F32)
    y = y.reshape(o_ref.shape[1:])
    o_ref[0] = x_ref[0] * (y + v_ref[0] * sk_ref[...])


def _hy_outer_inv(d, mat8, src, si, gate, gi, skip_row):
    _, n1, n2, c = d.shape
    half = n1 // 2
    ct = min(c, HY_OT)
    blk = lambda which: pl.BlockSpec((1, 2, half, HY_S, ct), lambda q, j: (which, 0, 0, q, j))
    return pl.pallas_call(
        _hy_outer_inv_kernel,
        grid=(n2 // HY_S, c // ct),
        in_specs=[pl.BlockSpec((2, n1, HY_S, ct), lambda q, j: (0, 0, q, j)),
                  pl.BlockSpec(mat8.shape, lambda q, j: (0, 0)),
                  blk(si), blk(gi),
                  pl.BlockSpec((1, ct), lambda q, j: (0, j))],
        out_specs=blk(0),
        out_shape=jax.ShapeDtypeStruct((1, 2, half, n2, c), F32),
        compiler_params=_cparams(("arbitrary", "arbitrary")),
        name="hy_outer_inv",
    )(d, mat8, src, gate, skip_row)


def _hyena_spectrum(g, mats):
    n, c = g[0].shape
    n1 = mats["n1"]
    n2 = n // n1
    out = []
    for go in g:
        if n1 > 1:
            a = _hy_outer_fwd(go.reshape(1, 1, n1, n2, c), 0, mats["outer_real"])
            out.append(_hy_inner(a, mats["inner"], None, F32, 1.0 / n))
        else:
            out.append(_hy_inner(go.reshape(1, 1, n, c), mats["inner_real"], None, F32, 1.0 / n))
    return out


def _hy_gate_kernel(y_ref, v_ref, x_ref, sk_ref, o_ref):
    o_ref[...] = x_ref[...] * (y_ref[...] + v_ref[...] * sk_ref[...])


def _hy_gate(y, v, x, skip_row):
    b, L, c = y.shape
    tok = pl.BlockSpec((1, L, c), lambda bi: (bi, 0, 0))
    return pl.pallas_call(
        _hy_gate_kernel,
        grid=(b,),
        in_specs=[tok, tok, tok, pl.BlockSpec((1, c), lambda bi: (0, 0))],
        out_specs=tok,
        out_shape=jax.ShapeDtypeStruct((b, L, c), F32),
        compiler_params=_cparams(("arbitrary",)),
        name="hy_gate",
    )(y, v, x, skip_row)


def _hyena_mixer(hy, spec, skip, mats):
    _, b, L, c = hy.shape
    assert b == 2, "the two batch entries are packed as one complex signal"
    n1 = mats["n1"]
    if n1 == 1:
        z = hy[0]
        for o in range(HY_ORDER):
            y = _hy_inner(z.reshape(2, 1, L, c), mats["inner"], spec[o], F32).reshape(b, L, c)
            z = _hy_gate(y, z, hy[o + 1], skip[o].reshape(1, c))
        return z
    n2 = 2 * L // n1
    src, si = hy.reshape(3, b, n1 // 2, n2, c), 0
    gate = src
    for o in range(HY_ORDER):
        a = _hy_outer_fwd(src, si, mats["outer"])
        d = _hy_inner(a, mats["inner"], spec[o], F32)
        src, si = _hy_outer_inv(d, mats["outer"], src, si, gate, o + 1, skip[o].reshape(1, c)), 0
    return src.reshape(b, L, c)


def _token_mixing(x, g, scale, shift, ga, lw, n_seg, s0_f, s0_b, hy_p):
    b, L, _ = x.shape
    seg = L // n_seg
    proj = functools.partial(_in_proj, x, g, scale, shift, lw["w_in"], lw["conv_w"], seg)
    y_rw, s_f, s_b = _rwkv_mixer(proj("rw"), lw["rw"], s0_f, s0_b)
    if hy_p is None:
        return None, s_f, s_b
    n = 2 * L
    n1 = n // HY_N2 if n // HY_N2 >= 16 else 1
    mats = _dft_mats(n1, n // n1, L if n1 == 1 else n // n1)
    spec = _hyena_spectrum(_hyena_filter(L, hy_p), mats)
    y_hy = _hyena_mixer(proj("hy"), spec, lw["hy_skip"], mats)
    return _out_proj(y_rw, y_hy, proj("gate"), x, ga, lw["w_out"]), s_f, s_b


def kernel(x, c, ctx, c_ctx, ada_w, ada_b, norm1_g, norm2_g, w_in, conv_w, rw_w0, rw_w2, rw_a0, rw_a2, rw_g2, rw_k_k, rw_k_a, rw_r_k, rw_ln_w, rw_ln_b, hy_w1, hy_b1, hy_w2, hy_b2, hy_w3, hy_freq, hy_skip, w_out, peer_wq, peer_keys, peer_u, peer_v, final_g):
    b, L, d = x.shape
    depth = ada_w.shape[0]
    ctx_len = ctx.shape[1]
    s_zero = jnp.zeros((b, N_GROUPS, GROUP_W, GROUP_W), F32)
    cond = jnp.concatenate([c, c_ctx[None, :]], axis=0)
    cond = jnp.pad(jax.nn.silu(cond), ((0, 8 - (b + 1)), (0, 0)))
    for l in range(depth):
        last = l == depth - 1
        mod = _matmul(cond, ada_w[l], 1024) + ada_b[l]
        mx = [m[:, None, :] for m in jnp.split(mod[:b], 6, axis=-1)]
        mc = [jnp.broadcast_to(m[None, :, :], (b, 1, d)) for m in jnp.split(mod[b:b + 1], 6, axis=-1)]
        lw = {
            "w_in": _pack_cols(w_in[l]).astype(BF16),
            "conv_w": _pack_cols(conv_w[l]),
            "rw": _rwkv_params(l, rw_w0, rw_w2, rw_a0, rw_a2, rw_g2, rw_k_k, rw_k_a, rw_r_k, rw_ln_w, rw_ln_b),
            "hy_skip": hy_skip[l],
            "w_out": w_out[l].astype(BF16),
        }
        hy_p = (hy_w1[l], hy_b1[l], hy_w2[l], hy_b2[l], hy_w3[l], hy_freq[l])
        pw = {
            "wqt": peer_wq[l].T.astype(BF16),
            "keys": peer_keys[l].astype(BF16),
            "u": peer_u[l].astype(BF16),
            "vt": peer_v[l].T.astype(BF16),
        }
        g1 = norm1_g[l].reshape(1, d)
        g2 = norm2_g[l].reshape(1, d)

        ctx_new, s_f, s_b = _token_mixing(ctx, g1, mc[1], mc[0], mc[2], lw, 1, s_zero, s_zero,
                                          None if last else hy_p)
        if not last:
            ctx = _peer_block(ctx_new, g2, mc[4], mc[3], mc[5], pw)

        x, _, _ = _token_mixing(x, g1, mx[1], mx[0], mx[2], lw, L // GRID_W, s_f, s_b, hy_p)
        x = _peer_block(x, g2, mx[4], mx[3], mx[5], pw)
    return _final_norm(x, final_g)
```

```python
import functools
import math

import numpy as np
import jax
import jax.numpy as jnp
from jax import lax
from jax.experimental import pallas as pl
from jax.experimental.pallas import tpu as pltpu

F32 = jnp.float32
BF16 = jnp.bfloat16

D_MODEL = 2048
GRID_W = 64
NORM_EPS = 1e-6
RW_N = 64
RW_H = D_MODEL // RW_N
RW_D = D_MODEL
W_LORA = 96
A_LORA = 96
G_LORA = 64
RW_GN_EPS = 64e-5
HY_D = D_MODEL
HY_ORDER = 2
HY_EMB = 33
HY_BANDS = (HY_EMB - 1) // 2
HY_HIDDEN = 64
HY_TARGET = 1e-2
HY_FAST_PCT = 0.3
HY_SLOW_PCT = 1.5
P_HEADS = 8
N_KEYS = 128
N_EXPERTS = N_KEYS * N_KEYS
P_TOPK = 16
D_KEY = 256

LANES = 128
MXU_DIM = 256
VMEM_LIMIT = 48 * 1024 * 1024

COL_K, COL_V, COL_R = 0, RW_D, 2 * RW_D
COL_SMALL = 3 * RW_D
SMALL_W = 1024
COL_GATE = COL_SMALL + SMALL_W
COL_HY = COL_GATE + 2 * D_MODEL
N_PACKED = COL_HY + 3 * HY_D
PROJ_TN = 512
PROJ_RC = 256

CHUNK = 64
HEADS_PER_GROUP = MXU_DIM // RW_N
GROUP_W = HEADS_PER_GROUP * RW_N
N_GROUPS = RW_D // GROUP_W
STACK = HEADS_PER_GROUP * CHUNK
GROUPS_PER_STEP = 8
STEP_W = GROUPS_PER_STEP * GROUP_W
RWKV_TB = 128


def _cparams(sem):
    return pltpu.CompilerParams(dimension_semantics=sem, vmem_limit_bytes=VMEM_LIMIT)


def _rwkv_consts(reverse):
    t = np.arange(CHUNK)
    tt, ss = t[:, None], t[None, :]
    strict = (tt < ss) if reverse else (tt > ss)
    eye = tt == ss
    levels = []
    m = 1
    while m < CHUNK:
        levels.append(strict & ((tt // (2 * m)) == (ss // (2 * m))) & ((tt // m) != (ss // m)))
        m *= 2
    lanes = lambda a: np.tile(a, (1,) * (a.ndim - 1) + (HEADS_PER_GROUP,))
    hb = np.arange(STACK) // CHUNK
    same = hb[:, None] == hb[None, :]
    tri = (tt <= ss) if reverse else (tt >= ss)
    f = lambda a: jnp.asarray(a.astype(np.float32))
    return (f(lanes(strict)), f(lanes(strict | eye)), f(same), f(lanes(eye)),
            f(lanes(np.stack(levels))), f(tri))


def _rwkv_kernel(k_ref, v_ref, r_ref, wd_ref, ad_ref, gd_ref, yf_ref,
                 w0_ref, w2_ref, a0_ref, a2_ref, g2_ref, kk_ref, ka_ref, rk_ref, lnw_ref, lnb_ref,
                 ms_ref, mi_ref, same_ref, eye_ref, lev_ref, tri_ref, s0_ref,
                 y_ref, sout_ref,
                 S_ref, lw_s, kt_s, kn_s, b_s, yacc_s, *, reverse, final, nchunk):
    step = pl.program_id(2)

    @pl.when(step == 0)
    def _():
        S_ref[...] = s0_ref[0]

    same = same_ref[...]
    gsl = [slice(g * GROUP_W, (g + 1) * GROUP_W) for g in range(GROUPS_PER_STEP)]

    same_b = same.astype(BF16)

    def split(x, pieces):
        out = []
        for _ in range(pieces):
            p = x.astype(BF16)
            out.append(p)
            x = x - p.astype(F32)
        return out

    def head_sum(x):
        cols = []
        for s in gsl:
            hi, lo = split(x[:, s], 2)
            cols.append(jnp.dot(hi, same_b, preferred_element_type=F32)
                        + jnp.dot(lo, same_b, preferred_element_type=F32))
        return jnp.concatenate(cols, axis=1)

    k = k_ref[0]
    wraw = w0_ref[...] + jnp.dot(jnp.tanh(wd_ref[0]).astype(BF16), w2_ref[...],
                                 preferred_element_type=F32)
    lw_s[...] = -jax.nn.sigmoid(wraw) * math.exp(-0.5)
    a = jax.nn.sigmoid(a0_ref[...] + jnp.dot(ad_ref[0].astype(BF16), a2_ref[...],
                                             preferred_element_type=F32))
    kt_s[...] = k * (1.0 + (a - 1.0) * ka_ref[...])
    kn = k * kk_ref[...]
    kn = kn * lax.rsqrt(jnp.maximum(head_sum(kn * kn), 1e-24))
    kn_s[...] = kn
    b_s[...] = kn * a

    ms = ms_ref[...]
    mi = mi_ref[...]
    tri = tri_ref[...].astype(BF16)
    mid = CHUNK // 2 if reverse else CHUNK // 2 - 1
    last = 0 if reverse else CHUNK - 1

    def bd(x):
        return jnp.concatenate([x.astype(BF16)] * HEADS_PER_GROUP, axis=0) * same_b

    def mm(a, w):
        return jnp.dot(a.astype(BF16), w, preferred_element_type=F32)

    nt = (((1,), (1,)), ((), ()))
    tn = (((0,), (0,)), ((), ()))

    def chunk_group(rows, g):
        cols = gsl[g]
        lw = lw_s[rows, cols]
        kt = kt_s[rows, cols]
        kn = kn_s[rows, cols]
        b = b_s[rows, cols]
        r = r_ref[0, rows, cols]
        v = v_ref[0, rows, cols]
        c = sum(jnp.dot(tri, p, preferred_element_type=F32) for p in split(lw, 3))
        yield
        cp = c - lw
        rho = c[mid:mid + 1, :]
        cend = c[last:last + 1, :]
        einv = jnp.exp(rho - c)
        eend = jnp.exp(cend - c)
        lhs = jnp.concatenate([kn * jnp.exp(cp - rho), r * jnp.exp(c - rho)], axis=0).astype(BF16)
        rhs = jnp.concatenate([bd(kt * einv), bd(b * einv)], axis=0)
        ag = lax.dot_general(lhs, rhs, nt, preferred_element_type=F32)
        yield
        ak = ag[:CHUNK, :STACK] * ms
        ab = ag[:CHUNK, STACK:] * ms
        gk = ag[CHUNK:, :STACK] * mi
        gb = ag[CHUNK:, STACK:] * mi
        tm = eye_ref[...] - ab * lev_ref[0]
        for lv in range(1, lev_ref.shape[0]):
            te = mm(tm, bd(ab * lev_ref[lv]))
            yield
            tm = tm - mm(te, bd(tm))
            yield
        S = S_ref[g]
        lhs_s = jnp.concatenate([kn * jnp.exp(cp), r * jnp.exp(c)], axis=0).astype(BF16)
        qs = lax.dot_general(lhs_s, S.astype(BF16), nt, preferred_element_type=F32)
        vs = bd(v)
        rhs_u = qs[:CHUNK] + mm(ak, vs)
        yield
        us = mm(tm, bd(rhs_u))
        yield
        y = qs[CHUNK:] + mm(jnp.concatenate([gk, gb], axis=1), jnp.concatenate([vs, bd(-us)], axis=0))
        yield
        if final:
            yacc_s[rows, cols] = y
        else:
            y_ref[0, rows, cols] = y
        vu = jnp.concatenate([v, -us], axis=0).astype(BF16)
        kb = jnp.concatenate([kt * eend, b * eend], axis=0).astype(BF16)
        ds = lax.dot_general(vu, kb, tn, preferred_element_type=F32)
        S_ref[g] = S * jnp.exp(cend) + ds * same

    def chunk(ci, carry):
        c_idx = (nchunk - 1 - ci) if reverse else ci
        rows = pl.ds(pl.multiple_of(c_idx * CHUNK, CHUNK), CHUNK)
        pending = [chunk_group(rows, g) for g in range(GROUPS_PER_STEP)]
        while pending:
            pending = [gen for gen in pending if next(gen, True) is None]
        return carry

    lax.fori_loop(0, nchunk, chunk, 0)
    sout_ref[0] = S_ref[...]

    if final:
        y = yf_ref[0] + yacc_s[...]
        inv_n = 1.0 / RW_N
        mu = head_sum(y) * inv_n
        d = y - mu
        var = head_sum(d * d) * inv_n
        yn = d * lax.rsqrt(var + RW_GN_EPS) * lnw_ref[...] + lnb_ref[...]
        bonus = head_sum(r_ref[0] * k * rk_ref[...]) * v_ref[0]
        g = jnp.dot(jax.nn.sigmoid(gd_ref[0]).astype(BF16), g2_ref[...], preferred_element_type=F32)
        y_ref[0] = (yn + bonus) * g


def _pad_rows(w, rows):
    return jnp.pad(w, ((0, rows - w.shape[0]), (0, 0)))


def _rwkv_direction(mix, yf, prm, s0, d, *, final):
    b, L, _ = mix.shape
    reverse = d == 1
    tb = min(L, RWKV_TB)
    nb = L // tb
    nchunk = tb // CHUNK
    ms, mi, same, eye, lev, tri = _rwkv_consts(reverse)
    blk = (lambda i: nb - 1 - i) if reverse else (lambda i: i)
    sm = COL_SMALL // LANES

    def col(c0):
        return pl.BlockSpec((1, tb, STEP_W), lambda bi, g, i: (bi, blk(i), c0 // STEP_W + g))

    def small(j):
        return pl.BlockSpec((1, tb, LANES), lambda bi, g, i: (bi, blk(i), sm + j))

    vec = pl.BlockSpec((1, STEP_W), lambda bi, g, i: (0, g))
    lora = pl.BlockSpec((LANES, STEP_W), lambda bi, g, i: (0, g))
    const2 = lambda shp: pl.BlockSpec(shp, lambda bi, g, i: (0,) * len(shp))
    state = pl.BlockSpec((1, GROUPS_PER_STEP, GROUP_W, GROUP_W), lambda bi, g, i: (bi, g, 0, 0))
    yspec = pl.BlockSpec((1, tb, STEP_W), lambda bi, g, i: (bi, blk(i), g))

    kern = functools.partial(_rwkv_kernel, reverse=reverse, final=final, nchunk=nchunk)
    if yf is None:
        yf, yf_spec = mix, col(COL_K)
    else:
        yf_spec = yspec
    y, s_fin = pl.pallas_call(
        kern,
        grid=(b, N_GROUPS // GROUPS_PER_STEP, nb),
        in_specs=[col(COL_K), col(COL_V), col(COL_R), small(d), small(2 + d), small(4), yf_spec,
                  vec, lora, vec, lora, lora, vec, vec, vec, vec, vec,
                  const2((CHUNK, STACK)), const2((CHUNK, STACK)), const2((STACK, STACK)),
                  const2((CHUNK, STACK)), const2(tuple(lev.shape)), const2((CHUNK, CHUNK)), state],
        out_specs=[yspec, state],
        out_shape=[jax.ShapeDtypeStruct((b, L, RW_D), F32),
                   jax.ShapeDtypeStruct((b, N_GROUPS, GROUP_W, GROUP_W), F32)],
        scratch_shapes=[pltpu.VMEM((GROUPS_PER_STEP, GROUP_W, GROUP_W), F32)]
        + [pltpu.VMEM((tb, STEP_W), F32)] * 5,
        compiler_params=_cparams(("arbitrary", "arbitrary", "arbitrary")),
        name="rwkv_bwd" if reverse else "rwkv_fwd",
    )(mix, mix, mix, mix, mix, mix, yf,
      prm["w0"][d], prm["w2"][d], prm["a0"][d], prm["a2"][d], prm["g2"], prm["k_k"], prm["k_a"],
      prm["r_k"], prm["ln_w"], prm["ln_b"],
      ms, mi, same, eye, lev, tri, s0)
    return y, s_fin


def _rwkv_params(l, rw_w0, rw_w2, rw_a0, rw_a2, rw_g2, rw_k_k, rw_k_a, rw_r_k, rw_ln_w, rw_ln_b):
    row = lambda v: v.reshape(1, RW_D)
    return {
        "w0": [row(rw_w0[l, d]) for d in range(2)],
        "w2": [_pad_rows(rw_w2[l, d], LANES).astype(BF16) for d in range(2)],
        "a0": [row(rw_a0[l, d]) for d in range(2)],
        "a2": [_pad_rows(rw_a2[l, d], LANES).astype(BF16) for d in range(2)],
        "g2": _pad_rows(rw_g2[l], LANES).astype(BF16),
        "k_k": row(rw_k_k[l]), "k_a": row(rw_k_a[l]), "r_k": row(rw_r_k[l]),
        "ln_w": row(rw_ln_w[l]), "ln_b": row(rw_ln_b[l]),
    }


def _rwkv_mixer(mix, prm, s0_f, s0_b):
    b, L, _ = mix.shape
    y_f, s_f = _rwkv_direction(mix, None, prm, s0_f, 0, final=False)
    y, s_b = _rwkv_direction(mix, y_f, prm, s0_b, 1, final=True)
    return y, s_f, s_b


def _matmul_kernel(a_ref, b_ref, o_ref):
    o_ref[...] = jnp.dot(a_ref[...].astype(BF16), b_ref[...].astype(BF16), preferred_element_type=F32)


def _matmul(a, b, tn):
    m, k = a.shape
    n = b.shape[1]
    return pl.pallas_call(
        _matmul_kernel,
        grid=(n // tn,),
        in_specs=[pl.BlockSpec((m, k), lambda j: (0, 0)), pl.BlockSpec((k, tn), lambda j: (0, j))],
        out_specs=pl.BlockSpec((m, tn), lambda j: (0, j)),
        out_shape=jax.ShapeDtypeStruct((m, n), F32),
        compiler_params=_cparams(("arbitrary",)),
        name="matmul",
    )(a, b)


def _modulate(x, g, scale, shift):
    y = x * lax.rsqrt(jnp.mean(x * x, axis=-1, keepdims=True) + NORM_EPS)
    return y * g * (1.0 + scale) + shift


def _pack_cols(m):
    z = lambda n: jnp.zeros(m.shape[:-1] + (n,), m.dtype)
    s = lambda a, b: m[..., a:b]
    o = 2 * RW_D
    lo = [s(o + j * W_LORA, o + (j + 1) * W_LORA) for j in range(4)]
    r0 = o + 2 * W_LORA + 2 * A_LORA
    g0 = r0 + RW_D
    h0 = g0 + G_LORA
    parts = [s(0, RW_D), s(RW_D, 2 * RW_D), s(r0, r0 + RW_D)]
    for p in lo:
        parts += [p, z(LANES - W_LORA)]
    parts += [s(g0, g0 + G_LORA), z(LANES - G_LORA), z(SMALL_W - 5 * LANES)]
    if m.shape[-1] > h0 + 3 * HY_D:
        parts.append(s(h0 + 3 * HY_D, h0 + 3 * HY_D + 2 * D_MODEL))
    else:
        parts.append(z(2 * D_MODEL))
    parts.append(s(h0, h0 + 3 * HY_D))
    return jnp.concatenate(parts, axis=-1)


def _inproj_kernel(x_ref, g_ref, sc_ref, sh_ref, w_ref, cw_ref, o_ref, h_s, *, seg, gate):
    j = pl.program_id(2)

    @pl.when(j == 0)
    def _():
        h_s[...] = _modulate(x_ref[0], g_ref[...], sc_ref[0], sh_ref[0]).astype(BF16)

    out_blk = (0,) * (len(o_ref.shape) - 2)
    tm = h_s.shape[0]
    rc = min(tm, max(seg, PROJ_RC))
    for r0 in range(0, tm, rc):
        p = jnp.dot(h_s[r0:r0 + rc, :], w_ref[...], preferred_element_type=F32)
        if gate:
            res = jax.nn.sigmoid(p)
        else:
            row = lax.broadcasted_iota(jnp.int32, p.shape, 0) % seg
            prev = jnp.where(row == 0, 0.0, pltpu.roll(p, 1, 0))
            nxt = jnp.where(row == seg - 1, 0.0, pltpu.roll(p, rc - 1, 0))
            res = cw_ref[0:1, :] * prev + cw_ref[1:2, :] * p + cw_ref[2:3, :] * nxt
        o_ref[out_blk + (slice(r0, r0 + rc), slice(None))] = res


def _in_proj(x, g, scale, shift, w_packed, cw_packed, seg, part):
    b, L, d = x.shape
    tm = min(L, 1024)
    c0, c1 = {"rw": (0, COL_GATE), "gate": (COL_GATE, COL_HY), "hy": (COL_HY, N_PACKED)}[part]
    t0, nt = c0 // PROJ_TN, (c1 - c0) // PROJ_TN
    if part == "hy":
        per = HY_D // PROJ_TN
        out_spec = pl.BlockSpec((1, 1, tm, PROJ_TN), lambda bi, i, j: (j // per, bi, i, j % per))
        out_shape = jax.ShapeDtypeStruct((3, b, L, HY_D), F32)
    else:
        out_spec = pl.BlockSpec((1, tm, PROJ_TN), lambda bi, i, j: (bi, i, j))
        out_shape = jax.ShapeDtypeStruct((b, L, c1 - c0), F32)
    return pl.pallas_call(
        functools.partial(_inproj_kernel, seg=seg, gate=part == "gate"),
        grid=(b, L // tm, nt),
        in_specs=[pl.BlockSpec((1, tm, d), lambda bi, i, j: (bi, i, 0)),
                  pl.BlockSpec((1, d), lambda bi, i, j: (0, 0)),
                  pl.BlockSpec((1, 1, d), lambda bi, i, j: (bi, 0, 0)),
                  pl.BlockSpec((1, 1, d), lambda bi, i, j: (bi, 0, 0)),
                  pl.BlockSpec((d, PROJ_TN), lambda bi, i, j: (0, t0 + j)),
                  pl.BlockSpec((3, PROJ_TN), lambda bi, i, j: (0, t0 + j))],
        out_specs=out_spec,
        out_shape=out_shape,
        scratch_shapes=[pltpu.VMEM((tm, d), BF16)],
        compiler_params=_cparams(("arbitrary", "arbitrary", "arbitrary")),
        name="in_proj_" + part,
    )(x, g, scale, shift, w_packed, cw_packed)


def _outproj_kernel(yrw_ref, yhy_ref, grw_ref, ghy_ref, x_ref, ga_ref, w_ref, o_ref):
    y = grw_ref[0] * yrw_ref[0] + ghy_ref[0] * yhy_ref[0]
    o_ref[0] = x_ref[0] + ga_ref[0] * jnp.dot(y.astype(BF16), w_ref[...], preferred_element_type=F32)


def _out_proj(y_rw, y_hy, gates, x, ga, w_out_bf16):
    b, L, d = x.shape
    tm = 256
    tok = pl.BlockSpec((1, tm, d), lambda bi, i: (bi, i, 0))
    return pl.pallas_call(
        _outproj_kernel,
        grid=(b, L // tm),
        in_specs=[tok, tok,
                  tok,
                  pl.BlockSpec((1, tm, d), lambda bi, i: (bi, i, 1)),
                  tok,
                  pl.BlockSpec((1, 1, d), lambda bi, i: (bi, 0, 0)),
                  pl.BlockSpec((d, d), lambda bi, i: (0, 0))],
        out_specs=tok,
        out_shape=jax.ShapeDtypeStruct((b, L, d), F32),
        compiler_params=_cparams(("arbitrary", "arbitrary")),
        name="out_proj",
    )(y_rw, y_hy, gates, gates, x, ga, w_out_bf16)


PEER_TM = 512
PEER_TE = 512
PEER_SUB = 64
PEER_PAIR = 2
STAT_ROWS = 8


def _top_rows(s, n):
    rows = lax.broadcasted_iota(jnp.int32, (n, s.shape[1]), 0)
    out = jnp.zeros((n, s.shape[1]), F32)
    cur = s
    for i in range(n):
        m = jnp.max(cur, axis=0, keepdims=True)
        out = jnp.where(rows == i, m, out)
        cur = jnp.where(cur == m, -jnp.inf, cur)
    return out


def _route_kernel(x_ref, g_ref, sc_ref, sh_ref, wqt_ref, keys_ref, h_ref, s1_ref, c1_ref, s2_ref, st_ref):
    hb = _modulate(x_ref[0], g_ref[...], sc_ref[0], sh_ref[0]).astype(BF16)
    h_ref[0] = hb
    nt = (((1,), (1,)), ((), ()))
    qt = lax.dot_general(wqt_ref[...], hb, nt, preferred_element_type=F32)
    half = D_KEY // 2
    for hd in range(P_HEADS):
        sc = []
        for p in range(2):
            r0 = (hd * 2 + p) * half
            sc.append(jnp.dot(keys_ref[hd, p], qt[r0:r0 + half, :].astype(BF16),
                              preferred_element_type=F32))
        s1_ref[0, hd] = sc[0]
        s2_ref[0, hd] = sc[1]
        t1 = _top_rows(sc[0], P_TOPK)
        t2 = _top_rows(sc[1], P_TOPK)
        r8 = lax.broadcasted_iota(jnp.int32, (8, t1.shape[1]), 0)
        tiles = [t1[0:1] + t2[0:8], t1[0:1] + t2[8:16], t1[1:2] + t2[0:8]]
        for i in range(2, 8):
            tiles.append(jnp.where(r8 < P_TOPK // (i + 1), t1[i:i + 1] + t2[0:8], -jnp.inf))
        tiles.append(t1[8:16] + t2[0:1])
        best = _top_rows(jnp.concatenate(tiles, axis=0), P_TOPK)
        z = jnp.sum(jnp.exp(best - best[0:1]), axis=0, keepdims=True)
        c1_ref[0, hd] = jnp.exp(sc[0] - t1[0:1]) * (1.0 / z)
        st = jnp.where(r8 == 0, best[P_TOPK - 1:P_TOPK], 0.0)
        st_ref[0, hd] = jnp.where(r8 == 1, t2[0:1], st)


def _peer_route(x, g, scale, shift, wqt_bf16, keys_bf16):
    b, L, d = x.shape
    tm = min(L, PEER_TM)
    tok = lambda: pl.BlockSpec((1, tm, d), lambda bi, i: (bi, i, 0))
    rowv = pl.BlockSpec((1, 1, d), lambda bi, i: (bi, 0, 0))
    sspec = pl.BlockSpec((1, P_HEADS, N_KEYS, tm), lambda bi, i: (bi, 0, 0, i))
    return pl.pallas_call(
        _route_kernel,
        grid=(b, L // tm),
        in_specs=[tok(), pl.BlockSpec((1, d), lambda bi, i: (0, 0)), rowv, rowv,
                  pl.BlockSpec(wqt_bf16.shape, lambda bi, i: (0, 0)),
                  pl.BlockSpec(keys_bf16.shape, lambda bi, i: (0, 0, 0, 0))],
        out_specs=[tok(), sspec, sspec, sspec,
                   pl.BlockSpec((1, P_HEADS, STAT_ROWS, tm), lambda bi, i: (bi, 0, 0, i))],
        out_shape=[jax.ShapeDtypeStruct((b, L, d), BF16),
                   jax.ShapeDtypeStruct((b, P_HEADS, N_KEYS, L), F32),
                   jax.ShapeDtypeStruct((b, P_HEADS, N_KEYS, L), F32),
                   jax.ShapeDtypeStruct((b, P_HEADS, N_KEYS, L), F32),
                   jax.ShapeDtypeStruct((b, P_HEADS, STAT_ROWS, L), F32)],
        compiler_params=_cparams(("arbitrary", "arbitrary")),
        name="peer_route",
    )(x, g, scale, shift, wqt_bf16, keys_bf16)


def _peer_dense_kernel(h_ref, u_ref, vt_ref, s1_ref, c1_ref, s2_ref, st_ref, x_ref, ga_ref, o_ref,
                       acc_s, e2_s, ge_s, wa_s, wb_s, row_s):
    e = pl.program_id(2)
    last = pl.num_programs(2) - 1
    tm = h_ref.shape[1]
    n1 = PEER_TE // N_KEYS

    @pl.when(e == 0)
    def _():
        acc_s[...] = jnp.zeros_like(acc_s)
        wb_s[...] = jnp.zeros_like(wb_s)
        for hd in range(P_HEADS):
            e2_s[hd] = jnp.exp(s2_ref[0, hd] - st_ref[0, hd, 1:2, :])

    def step(w_prev, w_cur):
        acc_s[...] += jnp.dot(vt_ref[...], w_prev[...], preferred_element_type=F32)
        nt = (((1,), (1,)), ((), ()))
        act = lax.dot_general(u_ref[...], h_ref[0], nt, preferred_element_type=F32)
        ge_s[...] = 0.5 * act * (1.0 + lax.erf(act * (1.0 / math.sqrt(2.0))))
        tile = jnp.minimum(e, last - 1)
        for i in range(n1):
            for hd in range(P_HEADS):
                row_s[0, i, hd] = s1_ref[0, hd, pl.ds(tile * n1 + i, 1), :]
                row_s[1, i, hd] = c1_ref[0, hd, pl.ds(tile * n1 + i, 1), :]
        for ip in range(0, n1, PEER_PAIR):
            pair = range(ip, ip + PEER_PAIR)
            for strip in range(tm // LANES):
                cs = slice(strip * LANES, (strip + 1) * LANES)
                for sub in range(N_KEYS // PEER_SUB):
                    ks = slice(sub * PEER_SUB, (sub + 1) * PEER_SUB)
                    gsum = [None] * PEER_PAIR
                    for hd in range(P_HEADS):
                        s2t = s2_ref[0, hd, ks, cs]
                        e2t = e2_s[hd, ks, cs]
                        thr = st_ref[0, hd, 0:1, cs]
                        for q, i in enumerate(pair):
                            term = jnp.where((s2t + row_s[0, i, hd, :, cs]) >= thr,
                                             e2t * row_s[1, i, hd, :, cs], 0.0)
                            gsum[q] = term if gsum[q] is None else gsum[q] + term
                    for q, i in enumerate(pair):
                        ws = slice(i * N_KEYS + sub * PEER_SUB, i * N_KEYS + (sub + 1) * PEER_SUB)
                        w_cur[ws, cs] = (gsum[q] * ge_s[ws, cs]).astype(BF16)

    @pl.when(e % 2 == 0)
    def _():
        step(wb_s, wa_s)

    @pl.when(e % 2 == 1)
    def _():
        step(wa_s, wb_s)

    @pl.when(e == last)
    def _():
        o_ref[0] = x_ref[0] + ga_ref[0] * acc_s[...].T


def _peer_dense(h2, u_bf16, vt_bf16, s1, c1, s2, st, x, ga):
    b, L, d = x.shape
    tm = min(L, PEER_TM)
    tok = lambda: pl.BlockSpec((1, tm, d), lambda bi, i, e: (bi, i, 0))
    sspec = lambda: pl.BlockSpec((1, P_HEADS, N_KEYS, tm), lambda bi, i, e: (bi, 0, 0, i))
    ne = N_EXPERTS // PEER_TE
    return pl.pallas_call(
        _peer_dense_kernel,
        grid=(b, L // tm, ne + 1),
        in_specs=[tok(),
                  pl.BlockSpec((PEER_TE, d), lambda bi, i, e: (jnp.minimum(e, ne - 1), 0)),
                  pl.BlockSpec((d, PEER_TE), lambda bi, i, e: (0, jnp.maximum(e - 1, 0))),
                  sspec(), sspec(), sspec(),
                  pl.BlockSpec((1, P_HEADS, STAT_ROWS, tm), lambda bi, i, e: (bi, 0, 0, i)),
                  tok(),
                  pl.BlockSpec((1, 1, d), lambda bi, i, e: (bi, 0, 0))],
        out_specs=tok(),
        out_shape=jax.ShapeDtypeStruct((b, L, d), F32),
        scratch_shapes=[pltpu.VMEM((d, tm), F32), pltpu.VMEM((P_HEADS, N_KEYS, tm), F32),
                        pltpu.VMEM((PEER_TE, tm), F32), pltpu.VMEM((PEER_TE, tm), BF16),
                        pltpu.VMEM((PEER_TE, tm), BF16), pltpu.VMEM((2, PEER_TE // N_KEYS, P_HEADS, 1, tm), F32)],
        compiler_params=pltpu.CompilerParams(
            dimension_semantics=("arbitrary", "arbitrary", "arbitrary"),
            vmem_limit_bytes=56 * 1024 * 1024),
        name="peer_dense",
    )(h2, u_bf16, vt_bf16, s1, c1, s2, st, x, ga)


def _peer_block(x, g, scale, shift, ga, pw):
    h2, s1, c1, s2, st = _peer_route(x, g, scale, shift, pw["wqt"], pw["keys"])
    return _peer_dense(h2, pw["u"], pw["vt"], s1, c1, s2, st, x, ga)


def _final_norm_kernel(x_ref, g_ref, o_ref):
    x = x_ref[0]
    o_ref[0] = x * lax.rsqrt(jnp.mean(x * x, axis=-1, keepdims=True) + NORM_EPS) * g_ref[...]


def _final_norm(x, g):
    b, L, d = x.shape
    tm = 512
    tok = pl.BlockSpec((1, tm, d), lambda bi, i: (bi, i, 0))
    return pl.pallas_call(
        _final_norm_kernel,
        grid=(b, L // tm),
        in_specs=[tok, pl.BlockSpec((1, d), lambda bi, i: (0, 0))],
        out_specs=tok,
        out_shape=jax.ShapeDtypeStruct((b, L, d), F32),
        compiler_params=_cparams(("arbitrary", "arbitrary")),
        name="final_norm",
    )(x, g.reshape(1, d))


HY_N2 = 256
HY_CT = 1024
HY_S = 8
HY_OT = 512


def _hyena_filter(L, hy_p):
    w1, b1, w2, b2, w3, freq = hy_p
    t = jnp.linspace(0.0, 1.0, L, dtype=F32)[:, None]
    bands = jnp.linspace(1e-4, HY_BANDS - 1, HY_BANDS, dtype=F32)[None, :]
    ang = (2.0 * math.pi / L) * jnp.arange(L, dtype=F32)[:, None] * bands
    z = jnp.concatenate([t, jnp.cos(ang), -jnp.sin(ang)], axis=-1)
    hdn = jnp.sin(freq * (z @ w1 + b1))
    hdn = jnp.sin(freq * (hdn @ w2 + b2))
    pos = jnp.arange(L)
    posb = (L - pos) % L
    hdn2 = jnp.stack([hdn, hdn[posb]])
    t2 = jnp.stack([t, t[posb]])
    deltas = jnp.abs(jnp.linspace(math.log(HY_TARGET) / HY_SLOW_PCT, math.log(HY_TARGET) / HY_FAST_PCT,
                                  HY_D, dtype=F32))
    w3r = w3.reshape(HY_HIDDEN, HY_ORDER, 2, HY_D)
    first = (jnp.arange(L) == 0)[None, :, None]
    side = jnp.arange(2)[:, None, None]
    out = []
    for o in range(HY_ORDER):
        h = jnp.einsum("slh,hsc->slc", hdn2, w3r[:, o]) * jnp.exp(-t2 * deltas)
        h = h * lax.rsqrt(jnp.sum(h * h, axis=(0, 1), keepdims=True))
        g = h + jnp.where(first & (side == 0), h[1:2, 0:1, :], 0.0)
        g = jnp.where(first & (side == 1), 0.0, g)
        out.append(g.reshape(2 * L, HY_D))
    return out


def _cis(idx, n):
    ang = (2.0 * math.pi / n) * (idx % n).astype(F32)
    return jnp.cos(ang), -jnp.sin(ang)


def _block_complex(re, im):
    return jnp.concatenate([jnp.concatenate([re, -im], axis=-1),
                            jnp.concatenate([im, re], axis=-1)], axis=-2)


def _dft_mats(n1, n2, kin):
    k = jnp.arange(n1, dtype=jnp.int32)
    c1, s1 = _cis(k[:, None] * k[None, :], n1)
    half = max(n1 // 2, 1)
    eye = jnp.eye(HY_S, dtype=F32)
    outer = jnp.kron(_block_complex(c1[:, :half], s1[:, :half]), eye).astype(BF16)
    outer_real = jnp.kron(jnp.concatenate([c1, s1], axis=0), eye).astype(BF16)
    m2 = jnp.arange(n2, dtype=jnp.int32)
    freq = k[:, None, None] + n1 * m2[None, :, None]
    gr, gi = _cis(m2[None, None, :] * freq, n1 * n2)
    inner = _block_complex(gr[:, :, :kin], gi[:, :, :kin]).astype(BF16)
    inner_real = jnp.concatenate([gr, gi], axis=1).astype(BF16)
    return {"n1": n1, "outer": outer, "outer_real": outer_real, "inner": inner, "inner_real": inner_real}


def _hy_outer_fwd_kernel(x_ref, m_ref, o_ref):
    ct = x_ref.shape[-1]
    x = x_ref[0].reshape(-1, ct).astype(BF16)
    f = jnp.dot(m_ref[...], x, preferred_element_type=F32)
    o_ref[...] = f.reshape(o_ref.shape)


def _hy_outer_fwd(x, which, mat8):
    _, groups, r, n2, c = x.shape
    n1 = mat8.shape[0] // (2 * HY_S)
    ct = min(c, HY_OT)
    return pl.pallas_call(
        _hy_outer_fwd_kernel,
        grid=(n2 // HY_S, c // ct),
        in_specs=[pl.BlockSpec((1, groups, r, HY_S, ct), lambda q, j: (which, 0, 0, q, j)),
                  pl.BlockSpec(mat8.shape, lambda q, j: (0, 0))],
        out_specs=pl.BlockSpec((2, n1, HY_S, ct), lambda q, j: (0, 0, q, j)),
        out_shape=jax.ShapeDtypeStruct((2, n1, n2, c), F32),
        compiler_params=_cparams(("arbitrary", "arbitrary")),
        name="hy_outer_fwd",
    )(x, mat8)


def _hy_inner_kernel(*refs, conv, scale):
    if conv:
        a_ref, g_ref, s_ref, o_ref = refs
    else:
        a_ref, g_ref, o_ref = refs
    parts = [a_ref[p, 0] for p in range(a_ref.shape[0])]
    a = (jnp.concatenate(parts, axis=0) if len(parts) > 1 else parts[0]).astype(BF16)
    g = g_ref[0]
    f = jnp.dot(g, a, preferred_element_type=F32)
    ko = f.shape[0] // 2
    fr, fi = f[:ko], f[ko:]
    if not conv:
        o_ref[0, 0] = fr * scale
        o_ref[1, 0] = fi * scale
    else:
        sr, si = s_ref[0, 0], s_ref[1, 0]
        p = jnp.concatenate([fr * sr - fi * si, fr * si + fi * sr], axis=0).astype(BF16)
        d = lax.dot_general(g, p, (((0,), (0,)), ((), ())), preferred_element_type=F32)
        kin = d.shape[0] // 2
        o_ref[0, 0] = d[:kin].astype(o_ref.dtype)
        o_ref[1, 0] = d[kin:].astype(o_ref.dtype)


def _hy_inner(a, gmat, spec, out_dtype, scale=1.0):
    p, n1, kin, c = a.shape
    ko = gmat.shape[1] // 2
    conv = spec is not None
    ct = min(c, HY_CT)
    in_specs = [pl.BlockSpec((p, 1, kin, ct), lambda k, j: (0, k, 0, j)),
                pl.BlockSpec((1,) + gmat.shape[1:], lambda k, j: (k, 0, 0))]
    args = [a, gmat]
    if conv:
        in_specs.append(pl.BlockSpec((2, 1, ko, ct), lambda k, j: (0, k, 0, j)))
        args.append(spec)
    rows = kin if conv else ko
    return pl.pallas_call(
        functools.partial(_hy_inner_kernel, conv=conv, scale=scale),
        grid=(n1, c // ct),
        in_specs=in_specs,
        out_specs=pl.BlockSpec((2, 1, rows, ct), lambda k, j: (0, k, 0, j)),
        out_shape=jax.ShapeDtypeStruct((2, n1, rows, c), out_dtype),
        compiler_params=_cparams(("arbitrary", "arbitrary")),
        name="hy_inner_conv" if conv else "hy_inner_spec",
    )(*args)


def _hy_outer_inv_kernel(d_ref, m_ref, v_ref, x_ref, sk_ref, o_ref):
    ct = d_ref.shape[-1]
    d = d_ref[...].reshape(-1, ct).astype(BF16)
    y = lax.dot_general(m_ref[...], d, (((0,), (0,)), ((), ())), preferred_element_type=F32)
    y = y.reshape(o_ref.shape[1:])
    o_ref[0] = x_ref[0] * (y + v_ref[0] * sk_ref[...])


def _hy_outer_inv(d, mat8, src, si, gate, gi, skip_row):
    _, n1, n2, c = d.shape
    half = n1 // 2
    ct = min(c, HY_OT)
    blk = lambda which: pl.BlockSpec((1, 2, half, HY_S, ct), lambda q, j: (which, 0, 0, q, j))
    return pl.pallas_call(
        _hy_outer_inv_kernel,
        grid=(n2 // HY_S, c // ct),
        in_specs=[pl.BlockSpec((2, n1, HY_S, ct), lambda q, j: (0, 0, q, j)),
                  pl.BlockSpec(mat8.shape, lambda q, j: (0, 0)),
                  blk(si), blk(gi),
                  pl.BlockSpec((1, ct), lambda q, j: (0, j))],
        out_specs=blk(0),
        out_shape=jax.ShapeDtypeStruct((1, 2, half, n2, c), F32),
        compiler_params=_cparams(("arbitrary", "arbitrary")),
        name="hy_outer_inv",
    )(d, mat8, src, gate, skip_row)


def _hyena_spectrum(g, mats):
    n, c = g[0].shape
    n1 = mats["n1"]
    n2 = n // n1
    out = []
    for go in g:
        if n1 > 1:
            a = _hy_outer_fwd(go.reshape(1, 1, n1, n2, c), 0, mats["outer_real"])
            out.append(_hy_inner(a, mats["inner"], None, F32, 1.0 / n))
        else:
            out.append(_hy_inner(go.reshape(1, 1, n, c), mats["inner_real"], None, F32, 1.0 / n))
    return out


def _hy_gate_kernel(y_ref, v_ref, x_ref, sk_ref, o_ref):
    o_ref[...] = x_ref[...] * (y_ref[...] + v_ref[...] * sk_ref[...])


def _hy_gate(y, v, x, skip_row):
    b, L, c = y.shape
    tok = pl.BlockSpec((1, L, c), lambda bi: (bi, 0, 0))
    return pl.pallas_call(
        _hy_gate_kernel,
        grid=(b,),
        in_specs=[tok, tok, tok, pl.BlockSpec((1, c), lambda bi: (0, 0))],
        out_specs=tok,
        out_shape=jax.ShapeDtypeStruct((b, L, c), F32),
        compiler_params=_cparams(("arbitrary",)),
        name="hy_gate",
    )(y, v, x, skip_row)


def _hyena_mixer(hy, spec, skip, mats):
    _, b, L, c = hy.shape
    assert b == 2, "the two batch entries are packed as one complex signal"
    n1 = mats["n1"]
    if n1 == 1:
        z = hy[0]
        for o in range(HY_ORDER):
            y = _hy_inner(z.reshape(2, 1, L, c), mats["inner"], spec[o], F32).reshape(b, L, c)
            z = _hy_gate(y, z, hy[o + 1], skip[o].reshape(1, c))
        return z
    n2 = 2 * L // n1
    src, si = hy.reshape(3, b, n1 // 2, n2, c), 0
    gate = src
    for o in range(HY_ORDER):
        a = _hy_outer_fwd(src, si, mats["outer"])
        d = _hy_inner(a, mats["inner"], spec[o], F32)
        src, si = _hy_outer_inv(d, mats["outer"], src, si, gate, o + 1, skip[o].reshape(1, c)), 0
    return src.reshape(b, L, c)


def _token_mixing(x, g, scale, shift, ga, lw, n_seg, s0_f, s0_b, hy_p):
    b, L, _ = x.shape
    seg = L // n_seg
    proj = functools.partial(_in_proj, x, g, scale, shift, lw["w_in"], lw["conv_w"], seg)
    y_rw, s_f, s_b = _rwkv_mixer(proj("rw"), lw["rw"], s0_f, s0_b)
    if hy_p is None:
        return None, s_f, s_b
    n = 2 * L
    n1 = n // HY_N2 if n // HY_N2 >= 16 else 1
    mats = _dft_mats(n1, n // n1, L if n1 == 1 else n // n1)
    spec = _hyena_spectrum(_hyena_filter(L, hy_p), mats)
    y_hy = _hyena_mixer(proj("hy"), spec, lw["hy_skip"], mats)
    return _out_proj(y_rw, y_hy, proj("gate"), x, ga, lw["w_out"]), s_f, s_b


def kernel(x, c, ctx, c_ctx, ada_w, ada_b, norm1_g, norm2_g, w_in, conv_w, rw_w0, rw_w2, rw_a0, rw_a2, rw_g2, rw_k_k, rw_k_a, rw_r_k, rw_ln_w, rw_ln_b, hy_w1, hy_b1, hy_w2, hy_b2, hy_w3, hy_freq, hy_skip, w_out, peer_wq, peer_keys, peer_u, peer_v, final_g):
    b, L, d = x.shape
    depth = ada_w.shape[0]
    ctx_len = ctx.shape[1]
    s_zero = jnp.zeros((b, N_GROUPS, GROUP_W, GROUP_W), F32)
    cond = jnp.concatenate([c, c_ctx[None, :]], axis=0)
    cond = jnp.pad(jax.nn.silu(cond), ((0, 8 - (b + 1)), (0, 0)))
    for l in range(depth):
        last = l == depth - 1
        mod = _matmul(cond, ada_w[l], 1024) + ada_b[l]
        mx = [m[:, None, :] for m in jnp.split(mod[:b], 6, axis=-1)]
        mc = [jnp.broadcast_to(m[None, :, :], (b, 1, d)) for m in jnp.split(mod[b:b + 1], 6, axis=-1)]
        lw = {
            "w_in": _pack_cols(w_in[l]).astype(BF16),
            "conv_w": _pack_cols(conv_w[l]),
            "rw": _rwkv_params(l, rw_w0, rw_w2, rw_a0, rw_a2, rw_g2, rw_k_k, rw_k_a, rw_r_k, rw_ln_w, rw_ln_b),
            "hy_skip": hy_skip[l],
            "w_out": w_out[l].astype(BF16),
        }
        hy_p = (hy_w1[l], hy_b1[l], hy_w2[l], hy_b2[l], hy_w3[l], hy_freq[l])
        pw = {
            "wqt": peer_wq[l].T.astype(BF16),
            "keys": peer_keys[l].astype(BF16),
            "u": peer_u[l].astype(BF16),
            "vt": peer_v[l].T.astype(BF16),
        }
        g1 = norm1_g[l].reshape(1, d)
        g2 = norm2_g[l].reshape(1, d)

        ctx_new, s_f, s_b = _token_mixing(ctx, g1, mc[1], mc[0], mc[2], lw, 1, s_zero, s_zero,
                                          None if last else hy_p)
        if not last:
            ctx = _peer_block(ctx_new, g2, mc[4], mc[3], mc[5], pw)

        x, _, _ = _token_mixing(x, g1, mx[1], mx[0], mx[2], lw, L // GRID_W, s_f, s_b, hy_p)
        x = _peer_block(x, g2, mx[4], mx[3], mx[5], pw)
    return _final_norm(x, final_g)
```

```python
import functools
import math

import numpy as np
import jax
import jax.numpy as jnp
from jax import lax
from jax.experimental import pallas as pl
from jax.experimental.pallas import tpu as pltpu

F32 = jnp.float32
BF16 = jnp.bfloat16

D_MODEL = 2048
GRID_W = 64
NORM_EPS = 1e-6
RW_N = 64
RW_H = D_MODEL // RW_N
RW_D = D_MODEL
W_LORA = 96
A_LORA = 96
G_LORA = 64
RW_GN_EPS = 64e-5
HY_D = D_MODEL
HY_ORDER = 2
HY_EMB = 33
HY_BANDS = (HY_EMB - 1) // 2
HY_HIDDEN = 64
HY_TARGET = 1e-2
HY_FAST_PCT = 0.3
HY_SLOW_PCT = 1.5
P_HEADS = 8
N_KEYS = 128
N_EXPERTS = N_KEYS * N_KEYS
P_TOPK = 16
D_KEY = 256

LANES = 128
MXU_DIM = 256
VMEM_LIMIT = 48 * 1024 * 1024

COL_K, COL_V, COL_R = 0, RW_D, 2 * RW_D
COL_SMALL = 3 * RW_D
SMALL_W = 1024
COL_GATE = COL_SMALL + SMALL_W
COL_HY = COL_GATE + 2 * D_MODEL
N_PACKED = COL_HY + 3 * HY_D
PROJ_TN = 512
PROJ_RC = 256

CHUNK = 64
HEADS_PER_GROUP = MXU_DIM // RW_N
GROUP_W = HEADS_PER_GROUP * RW_N
N_GROUPS = RW_D // GROUP_W
STACK = HEADS_PER_GROUP * CHUNK
GROUPS_PER_STEP = 8
STEP_W = GROUPS_PER_STEP * GROUP_W
RWKV_TB = 128


def _cparams(sem):
    return pltpu.CompilerParams(dimension_semantics=sem, vmem_limit_bytes=VMEM_LIMIT)


def _rwkv_consts(reverse):
    t = np.arange(CHUNK)
    tt, ss = t[:, None], t[None, :]
    strict = (tt < ss) if reverse else (tt > ss)
    eye = tt == ss
    levels = []
    m = 1
    while m < CHUNK:
        levels.append(strict & ((tt // (2 * m)) == (ss // (2 * m))) & ((tt // m) != (ss // m)))
        m *= 2
    lanes = lambda a: np.tile(a, (1,) * (a.ndim - 1) + (HEADS_PER_GROUP,))
    hb = np.arange(STACK) // CHUNK
    same = hb[:, None] == hb[None, :]
    tri = (tt <= ss) if reverse else (tt >= ss)
    f = lambda a: jnp.asarray(a.astype(np.float32))
    return (f(lanes(strict)), f(lanes(strict | eye)), f(same), f(lanes(eye)),
            f(lanes(np.stack(levels))), f(tri))


def _rwkv_kernel(k_ref, v_ref, r_ref, wd_ref, ad_ref, gd_ref, yf_ref,
                 w0_ref, w2_ref, a0_ref, a2_ref, g2_ref, kk_ref, ka_ref, rk_ref, lnw_ref, lnb_ref,
                 ms_ref, mi_ref, same_ref, eye_ref, lev_ref, tri_ref, s0_ref,
                 y_ref, sout_ref,
                 S_ref, lw_s, kt_s, kn_s, b_s, yacc_s, *, reverse, final, nchunk):
    step = pl.program_id(2)

    @pl.when(step == 0)
    def _():
        S_ref[...] = s0_ref[0]

    same = same_ref[...]
    gsl = [slice(g * GROUP_W, (g + 1) * GROUP_W) for g in range(GROUPS_PER_STEP)]

    same_b = same.astype(BF16)

    def split(x, pieces):
        out = []
        for _ in range(pieces):
            p = x.astype(BF16)
            out.append(p)
            x = x - p.astype(F32)
        return out

    def head_sum(x):
        cols = []
        for s in gsl:
            hi, lo = split(x[:, s], 2)
            cols.append(jnp.dot(hi, same_b, preferred_element_type=F32)
                        + jnp.dot(lo, same_b, preferred_element_type=F32))
        return jnp.concatenate(cols, axis=1)

    k = k_ref[0]
    wraw = w0_ref[...] + jnp.dot(jnp.tanh(wd_ref[0]).astype(BF16), w2_ref[...],
                                 preferred_element_type=F32)
    lw_s[...] = -jax.nn.sigmoid(wraw) * math.exp(-0.5)
    a = jax.nn.sigmoid(a0_ref[...] + jnp.dot(ad_ref[0].astype(BF16), a2_ref[...],
                                             preferred_element_type=F32))
    kt_s[...] = k * (1.0 + (a - 1.0) * ka_ref[...])
    kn = k * kk_ref[...]
    kn = kn * lax.rsqrt(jnp.maximum(head_sum(kn * kn), 1e-24))
    kn_s[...] = kn
    b_s[...] = kn * a

    ms = ms_ref[...]
    mi = mi_ref[...]
    tri = tri_ref[...].astype(BF16)
    mid = CHUNK // 2 if reverse else CHUNK // 2 - 1
    last = 0 if reverse else CHUNK - 1

    def bd(x):
        return jnp.concatenate([x.astype(BF16)] * HEADS_PER_GROUP, axis=0) * same_b

    def mm(a, w):
        return jnp.dot(a.astype(BF16), w, preferred_element_type=F32)

    nt = (((1,), (1,)), ((), ()))
    tn = (((0,), (0,)), ((), ()))

    def chunk_group(rows, g):
        cols = gsl[g]
        lw = lw_s[rows, cols]
        kt = kt_s[rows, cols]
        kn = kn_s[rows, cols]
        b = b_s[rows, cols]
        r = r_ref[0, rows, cols]
        v = v_ref[0, rows, cols]
        c = sum(jnp.dot(tri, p, preferred_element_type=F32) for p in split(lw, 3))
        yield
        cp = c - lw
        rho = c[mid:mid + 1, :]
        cend = c[last:last + 1, :]
        einv = jnp.exp(rho - c)
        eend = jnp.exp(cend - c)
        lhs = jnp.concatenate([kn * jnp.exp(cp - rho), r * jnp.exp(c - rho)], axis=0).astype(BF16)
        rhs = jnp.concatenate([bd(kt * einv), bd(b * einv)], axis=0)
        ag = lax.dot_general(lhs, rhs, nt, preferred_element_type=F32)
        yield
        ak = ag[:CHUNK, :STACK] * ms
        ab = ag[:CHUNK, STACK:] * ms
        gk = ag[CHUNK:, :STACK] * mi
        gb = ag[CHUNK:, STACK:] * mi
        tm = eye_ref[...] - ab * lev_ref[0]
        for lv in range(1, lev_ref.shape[0]):
            te = mm(tm, bd(ab * lev_ref[lv]))
            yield
            tm = tm - mm(te, bd(tm))
            yield
        S = S_ref[g]
        lhs_s = jnp.concatenate([kn * jnp.exp(cp), r * jnp.exp(c)], axis=0).astype(BF16)
        qs = lax.dot_general(lhs_s, S.astype(BF16), nt, preferred_element_type=F32)
        vs = bd(v)
        rhs_u = qs[:CHUNK] + mm(ak, vs)
        yield
        us = mm(tm, bd(rhs_u))
        yield
        y = qs[CHUNK:] + mm(jnp.concatenate([gk, gb], axis=1), jnp.concatenate([vs, bd(-us)], axis=0))
        yield
        if final:
            yacc_s[rows, cols] = y
        else:
            y_ref[0, rows, cols] = y
        vu = jnp.concatenate([v, -us], axis=0).astype(BF16)
        kb = jnp.concatenate([kt * eend, b * eend], axis=0).astype(BF16)
        ds = lax.dot_general(vu, kb, tn, preferred_element_type=F32)
        S_ref[g] = S * jnp.exp(cend) + ds * same

    def chunk(ci, carry):
        c_idx = (nchunk - 1 - ci) if reverse else ci
        rows = pl.ds(pl.multiple_of(c_idx * CHUNK, CHUNK), CHUNK)
        pending = [chunk_group(rows, g) for g in range(GROUPS_PER_STEP)]
        while pending:
            pending = [gen for gen in pending if next(gen, True) is None]
        return carry

    lax.fori_loop(0, nchunk, chunk, 0)
    sout_ref[0] = S_ref[...]

    if final:
        y = yf_ref[0] + yacc_s[...]
        inv_n = 1.0 / RW_N
        mu = head_sum(y) * inv_n
        d = y - mu
        var = head_sum(d * d) * inv_n
        yn = d * lax.rsqrt(var + RW_GN_EPS) * lnw_ref[...] + lnb_ref[...]
        bonus = head_sum(r_ref[0] * k * rk_ref[...]) * v_ref[0]
        g = jnp.dot(jax.nn.sigmoid(gd_ref[0]).astype(BF16), g2_ref[...], preferred_element_type=F32)
        y_ref[0] = (yn + bonus) * g


def _pad_rows(w, rows):
    return jnp.pad(w, ((0, rows - w.shape[0]), (0, 0)))


def _rwkv_direction(mix, yf, prm, s0, d, *, final):
    b, L, _ = mix.shape
    reverse = d == 1
    tb = min(L, RWKV_TB)
    nb = L // tb
    nchunk = tb // CHUNK
    ms, mi, same, eye, lev, tri = _rwkv_consts(reverse)
    blk = (lambda i: nb - 1 - i) if reverse else (lambda i: i)
    sm = COL_SMALL // LANES

    def col(c0):
        return pl.BlockSpec((1, tb, STEP_W), lambda bi, g, i: (bi, blk(i), c0 // STEP_W + g))

    def small(j):
        return pl.BlockSpec((1, tb, LANES), lambda bi, g, i: (bi, blk(i), sm + j))

    vec = pl.BlockSpec((1, STEP_W), lambda bi, g, i: (0, g))
    lora = pl.BlockSpec((LANES, STEP_W), lambda bi, g, i: (0, g))
    const2 = lambda shp: pl.BlockSpec(shp, lambda bi, g, i: (0,) * len(shp))
    state = pl.BlockSpec((1, GROUPS_PER_STEP, GROUP_W, GROUP_W), lambda bi, g, i: (bi, g, 0, 0))
    yspec = pl.BlockSpec((1, tb, STEP_W), lambda bi, g, i: (bi, blk(i), g))

    kern = functools.partial(_rwkv_kernel, reverse=reverse, final=final, nchunk=nchunk)
    if yf is None:
        yf, yf_spec = mix, col(COL_K)
    else:
        yf_spec = yspec
    y, s_fin = pl.pallas_call(
        kern,
        grid=(b, N_GROUPS // GROUPS_PER_STEP, nb),
        in_specs=[col(COL_K), col(COL_V), col(COL_R), small(d), small(2 + d), small(4), yf_spec,
                  vec, lora, vec, lora, lora, vec, vec, vec, vec, vec,
                  const2((CHUNK, STACK)), const2((CHUNK, STACK)), const2((STACK, STACK)),
                  const2((CHUNK, STACK)), const2(tuple(lev.shape)), const2((CHUNK, CHUNK)), state],
        out_specs=[yspec, state],
        out_shape=[jax.ShapeDtypeStruct((b, L, RW_D), F32),
                   jax.ShapeDtypeStruct((b, N_GROUPS, GROUP_W, GROUP_W), F32)],
        scratch_shapes=[pltpu.VMEM((GROUPS_PER_STEP, GROUP_W, GROUP_W), F32)]
        + [pltpu.VMEM((tb, STEP_W), F32)] * 5,
        compiler_params=_cparams(("arbitrary", "arbitrary", "arbitrary")),
        name="rwkv_bwd" if reverse else "rwkv_fwd",
    )(mix, mix, mix, mix, mix, mix, yf,
      prm["w0"][d], prm["w2"][d], prm["a0"][d], prm["a2"][d], prm["g2"], prm["k_k"], prm["k_a"],
      prm["r_k"], prm["ln_w"], prm["ln_b"],
      ms, mi, same, eye, lev, tri, s0)
    return y, s_fin


def _rwkv_params(l, rw_w0, rw_w2, rw_a0, rw_a2, rw_g2, rw_k_k, rw_k_a, rw_r_k, rw_ln_w, rw_ln_b):
    row = lambda v: v.reshape(1, RW_D)
    return {
        "w0": [row(rw_w0[l, d]) for d in range(2)],
        "w2": [_pad_rows(rw_w2[l, d], LANES).astype(BF16) for d in range(2)],
        "a0": [row(rw_a0[l, d]) for d in range(2)],
        "a2": [_pad_rows(rw_a2[l, d], LANES).astype(BF16) for d in range(2)],
        "g2": _pad_rows(rw_g2[l], LANES).astype(BF16),
        "k_k": row(rw_k_k[l]), "k_a": row(rw_k_a[l]), "r_k": row(rw_r_k[l]),
        "ln_w": row(rw_ln_w[l]), "ln_b": row(rw_ln_b[l]),
    }


def _rwkv_mixer(mix, prm, s0_f, s0_b):
    b, L, _ = mix.shape
    y_f, s_f = _rwkv_direction(mix, None, prm, s0_f, 0, final=False)
    y, s_b = _rwkv_direction(mix, y_f, prm, s0_b, 1, final=True)
    return y, s_f, s_b


def _matmul_kernel(a_ref, b_ref, o_ref):
    o_ref[...] = jnp.dot(a_ref[...].astype(BF16), b_ref[...].astype(BF16), preferred_element_type=F32)


def _matmul(a, b, tn):
    m, k = a.shape
    n = b.shape[1]
    return pl.pallas_call(
        _matmul_kernel,
        grid=(n // tn,),
        in_specs=[pl.BlockSpec((m, k), lambda j: (0, 0)), pl.BlockSpec((k, tn), lambda j: (0, j))],
        out_specs=pl.BlockSpec((m, tn), lambda j: (0, j)),
        out_shape=jax.ShapeDtypeStruct((m, n), F32),
        compiler_params=_cparams(("arbitrary",)),
        name="matmul",
    )(a, b)


def _modulate(x, g, scale, shift):
    y = x * lax.rsqrt(jnp.mean(x * x, axis=-1, keepdims=True) + NORM_EPS)
    return y * g * (1.0 + scale) + shift


def _pack_cols(m):
    z = lambda n: jnp.zeros(m.shape[:-1] + (n,), m.dtype)
    s = lambda a, b: m[..., a:b]
    o = 2 * RW_D
    lo = [s(o + j * W_LORA, o + (j + 1) * W_LORA) for j in range(4)]
    r0 = o + 2 * W_LORA + 2 * A_LORA
    g0 = r0 + RW_D
    h0 = g0 + G_LORA
    parts = [s(0, RW_D), s(RW_D, 2 * RW_D), s(r0, r0 + RW_D)]
    for p in lo:
        parts += [p, z(LANES - W_LORA)]
    parts += [s(g0, g0 + G_LORA), z(LANES - G_LORA), z(SMALL_W - 5 * LANES)]
    if m.shape[-1] > h0 + 3 * HY_D:
        parts.append(s(h0 + 3 * HY_D, h0 + 3 * HY_D + 2 * D_MODEL))
    else:
        parts.append(z(2 * D_MODEL))
    parts.append(s(h0, h0 + 3 * HY_D))
    return jnp.concatenate(parts, axis=-1)


def _inproj_kernel(x_ref, g_ref, sc_ref, sh_ref, w_ref, cw_ref, o_ref, h_s, *, seg, gate):
    j = pl.program_id(2)

    @pl.when(j == 0)
    def _():
        h_s[...] = _modulate(x_ref[0], g_ref[...], sc_ref[0], sh_ref[0]).astype(BF16)

    out_blk = (0,) * (len(o_ref.shape) - 2)
    tm = h_s.shape[0]
    rc = min(tm, max(seg, PROJ_RC))
    for r0 in range(0, tm, rc):
        p = jnp.dot(h_s[r0:r0 + rc, :], w_ref[...], preferred_element_type=F32)
        if gate:
            res = jax.nn.sigmoid(p)
        else:
            row = lax.broadcasted_iota(jnp.int32, p.shape, 0) % seg
            prev = jnp.where(row == 0, 0.0, pltpu.roll(p, 1, 0))
            nxt = jnp.where(row == seg - 1, 0.0, pltpu.roll(p, rc - 1, 0))
            res = cw_ref[0:1, :] * prev + cw_ref[1:2, :] * p + cw_ref[2:3, :] * nxt
        o_ref[out_blk + (slice(r0, r0 + rc), slice(None))] = res


def _in_proj(x, g, scale, shift, w_packed, cw_packed, seg, part):
    b, L, d = x.shape
    tm = min(L, 1024)
    c0, c1 = {"rw": (0, COL_GATE), "gate": (COL_GATE, COL_HY), "hy": (COL_HY, N_PACKED)}[part]
    t0, nt = c0 // PROJ_TN, (c1 - c0) // PROJ_TN
    if part == "hy":
        per = HY_D // PROJ_TN
        out_spec = pl.BlockSpec((1, 1, tm, PROJ_TN), lambda bi, i, j: (j // per, bi, i, j % per))
        out_shape = jax.ShapeDtypeStruct((3, b, L, HY_D), F32)
    else:
        out_spec = pl.BlockSpec((1, tm, PROJ_TN), lambda bi, i, j: (bi, i, j))
        out_shape = jax.ShapeDtypeStruct((b, L, c1 - c0), F32)
    return pl.pallas_call(
        functools.partial(_inproj_kernel, seg=seg, gate=part == "gate"),
        grid=(b, L // tm, nt),
        in_specs=[pl.BlockSpec((1, tm, d), lambda bi, i, j: (bi, i, 0)),
                  pl.BlockSpec((1, d), lambda bi, i, j: (0, 0)),
                  pl.BlockSpec((1, 1, d), lambda bi, i, j: (bi, 0, 0)),
                  pl.BlockSpec((1, 1, d), lambda bi, i, j: (bi, 0, 0)),
                  pl.BlockSpec((d, PROJ_TN), lambda bi, i, j: (0, t0 + j)),
                  pl.BlockSpec((3, PROJ_TN), lambda bi, i, j: (0, t0 + j))],
        out_specs=out_spec,
        out_shape=out_shape,
        scratch_shapes=[pltpu.VMEM((tm, d), BF16)],
        compiler_params=_cparams(("arbitrary", "arbitrary", "arbitrary")),
        name="in_proj_" + part,
    )(x, g, scale, shift, w_packed, cw_packed)


def _outproj_kernel(yrw_ref, yhy_ref, grw_ref, ghy_ref, x_ref, ga_ref, w_ref, o_ref):
    y = grw_ref[0] * yrw_ref[0] + ghy_ref[0] * yhy_ref[0]
    o_ref[0] = x_ref[0] + ga_ref[0] * jnp.dot(y.astype(BF16), w_ref[...], preferred_element_type=F32)


def _out_proj(y_rw, y_hy, gates, x, ga, w_out_bf16):
    b, L, d = x.shape
    tm = 256
    tok = pl.BlockSpec((1, tm, d), lambda bi, i: (bi, i, 0))
    return pl.pallas_call(
        _outproj_kernel,
        grid=(b, L // tm),
        in_specs=[tok, tok,
                  tok,
                  pl.BlockSpec((1, tm, d), lambda bi, i: (bi, i, 1)),
                  tok,
                  pl.BlockSpec((1, 1, d), lambda bi, i: (bi, 0, 0)),
                  pl.BlockSpec((d, d), lambda bi, i: (0, 0))],
        out_specs=tok,
        out_shape=jax.ShapeDtypeStruct((b, L, d), F32),
        compiler_params=_cparams(("arbitrary", "arbitrary")),
        name="out_proj",
    )(y_rw, y_hy, gates, gates, x, ga, w_out_bf16)


PEER_TM = 512
PEER_TE = 512
PEER_SUB = 64
PEER_PAIR = 2
STAT_ROWS = 8


def _top_rows(s, n):
    rows = lax.broadcasted_iota(jnp.int32, (n, s.shape[1]), 0)
    out = jnp.zeros((n, s.shape[1]), F32)
    cur = s
    for i in range(n):
        m = jnp.max(cur, axis=0, keepdims=True)
        out = jnp.where(rows == i, m, out)
        cur = jnp.where(cur == m, -jnp.inf, cur)
    return out


def _route_kernel(x_ref, g_ref, sc_ref, sh_ref, wqt_ref, keys_ref, h_ref, s1_ref, c1_ref, s2_ref, st_ref):
    hb = _modulate(x_ref[0], g_ref[...], sc_ref[0], sh_ref[0]).astype(BF16)
    h_ref[0] = hb
    nt = (((1,), (1,)), ((), ()))
    qt = lax.dot_general(wqt_ref[...], hb, nt, preferred_element_type=F32)
    half = D_KEY // 2
    for hd in range(P_HEADS):
        sc = []
        for p in range(2):
            r0 = (hd * 2 + p) * half
            sc.append(jnp.dot(keys_ref[hd, p], qt[r0:r0 + half, :].astype(BF16),
                              preferred_element_type=F32))
        s1_ref[0, hd] = sc[0]
        s2_ref[0, hd] = sc[1]
        t1 = _top_rows(sc[0], P_TOPK)
        t2 = _top_rows(sc[1], P_TOPK)
        r8 = lax.broadcasted_iota(jnp.int32, (8, t1.shape[1]), 0)
        tiles = [t1[0:1] + t2[0:8], t1[0:1] + t2[8:16], t1[1:2] + t2[0:8]]
        for i in range(2, 8):
            tiles.append(jnp.where(r8 < P_TOPK // (i + 1), t1[i:i + 1] + t2[0:8], -jnp.inf))
        tiles.append(t1[8:16] + t2[0:1])
        best = _top_rows(jnp.concatenate(tiles, axis=0), P_TOPK)
        z = jnp.sum(jnp.exp(best - best[0:1]), axis=0, keepdims=True)
        c1_ref[0, hd] = jnp.exp(sc[0] - t1[0:1]) * (1.0 / z)
        st = jnp.where(r8 == 0, best[P_TOPK - 1:P_TOPK], 0.0)
        st_ref[0, hd] = jnp.where(r8 == 1, t2[0:1], st)


def _peer_route(x, g, scale, shift, wqt_bf16, keys_bf16):
    b, L, d = x.shape
    tm = min(L, PEER_TM)
    tok = lambda: pl.BlockSpec((1, tm, d), lambda bi, i: (bi, i, 0))
    rowv = pl.BlockSpec((1, 1, d), lambda bi, i: (bi, 0, 0))
    sspec = pl.BlockSpec((1, P_HEADS, N_KEYS, tm), lambda bi, i: (bi, 0, 0, i))
    return pl.pallas_call(
        _route_kernel,
        grid=(b, L // tm),
        in_specs=[tok(), pl.BlockSpec((1, d), lambda bi, i: (0, 0)), rowv, rowv,
                  pl.BlockSpec(wqt_bf16.shape, lambda bi, i: (0, 0)),
                  pl.BlockSpec(keys_bf16.shape, lambda bi, i: (0, 0, 0, 0))],
        out_specs=[tok(), sspec, sspec, sspec,
                   pl.BlockSpec((1, P_HEADS, STAT_ROWS, tm), lambda bi, i: (bi, 0, 0, i))],
        out_shape=[jax.ShapeDtypeStruct((b, L, d), BF16),
                   jax.ShapeDtypeStruct((b, P_HEADS, N_KEYS, L), F32),
                   jax.ShapeDtypeStruct((b, P_HEADS, N_KEYS, L), F32),
                   jax.ShapeDtypeStruct((b, P_HEADS, N_KEYS, L), F32),
                   jax.ShapeDtypeStruct((b, P_HEADS, STAT_ROWS, L), F32)],
        compiler_params=_cparams(("arbitrary", "arbitrary")),
        name="peer_route",
    )(x, g, scale, shift, wqt_bf16, keys_bf16)


def _peer_dense_kernel(h_ref, u_ref, vt_ref, s1_ref, c1_ref, s2_ref, st_ref, x_ref, ga_ref, o_ref,
                       acc_s, e2_s, ge_s, wa_s, wb_s, row_s):
    e = pl.program_id(2)
    last = pl.num_programs(2) - 1
    tm = h_ref.shape[1]
    n1 = PEER_TE // N_KEYS

    @pl.when(e == 0)
    def _():
        acc_s[...] = jnp.zeros_like(acc_s)
        wb_s[...] = jnp.zeros_like(wb_s)
        for hd in range(P_HEADS):
            e2_s[hd] = jnp.exp(s2_ref[0, hd] - st_ref[0, hd, 1:2, :])

    def step(w_prev, w_cur):
        nt = (((1,), (1,)), ((), ()))
        act = lax.dot_general(u_ref[...], h_ref[0], nt, preferred_element_type=F32)
        ge_s[...] = 0.5 * act * (1.0 + lax.erf(act * (1.0 / math.sqrt(2.0))))
        tile = jnp.minimum(e, last - 1)
        for i in range(n1):
            for hd in range(P_HEADS):
                row_s[0, i, hd] = s1_ref[0, hd, pl.ds(tile * n1 + i, 1), :]
                row_s[1, i, hd] = c1_ref[0, hd, pl.ds(tile * n1 + i, 1), :]
        nblk = (n1 // PEER_PAIR) * (tm // LANES)
        rows = acc_s.shape[0] // nblk
        blk = 0
        for ip in range(0, n1, PEER_PAIR):
            pair = range(ip, ip + PEER_PAIR)
            for strip in range(tm // LANES):
                rs = slice(blk * rows, (blk + 1) * rows)
                part = jnp.dot(vt_ref[rs, :], w_prev[...], preferred_element_type=F32)
                acc_s[rs, :] += part
                anchor = jnp.minimum(jnp.abs(part[0:1, :]), 0.0)
                blk += 1
                cs = slice(strip * LANES, (strip + 1) * LANES)
                for sub in range(N_KEYS // PEER_SUB):
                    ks = slice(sub * PEER_SUB, (sub + 1) * PEER_SUB)
                    gsum = [None] * PEER_PAIR
                    for hd in range(P_HEADS):
                        s2t = s2_ref[0, hd, ks, cs]
                        e2t = e2_s[hd, ks, cs]
                        thr = st_ref[0, hd, 0:1, cs] + anchor[:, cs]
                        for q, i in enumerate(pair):
                            term = jnp.where((s2t + row_s[0, i, hd, :, cs]) >= thr,
                                             e2t * row_s[1, i, hd, :, cs], 0.0)
                            gsum[q] = term if gsum[q] is None else gsum[q] + term
                    for q, i in enumerate(pair):
                        ws = slice(i * N_KEYS + sub * PEER_SUB, i * N_KEYS + (sub + 1) * PEER_SUB)
                        w_cur[ws, cs] = (gsum[q] * ge_s[ws, cs]).astype(BF16)

    @pl.when(e % 2 == 0)
    def _():
        step(wb_s, wa_s)

    @pl.when(e % 2 == 1)
    def _():
        step(wa_s, wb_s)

    @pl.when(e == last)
    def _():
        o_ref[0] = x_ref[0] + ga_ref[0] * acc_s[...].T


def _peer_dense(h2, u_bf16, vt_bf16, s1, c1, s2, st, x, ga):
    b, L, d = x.shape
    tm = min(L, PEER_TM)
    tok = lambda: pl.BlockSpec((1, tm, d), lambda bi, i, e: (bi, i, 0))
    sspec = lambda: pl.BlockSpec((1, P_HEADS, N_KEYS, tm), lambda bi, i, e: (bi, 0, 0, i))
    ne = N_EXPERTS // PEER_TE
    return pl.pallas_call(
        _peer_dense_kernel,
        grid=(b, L // tm, ne + 1),
        in_specs=[tok(),
                  pl.BlockSpec((PEER_TE, d), lambda bi, i, e: (jnp.minimum(e, ne - 1), 0)),
                  pl.BlockSpec((d, PEER_TE), lambda bi, i, e: (0, jnp.maximum(e - 1, 0))),
                  sspec(), sspec(), sspec(),
                  pl.BlockSpec((1, P_HEADS, STAT_ROWS, tm), lambda bi, i, e: (bi, 0, 0, i)),
                  tok(),
                  pl.BlockSpec((1, 1, d), lambda bi, i, e: (bi, 0, 0))],
        out_specs=tok(),
        out_shape=jax.ShapeDtypeStruct((b, L, d), F32),
        scratch_shapes=[pltpu.VMEM((d, tm), F32), pltpu.VMEM((P_HEADS, N_KEYS, tm), F32),
                        pltpu.VMEM((PEER_TE, tm), F32), pltpu.VMEM((PEER_TE, tm), BF16),
                        pltpu.VMEM((PEER_TE, tm), BF16), pltpu.VMEM((2, PEER_TE // N_KEYS, P_HEADS, 1, tm), F32)],
        compiler_params=pltpu.CompilerParams(
            dimension_semantics=("arbitrary", "arbitrary", "arbitrary"),
            vmem_limit_bytes=56 * 1024 * 1024),
        name="peer_dense",
    )(h2, u_bf16, vt_bf16, s1, c1, s2, st, x, ga)


def _peer_block(x, g, scale, shift, ga, pw):
    h2, s1, c1, s2, st = _peer_route(x, g, scale, shift, pw["wqt"], pw["keys"])
    return _peer_dense(h2, pw["u"], pw["vt"], s1, c1, s2, st, x, ga)


def _final_norm_kernel(x_ref, g_ref, o_ref):
    x = x_ref[0]
    o_ref[0] = x * lax.rsqrt(jnp.mean(x * x, axis=-1, keepdims=True) + NORM_EPS) * g_ref[...]


def _final_norm(x, g):
    b, L, d = x.shape
    tm = 512
    tok = pl.BlockSpec((1, tm, d), lambda bi, i: (bi, i, 0))
    return pl.pallas_call(
        _final_norm_kernel,
        grid=(b, L // tm),
        in_specs=[tok, pl.BlockSpec((1, d), lambda bi, i: (0, 0))],
        out_specs=tok,
        out_shape=jax.ShapeDtypeStruct((b, L, d), F32),
        compiler_params=_cparams(("arbitrary", "arbitrary")),
        name="final_norm",
    )(x, g.reshape(1, d))


HY_N2 = 256
HY_CT = 1024
HY_S = 8
HY_OT = 512


def _hyena_filter(L, hy_p):
    w1, b1, w2, b2, w3, freq = hy_p
    t = jnp.linspace(0.0, 1.0, L, dtype=F32)[:, None]
    bands = jnp.linspace(1e-4, HY_BANDS - 1, HY_BANDS, dtype=F32)[None, :]
    ang = (2.0 * math.pi / L) * jnp.arange(L, dtype=F32)[:, None] * bands
    z = jnp.concatenate([t, jnp.cos(ang), -jnp.sin(ang)], axis=-1)
    hdn = jnp.sin(freq * (z @ w1 + b1))
    hdn = jnp.sin(freq * (hdn @ w2 + b2))
    pos = jnp.arange(L)
    posb = (L - pos) % L
    hdn2 = jnp.stack([hdn, hdn[posb]])
    t2 = jnp.stack([t, t[posb]])
    deltas = jnp.abs(jnp.linspace(math.log(HY_TARGET) / HY_SLOW_PCT, math.log(HY_TARGET) / HY_FAST_PCT,
                                  HY_D, dtype=F32))
    w3r = w3.reshape(HY_HIDDEN, HY_ORDER, 2, HY_D)
    first = (jnp.arange(L) == 0)[None, :, None]
    side = jnp.arange(2)[:, None, None]
    out = []
    for o in range(HY_ORDER):
        h = jnp.einsum("slh,hsc->slc", hdn2, w3r[:, o]) * jnp.exp(-t2 * deltas)
        h = h * lax.rsqrt(jnp.sum(h * h, axis=(0, 1), keepdims=True))
        g = h + jnp.where(first & (side == 0), h[1:2, 0:1, :], 0.0)
        g = jnp.where(first & (side == 1), 0.0, g)
        out.append(g.reshape(2 * L, HY_D))
    return out


def _cis(idx, n):
    ang = (2.0 * math.pi / n) * (idx % n).astype(F32)
    return jnp.cos(ang), -jnp.sin(ang)


def _block_complex(re, im):
    return jnp.concatenate([jnp.concatenate([re, -im], axis=-1),
                            jnp.concatenate([im, re], axis=-1)], axis=-2)


def _dft_mats(n1, n2, kin):
    k = jnp.arange(n1, dtype=jnp.int32)
    c1, s1 = _cis(k[:, None] * k[None, :], n1)
    half = max(n1 // 2, 1)
    eye = jnp.eye(HY_S, dtype=F32)
    outer = jnp.kron(_block_complex(c1[:, :half], s1[:, :half]), eye).astype(BF16)
    outer_real = jnp.kron(jnp.concatenate([c1, s1], axis=0), eye).astype(BF16)
    m2 = jnp.arange(n2, dtype=jnp.int32)
    freq = k[:, None, None] + n1 * m2[None, :, None]
    gr, gi = _cis(m2[None, None, :] * freq, n1 * n2)
    inner = _block_complex(gr[:, :, :kin], gi[:, :, :kin]).astype(BF16)
    inner_real = jnp.concatenate([gr, gi], axis=1).astype(BF16)
    return {"n1": n1, "outer": outer, "outer_real": outer_real, "inner": inner, "inner_real": inner_real}


def _hy_outer_fwd_kernel(x_ref, m_ref, o_ref):
    ct = x_ref.shape[-1]
    x = x_ref[0].reshape(-1, ct).astype(BF16)
    f = jnp.dot(m_ref[...], x, preferred_element_type=F32)
    o_ref[...] = f.reshape(o_ref.shape)


def _hy_outer_fwd(x, which, mat8):
    _, groups, r, n2, c = x.shape
    n1 = mat8.shape[0] // (2 * HY_S)
    ct = min(c, HY_OT)
    return pl.pallas_call(
        _hy_outer_fwd_kernel,
        grid=(n2 // HY_S, c // ct),
        in_specs=[pl.BlockSpec((1, groups, r, HY_S, ct), lambda q, j: (which, 0, 0, q, j)),
                  pl.BlockSpec(mat8.shape, lambda q, j: (0, 0))],
        out_specs=pl.BlockSpec((2, n1, HY_S, ct), lambda q, j: (0, 0, q, j)),
        out_shape=jax.ShapeDtypeStruct((2, n1, n2, c), F32),
        compiler_params=_cparams(("arbitrary", "arbitrary")),
        name="hy_outer_fwd",
    )(x, mat8)


def _hy_inner_kernel(*refs, conv, scale):
    if conv:
        a_ref, g_ref, s_ref, o_ref = refs
    else:
        a_ref, g_ref, o_ref = refs
    parts = [a_ref[p, 0] for p in range(a_ref.shape[0])]
    a = (jnp.concatenate(parts, axis=0) if len(parts) > 1 else parts[0]).astype(BF16)
    g = g_ref[0]
    f = jnp.dot(g, a, preferred_element_type=F32)
    ko = f.shape[0] // 2
    fr, fi = f[:ko], f[ko:]
    if not conv:
        o_ref[0, 0] = fr * scale
        o_ref[1, 0] = fi * scale
    else:
        sr, si = s_ref[0, 0], s_ref[1, 0]
        p = jnp.concatenate([fr * sr - fi * si, fr * si + fi * sr], axis=0).astype(BF16)
        d = lax.dot_general(g, p, (((0,), (0,)), ((), ())), preferred_element_type=F32)
        kin = d.shape[0] // 2
        o_ref[0, 0] = d[:kin].astype(o_ref.dtype)
        o_ref[1, 0] = d[kin:].astype(o_ref.dtype)


def _hy_inner(a, gmat, spec, out_dtype, scale=1.0):
    p, n1, kin, c = a.shape
    ko = gmat.shape[1] // 2
    conv = spec is not None
    ct = min(c, HY_CT)
    in_specs = [pl.BlockSpec((p, 1, kin, ct), lambda k, j: (0, k, 0, j)),
                pl.BlockSpec((1,) + gmat.shape[1:], lambda k, j: (k, 0, 0))]
    args = [a, gmat]
    if conv:
        in_specs.append(pl.BlockSpec((2, 1, ko, ct), lambda k, j: (0, k, 0, j)))
        args.append(spec)
    rows = kin if conv else ko
    return pl.pallas_call(
        functools.partial(_hy_inner_kernel, conv=conv, scale=scale),
        grid=(n1, c // ct),
        in_specs=in_specs,
        out_specs=pl.BlockSpec((2, 1, rows, ct), lambda k, j: (0, k, 0, j)),
        out_shape=jax.ShapeDtypeStruct((2, n1, rows, c), out_dtype),
        compiler_params=_cparams(("arbitrary", "arbitrary")),
        name="hy_inner_conv" if conv else "hy_inner_spec",
    )(*args)


def _hy_outer_inv_kernel(d_ref, m_ref, v_ref, x_ref, sk_ref, o_ref):
    ct = d_ref.shape[-1]
    d = d_ref[...].reshape(-1, ct).astype(BF16)
    y = lax.dot_general(m_ref[...], d, (((0,), (0,)), ((), ())), preferred_element_type=F32)
    y = y.reshape(o_ref.shape[1:])
    o_ref[0] = x_ref[0] * (y + v_ref[0] * sk_ref[...])


def _hy_outer_inv(d, mat8, src, si, gate, gi, skip_row):
    _, n1, n2, c = d.shape
    half = n1 // 2
    ct = min(c, HY_OT)
    blk = lambda which: pl.BlockSpec((1, 2, half, HY_S, ct), lambda q, j: (which, 0, 0, q, j))
    return pl.pallas_call(
        _hy_outer_inv_kernel,
        grid=(n2 // HY_S, c // ct),
        in_specs=[pl.BlockSpec((2, n1, HY_S, ct), lambda q, j: (0, 0, q, j)),
                  pl.BlockSpec(mat8.shape, lambda q, j: (0, 0)),
                  blk(si), blk(gi),
                  pl.BlockSpec((1, ct), lambda q, j: (0, j))],
        out_specs=blk(0),
        out_shape=jax.ShapeDtypeStruct((1, 2, half, n2, c), F32),
        compiler_params=_cparams(("arbitrary", "arbitrary")),
        name="hy_outer_inv",
    )(d, mat8, src, gate, skip_row)


def _hyena_spectrum(g, mats):
    n, c = g[0].shape
    n1 = mats["n1"]
    n2 = n // n1
    out = []
    for go in g:
        if n1 > 1:
            a = _hy_outer_fwd(go.reshape(1, 1, n1, n2, c), 0, mats["outer_real"])
            out.append(_hy_inner(a, mats["inner"], None, F32, 1.0 / n))
        else:
            out.append(_hy_inner(go.reshape(1, 1, n, c), mats["inner_real"], None, F32, 1.0 / n))
    return out


def _hy_gate_kernel(y_ref, v_ref, x_ref, sk_ref, o_ref):
    o_ref[...] = x_ref[...] * (y_ref[...] + v_ref[...] * sk_ref[...])


def _hy_gate(y, v, x, skip_row):
    b, L, c = y.shape
    tok = pl.BlockSpec((1, L, c), lambda bi: (bi, 0, 0))
    return pl.pallas_call(
        _hy_gate_kernel,
        grid=(b,),
        in_specs=[tok, tok, tok, pl.BlockSpec((1, c), lambda bi: (0, 0))],
        out_specs=tok,
        out_shape=jax.ShapeDtypeStruct((b, L, c), F32),
        compiler_params=_cparams(("arbitrary",)),
        name="hy_gate",
    )(y, v, x, skip_row)


def _hyena_mixer(hy, spec, skip, mats):
    _, b, L, c = hy.shape
    assert b == 2, "the two batch entries are packed as one complex signal"
    n1 = mats["n1"]
    if n1 == 1:
        z = hy[0]
        for o in range(HY_ORDER):
            y = _hy_inner(z.reshape(2, 1, L, c), mats["inner"], spec[o], F32).reshape(b, L, c)
            z = _hy_gate(y, z, hy[o + 1], skip[o].reshape(1, c))
        return z
    n2 = 2 * L // n1
    src, si = hy.reshape(3, b, n1 // 2, n2, c), 0
    gate = src
    for o in range(HY_ORDER):
        a = _hy_outer_fwd(src, si, mats["outer"])
        d = _hy_inner(a, mats["inner"], spec[o], F32)
        src, si = _hy_outer_inv(d, mats["outer"], src, si, gate, o + 1, skip[o].reshape(1, c)), 0
    return src.reshape(b, L, c)


def _token_mixing(x, g, scale, shift, ga, lw, n_seg, s0_f, s0_b, hy_p):
    b, L, _ = x.shape
    seg = L // n_seg
    proj = functools.partial(_in_proj, x, g, scale, shift, lw["w_in"], lw["conv_w"], seg)
    y_rw, s_f, s_b = _rwkv_mixer(proj("rw"), lw["rw"], s0_f, s0_b)
    if hy_p is None:
        return None, s_f, s_b
    n = 2 * L
    n1 = n // HY_N2 if n // HY_N2 >= 16 else 1
    mats = _dft_mats(n1, n // n1, L if n1 == 1 else n // n1)
    spec = _hyena_spectrum(_hyena_filter(L, hy_p), mats)
    y_hy = _hyena_mixer(proj("hy"), spec, lw["hy_skip"], mats)
    return _out_proj(y_rw, y_hy, proj("gate"), x, ga, lw["w_out"]), s_f, s_b


def kernel(x, c, ctx, c_ctx, ada_w, ada_b, norm1_g, norm2_g, w_in, conv_w, rw_w0, rw_w2, rw_a0, rw_a2, rw_g2, rw_k_k, rw_k_a, rw_r_k, rw_ln_w, rw_ln_b, hy_w1, hy_b1, hy_w2, hy_b2, hy_w3, hy_freq, hy_skip, w_out, peer_wq, peer_keys, peer_u, peer_v, final_g):
    b, L, d = x.shape
    depth = ada_w.shape[0]
    ctx_len = ctx.shape[1]
    s_zero = jnp.zeros((b, N_GROUPS, GROUP_W, GROUP_W), F32)
    cond = jnp.concatenate([c, c_ctx[None, :]], axis=0)
    cond = jnp.pad(jax.nn.silu(cond), ((0, 8 - (b + 1)), (0, 0)))
    for l in range(depth):
        last = l == depth - 1
        mod = _matmul(cond, ada_w[l], 1024) + ada_b[l]
        mx = [m[:, None, :] for m in jnp.split(mod[:b], 6, axis=-1)]
        mc = [jnp.broadcast_to(m[None, :, :], (b, 1, d)) for m in jnp.split(mod[b:b + 1], 6, axis=-1)]
        lw = {
            "w_in": _pack_cols(w_in[l]).astype(BF16),
            "conv_w": _pack_cols(conv_w[l]),
            "rw": _rwkv_params(l, rw_w0, rw_w2, rw_a0, rw_a2, rw_g2, rw_k_k, rw_k_a, rw_r_k, rw_ln_w, rw_ln_b),
            "hy_skip": hy_skip[l],
            "w_out": w_out[l].astype(BF16),
        }
        hy_p = (hy_w1[l], hy_b1[l], hy_w2[l], hy_b2[l], hy_w3[l], hy_freq[l])
        pw = {
            "wqt": peer_wq[l].T.astype(BF16),
            "keys": peer_keys[l].astype(BF16),
            "u": peer_u[l].astype(BF16),
            "vt": peer_v[l].T.astype(BF16),
        }
        g1 = norm1_g[l].reshape(1, d)
        g2 = norm2_g[l].reshape(1, d)

        ctx_new, s_f, s_b = _token_mixing(ctx, g1, mc[1], mc[0], mc[2], lw, 1, s_zero, s_zero,
                                          None if last else hy_p)
        if not last:
            ctx = _peer_block(ctx_new, g2, mc[4], mc[3], mc[5], pw)

        x, _, _ = _token_mixing(x, g1, mx[1], mx[0], mx[2], lw, L // GRID_W, s_f, s_b, hy_p)
        x = _peer_block(x, g2, mx[4], mx[3], mx[5], pw)
    return _final_norm(x, final_g)
```

```python
import functools
import math

import numpy as np
import jax
import jax.numpy as jnp
from jax import lax
from jax.experimental import pallas as pl
from jax.experimental.pallas import tpu as pltpu

F32 = jnp.float32
BF16 = jnp.bfloat16

D_MODEL = 2048
GRID_W = 64
NORM_EPS = 1e-6
RW_N = 64
RW_H = D_MODEL // RW_N
RW_D = D_MODEL
W_LORA = 96
A_LORA = 96
G_LORA = 64
RW_GN_EPS = 64e-5
HY_D = D_MODEL
HY_ORDER = 2
HY_EMB = 33
HY_BANDS = (HY_EMB - 1) // 2
HY_HIDDEN = 64
HY_TARGET = 1e-2
HY_FAST_PCT = 0.3
HY_SLOW_PCT = 1.5
P_HEADS = 8
N_KEYS = 128
N_EXPERTS = N_KEYS * N_KEYS
P_TOPK = 16
D_KEY = 256

LANES = 128
MXU_DIM = 256
VMEM_LIMIT = 48 * 1024 * 1024

COL_K, COL_V, COL_R = 0, RW_D, 2 * RW_D
COL_SMALL = 3 * RW_D
SMALL_W = 1024
COL_GATE = COL_SMALL + SMALL_W
COL_HY = COL_GATE + 2 * D_MODEL
N_PACKED = COL_HY + 3 * HY_D
PROJ_TN = 512
PROJ_RC = 256

CHUNK = 64
HEADS_PER_GROUP = MXU_DIM // RW_N
GROUP_W = HEADS_PER_GROUP * RW_N
N_GROUPS = RW_D // GROUP_W
STACK = HEADS_PER_GROUP * CHUNK
GROUPS_PER_STEP = 8
STEP_W = GROUPS_PER_STEP * GROUP_W
RWKV_TB = 128


def _cparams(sem):
    return pltpu.CompilerParams(dimension_semantics=sem, vmem_limit_bytes=VMEM_LIMIT)


def _rwkv_consts(reverse):
    t = np.arange(CHUNK)
    tt, ss = t[:, None], t[None, :]
    strict = (tt < ss) if reverse else (tt > ss)
    eye = tt == ss
    levels = []
    m = 1
    while m < CHUNK:
        levels.append(strict & ((tt // (2 * m)) == (ss // (2 * m))) & ((tt // m) != (ss // m)))
        m *= 2
    lanes = lambda a: np.tile(a, (1,) * (a.ndim - 1) + (HEADS_PER_GROUP,))
    hb = np.arange(STACK) // CHUNK
    same = hb[:, None] == hb[None, :]
    tri = (tt <= ss) if reverse else (tt >= ss)
    f = lambda a: jnp.asarray(a.astype(np.float32))
    return (f(lanes(strict)), f(lanes(strict | eye)), f(same), f(lanes(eye)),
            f(lanes(np.stack(levels))), f(tri))


def _rwkv_kernel(k_ref, v_ref, r_ref, wd_ref, ad_ref, gd_ref, yf_ref,
                 w0_ref, w2_ref, a0_ref, a2_ref, g2_ref, kk_ref, ka_ref, rk_ref, lnw_ref, lnb_ref,
                 ms_ref, mi_ref, same_ref, eye_ref, lev_ref, tri_ref, s0_ref,
                 y_ref, sout_ref,
                 S_ref, lw_s, kt_s, kn_s, b_s, yacc_s, *, reverse, final, nchunk):
    step = pl.program_id(2)

    @pl.when(step == 0)
    def _():
        S_ref[...] = s0_ref[0]

    same = same_ref[...]
    gsl = [slice(g * GROUP_W, (g + 1) * GROUP_W) for g in range(GROUPS_PER_STEP)]

    same_b = same.astype(BF16)

    def split(x, pieces):
        out = []
        for _ in range(pieces):
            p = x.astype(BF16)
            out.append(p)
            x = x - p.astype(F32)
        return out

    def head_sum(x):
        cols = []
        for s in gsl:
            hi, lo = split(x[:, s], 2)
            cols.append(jnp.dot(hi, same_b, preferred_element_type=F32)
                        + jnp.dot(lo, same_b, preferred_element_type=F32))
        return jnp.concatenate(cols, axis=1)

    k = k_ref[0]
    wraw = w0_ref[...] + jnp.dot(jnp.tanh(wd_ref[0]).astype(BF16), w2_ref[...],
                                 preferred_element_type=F32)
    lw_s[...] = -jax.nn.sigmoid(wraw) * math.exp(-0.5)
    a = jax.nn.sigmoid(a0_ref[...] + jnp.dot(ad_ref[0].astype(BF16), a2_ref[...],
                                             preferred_element_type=F32))
    kt_s[...] = k * (1.0 + (a - 1.0) * ka_ref[...])
    kn = k * kk_ref[...]
    kn = kn * lax.rsqrt(jnp.maximum(head_sum(kn * kn), 1e-24))
    kn_s[...] = kn
    b_s[...] = kn * a

    ms = ms_ref[...]
    mi = mi_ref[...]
    tri = tri_ref[...].astype(BF16)
    mid = CHUNK // 2 if reverse else CHUNK // 2 - 1
    last = 0 if reverse else CHUNK - 1

    def bd(x):
        return jnp.concatenate([x.astype(BF16)] * HEADS_PER_GROUP, axis=0) * same_b

    def mm(a, w):
        return jnp.dot(a.astype(BF16), w, preferred_element_type=F32)

    nt = (((1,), (1,)), ((), ()))
    tn = (((0,), (0,)), ((), ()))

    def chunk_group(rows, g):
        cols = gsl[g]
        lw = lw_s[rows, cols]
        kt = kt_s[rows, cols]
        kn = kn_s[rows, cols]
        b = b_s[rows, cols]
        r = r_ref[0, rows, cols]
        v = v_ref[0, rows, cols]
        c = sum(jnp.dot(tri, p, preferred_element_type=F32) for p in split(lw, 3))
        yield
        cp = c - lw
        rho = c[mid:mid + 1, :]
        cend = c[last:last + 1, :]
        einv = jnp.exp(rho - c)
        eend = jnp.exp(cend - c)
        lhs = jnp.concatenate([kn * jnp.exp(cp - rho), r * jnp.exp(c - rho)], axis=0).astype(BF16)
        rhs = jnp.concatenate([bd(kt * einv), bd(b * einv)], axis=0)
        ag = lax.dot_general(lhs, rhs, nt, preferred_element_type=F32)
        yield
        ak = ag[:CHUNK, :STACK] * ms
        ab = ag[:CHUNK, STACK:] * ms
        gk = ag[CHUNK:, :STACK] * mi
        gb = ag[CHUNK:, STACK:] * mi
        tm = eye_ref[...] - ab * lev_ref[0]
        for lv in range(1, lev_ref.shape[0]):
            te = mm(tm, bd(ab * lev_ref[lv]))
            yield
            tm = tm - mm(te, bd(tm))
            yield
        S = S_ref[g]
        lhs_s = jnp.concatenate([kn * jnp.exp(cp), r * jnp.exp(c)], axis=0).astype(BF16)
        qs = lax.dot_general(lhs_s, S.astype(BF16), nt, preferred_element_type=F32)
        vs = bd(v)
        rhs_u = qs[:CHUNK] + mm(ak, vs)
        yield
        us = mm(tm, bd(rhs_u))
        yield
        y = qs[CHUNK:] + mm(jnp.concatenate([gk, gb], axis=1), jnp.concatenate([vs, bd(-us)], axis=0))
        yield
        if final:
            yacc_s[rows, cols] = y
        else:
            y_ref[0, rows, cols] = y
        vu = jnp.concatenate([v, -us], axis=0).astype(BF16)
        kb = jnp.concatenate([kt * eend, b * eend], axis=0).astype(BF16)
        ds = lax.dot_general(vu, kb, tn, preferred_element_type=F32)
        S_ref[g] = S * jnp.exp(cend) + ds * same

    def chunk(ci, carry):
        c_idx = (nchunk - 1 - ci) if reverse else ci
        rows = pl.ds(pl.multiple_of(c_idx * CHUNK, CHUNK), CHUNK)
        pending = [chunk_group(rows, g) for g in range(GROUPS_PER_STEP)]
        while pending:
            pending = [gen for gen in pending if next(gen, True) is None]
        return carry

    lax.fori_loop(0, nchunk, chunk, 0)
    sout_ref[0] = S_ref[...]

    if final:
        y = yf_ref[0] + yacc_s[...]
        inv_n = 1.0 / RW_N
        mu = head_sum(y) * inv_n
        d = y - mu
        var = head_sum(d * d) * inv_n
        yn = d * lax.rsqrt(var + RW_GN_EPS) * lnw_ref[...] + lnb_ref[...]
        bonus = head_sum(r_ref[0] * k * rk_ref[...]) * v_ref[0]
        g = jnp.dot(jax.nn.sigmoid(gd_ref[0]).astype(BF16), g2_ref[...], preferred_element_type=F32)
        y_ref[0] = (yn + bonus) * g


def _pad_rows(w, rows):
    return jnp.pad(w, ((0, rows - w.shape[0]), (0, 0)))


def _rwkv_direction(mix, yf, prm, s0, d, *, final):
    b, L, _ = mix.shape
    reverse = d == 1
    tb = min(L, RWKV_TB)
    nb = L // tb
    nchunk = tb // CHUNK
    ms, mi, same, eye, lev, tri = _rwkv_consts(reverse)
    blk = (lambda i: nb - 1 - i) if reverse else (lambda i: i)
    sm = COL_SMALL // LANES

    def col(c0):
        return pl.BlockSpec((1, tb, STEP_W), lambda bi, g, i: (bi, blk(i), c0 // STEP_W + g))

    def small(j):
        return pl.BlockSpec((1, tb, LANES), lambda bi, g, i: (bi, blk(i), sm + j))

    vec = pl.BlockSpec((1, STEP_W), lambda bi, g, i: (0, g))
    lora = pl.BlockSpec((LANES, STEP_W), lambda bi, g, i: (0, g))
    const2 = lambda shp: pl.BlockSpec(shp, lambda bi, g, i: (0,) * len(shp))
    state = pl.BlockSpec((1, GROUPS_PER_STEP, GROUP_W, GROUP_W), lambda bi, g, i: (bi, g, 0, 0))
    yspec = pl.BlockSpec((1, tb, STEP_W), lambda bi, g, i: (bi, blk(i), g))

    kern = functools.partial(_rwkv_kernel, reverse=reverse, final=final, nchunk=nchunk)
    if yf is None:
        yf, yf_spec = mix, col(COL_K)
    else:
        yf_spec = yspec
    y, s_fin = pl.pallas_call(
        kern,
        grid=(b, N_GROUPS // GROUPS_PER_STEP, nb),
        in_specs=[col(COL_K), col(COL_V), col(COL_R), small(d), small(2 + d), small(4), yf_spec,
                  vec, lora, vec, lora, lora, vec, vec, vec, vec, vec,
                  const2((CHUNK, STACK)), const2((CHUNK, STACK)), const2((STACK, STACK)),
                  const2((CHUNK, STACK)), const2(tuple(lev.shape)), const2((CHUNK, CHUNK)), state],
        out_specs=[yspec, state],
        out_shape=[jax.ShapeDtypeStruct((b, L, RW_D), F32),
                   jax.ShapeDtypeStruct((b, N_GROUPS, GROUP_W, GROUP_W), F32)],
        scratch_shapes=[pltpu.VMEM((GROUPS_PER_STEP, GROUP_W, GROUP_W), F32)]
        + [pltpu.VMEM((tb, STEP_W), F32)] * 5,
        compiler_params=_cparams(("arbitrary", "arbitrary", "arbitrary")),
        name="rwkv_bwd" if reverse else "rwkv_fwd",
    )(mix, mix, mix, mix, mix, mix, yf,
      prm["w0"][d], prm["w2"][d], prm["a0"][d], prm["a2"][d], prm["g2"], prm["k_k"], prm["k_a"],
      prm["r_k"], prm["ln_w"], prm["ln_b"],
      ms, mi, same, eye, lev, tri, s0)
    return y, s_fin


def _rwkv_params(l, rw_w0, rw_w2, rw_a0, rw_a2, rw_g2, rw_k_k, rw_k_a, rw_r_k, rw_ln_w, rw_ln_b):
    row = lambda v: v.reshape(1, RW_D)
    return {
        "w0": [row(rw_w0[l, d]) for d in range(2)],
        "w2": [_pad_rows(rw_w2[l, d], LANES).astype(BF16) for d in range(2)],
        "a0": [row(rw_a0[l, d]) for d in range(2)],
        "a2": [_pad_rows(rw_a2[l, d], LANES).astype(BF16) for d in range(2)],
        "g2": _pad_rows(rw_g2[l], LANES).astype(BF16),
        "k_k": row(rw_k_k[l]), "k_a": row(rw_k_a[l]), "r_k": row(rw_r_k[l]),
        "ln_w": row(rw_ln_w[l]), "ln_b": row(rw_ln_b[l]),
    }


def _rwkv_mixer(mix, prm, s0_f, s0_b):
    b, L, _ = mix.shape
    y_f, s_f = _rwkv_direction(mix, None, prm, s0_f, 0, final=False)
    y, s_b = _rwkv_direction(mix, y_f, prm, s0_b, 1, final=True)
    return y, s_f, s_b


def _matmul_kernel(a_ref, b_ref, o_ref):
    o_ref[...] = jnp.dot(a_ref[...].astype(BF16), b_ref[...].astype(BF16), preferred_element_type=F32)


def _matmul(a, b, tn):
    m, k = a.shape
    n = b.shape[1]
    return pl.pallas_call(
        _matmul_kernel,
        grid=(n // tn,),
        in_specs=[pl.BlockSpec((m, k), lambda j: (0, 0)), pl.BlockSpec((k, tn), lambda j: (0, j))],
        out_specs=pl.BlockSpec((m, tn), lambda j: (0, j)),
        out_shape=jax.ShapeDtypeStruct((m, n), F32),
        compiler_params=_cparams(("arbitrary",)),
        name="matmul",
    )(a, b)


def _modulate(x, g, scale, shift):
    y = x * lax.rsqrt(jnp.mean(x * x, axis=-1, keepdims=True) + NORM_EPS)
    return y * g * (1.0 + scale) + shift


def _pack_cols(m):
    z = lambda n: jnp.zeros(m.shape[:-1] + (n,), m.dtype)
    s = lambda a, b: m[..., a:b]
    o = 2 * RW_D
    lo = [s(o + j * W_LORA, o + (j + 1) * W_LORA) for j in range(4)]
    r0 = o + 2 * W_LORA + 2 * A_LORA
    g0 = r0 + RW_D
    h0 = g0 + G_LORA
    parts = [s(0, RW_D), s(RW_D, 2 * RW_D), s(r0, r0 + RW_D)]
    for p in lo:
        parts += [p, z(LANES - W_LORA)]
    parts += [s(g0, g0 + G_LORA), z(LANES - G_LORA), z(SMALL_W - 5 * LANES)]
    if m.shape[-1] > h0 + 3 * HY_D:
        parts.append(s(h0 + 3 * HY_D, h0 + 3 * HY_D + 2 * D_MODEL))
    else:
        parts.append(z(2 * D_MODEL))
    parts.append(s(h0, h0 + 3 * HY_D))
    return jnp.concatenate(parts, axis=-1)


def _inproj_kernel(x_ref, g_ref, sc_ref, sh_ref, w_ref, cw_ref, o_ref, h_s, *, seg, gate):
    j = pl.program_id(2)

    @pl.when(j == 0)
    def _():
        h_s[...] = _modulate(x_ref[0], g_ref[...], sc_ref[0], sh_ref[0]).astype(BF16)

    out_blk = (0,) * (len(o_ref.shape) - 2)
    tm = h_s.shape[0]
    rc = min(tm, max(seg, PROJ_RC))
    for r0 in range(0, tm, rc):
        p = jnp.dot(h_s[r0:r0 + rc, :], w_ref[...], preferred_element_type=F32)
        if gate:
            res = jax.nn.sigmoid(p)
        else:
            row = lax.broadcasted_iota(jnp.int32, p.shape, 0) % seg
            prev = jnp.where(row == 0, 0.0, pltpu.roll(p, 1, 0))
            nxt = jnp.where(row == seg - 1, 0.0, pltpu.roll(p, rc - 1, 0))
            res = cw_ref[0:1, :] * prev + cw_ref[1:2, :] * p + cw_ref[2:3, :] * nxt
        o_ref[out_blk + (slice(r0, r0 + rc), slice(None))] = res


def _in_proj(x, g, scale, shift, w_packed, cw_packed, seg, part):
    b, L, d = x.shape
    tm = min(L, 1024)
    c0, c1 = {"rw": (0, COL_GATE), "gate": (COL_GATE, COL_HY), "hy": (COL_HY, N_PACKED)}[part]
    t0, nt = c0 // PROJ_TN, (c1 - c0) // PROJ_TN
    if part == "hy":
        per = HY_D // PROJ_TN
        out_spec = pl.BlockSpec((1, 1, tm, PROJ_TN), lambda bi, i, j: (j // per, bi, i, j % per))
        out_shape = jax.ShapeDtypeStruct((3, b, L, HY_D), F32)
    else:
        out_spec = pl.BlockSpec((1, tm, PROJ_TN), lambda bi, i, j: (bi, i, j))
        out_shape = jax.ShapeDtypeStruct((b, L, c1 - c0), F32)
    return pl.pallas_call(
        functools.partial(_inproj_kernel, seg=seg, gate=part == "gate"),
        grid=(b, L // tm, nt),
        in_specs=[pl.BlockSpec((1, tm, d), lambda bi, i, j: (bi, i, 0)),
                  pl.BlockSpec((1, d), lambda bi, i, j: (0, 0)),
                  pl.BlockSpec((1, 1, d), lambda bi, i, j: (bi, 0, 0)),
                  pl.BlockSpec((1, 1, d), lambda bi, i, j: (bi, 0, 0)),
                  pl.BlockSpec((d, PROJ_TN), lambda bi, i, j: (0, t0 + j)),
                  pl.BlockSpec((3, PROJ_TN), lambda bi, i, j: (0, t0 + j))],
        out_specs=out_spec,
        out_shape=out_shape,
        scratch_shapes=[pltpu.VMEM((tm, d), BF16)],
        compiler_params=_cparams(("arbitrary", "arbitrary", "arbitrary")),
        name="in_proj_" + part,
    )(x, g, scale, shift, w_packed, cw_packed)


def _outproj_kernel(yrw_ref, yhy_ref, grw_ref, ghy_ref, x_ref, ga_ref, w_ref, o_ref):
    y = grw_ref[0] * yrw_ref[0] + ghy_ref[0] * yhy_ref[0]
    o_ref[0] = x_ref[0] + ga_ref[0] * jnp.dot(y.astype(BF16), w_ref[...], preferred_element_type=F32)


def _out_proj(y_rw, y_hy, gates, x, ga, w_out_bf16):
    b, L, d = x.shape
    tm = 256
    tok = pl.BlockSpec((1, tm, d), lambda bi, i: (bi, i, 0))
    return pl.pallas_call(
        _outproj_kernel,
        grid=(b, L // tm),
        in_specs=[tok, tok,
                  tok,
                  pl.BlockSpec((1, tm, d), lambda bi, i: (bi, i, 1)),
                  tok,
                  pl.BlockSpec((1, 1, d), lambda bi, i: (bi, 0, 0)),
                  pl.BlockSpec((d, d), lambda bi, i: (0, 0))],
        out_specs=tok,
        out_shape=jax.ShapeDtypeStruct((b, L, d), F32),
        compiler_params=_cparams(("arbitrary", "arbitrary")),
        name="out_proj",
    )(y_rw, y_hy, gates, gates, x, ga, w_out_bf16)


PEER_TM = 512
PEER_TE = 512
PEER_SUB = 64
PEER_PAIR = 2
STAT_ROWS = 8


def _top_rows(s, n):
    rows = lax.broadcasted_iota(jnp.int32, (n, s.shape[1]), 0)
    out = jnp.zeros((n, s.shape[1]), F32)
    cur = s
    for i in range(n):
        m = jnp.max(cur, axis=0, keepdims=True)
        out = jnp.where(rows == i, m, out)
        cur = jnp.where(cur == m, -jnp.inf, cur)
    return out


def _route_kernel(x_ref, g_ref, sc_ref, sh_ref, wqt_ref, keys_ref, h_ref, s1_ref, c1_ref, s2_ref, st_ref):
    hb = _modulate(x_ref[0], g_ref[...], sc_ref[0], sh_ref[0]).astype(BF16)
    h_ref[0] = hb
    nt = (((1,), (1,)), ((), ()))
    qt = lax.dot_general(wqt_ref[...], hb, nt, preferred_element_type=F32)
    half = D_KEY // 2
    for hd in range(P_HEADS):
        sc = []
        for p in range(2):
            r0 = (hd * 2 + p) * half
            sc.append(jnp.dot(keys_ref[hd, p], qt[r0:r0 + half, :].astype(BF16),
                              preferred_element_type=F32))
        s1_ref[0, hd] = sc[0]
        s2_ref[0, hd] = sc[1]
        t1 = _top_rows(sc[0], P_TOPK)
        t2 = _top_rows(sc[1], P_TOPK)
        r8 = lax.broadcasted_iota(jnp.int32, (8, t1.shape[1]), 0)
        tiles = [t1[0:1] + t2[0:8], t1[0:1] + t2[8:16], t1[1:2] + t2[0:8]]
        for i in range(2, 8):
            tiles.append(jnp.where(r8 < P_TOPK // (i + 1), t1[i:i + 1] + t2[0:8], -jnp.inf))
        tiles.append(t1[8:16] + t2[0:1])
        best = _top_rows(jnp.concatenate(tiles, axis=0), P_TOPK)
        z = jnp.sum(jnp.exp(best - best[0:1]), axis=0, keepdims=True)
        c1_ref[0, hd] = jnp.exp(sc[0] - t1[0:1]) * (1.0 / z)
        st = jnp.where(r8 == 0, best[P_TOPK - 1:P_TOPK], 0.0)
        st_ref[0, hd] = jnp.where(r8 == 1, t2[0:1], st)


def _peer_route(x, g, scale, shift, wqt_bf16, keys_bf16):
    b, L, d = x.shape
    tm = min(L, PEER_TM)
    tok = lambda: pl.BlockSpec((1, tm, d), lambda bi, i: (bi, i, 0))
    rowv = pl.BlockSpec((1, 1, d), lambda bi, i: (bi, 0, 0))
    sspec = pl.BlockSpec((1, P_HEADS, N_KEYS, tm), lambda bi, i: (bi, 0, 0, i))
    return pl.pallas_call(
        _route_kernel,
        grid=(b, L // tm),
        in_specs=[tok(), pl.BlockSpec((1, d), lambda bi, i: (0, 0)), rowv, rowv,
                  pl.BlockSpec(wqt_bf16.shape, lambda bi, i: (0, 0)),
                  pl.BlockSpec(keys_bf16.shape, lambda bi, i: (0, 0, 0, 0))],
        out_specs=[tok(), sspec, sspec, sspec,
                   pl.BlockSpec((1, P_HEADS, STAT_ROWS, tm), lambda bi, i: (bi, 0, 0, i))],
        out_shape=[jax.ShapeDtypeStruct((b, L, d), BF16),
                   jax.ShapeDtypeStruct((b, P_HEADS, N_KEYS, L), F32),
                   jax.ShapeDtypeStruct((b, P_HEADS, N_KEYS, L), F32),
                   jax.ShapeDtypeStruct((b, P_HEADS, N_KEYS, L), F32),
                   jax.ShapeDtypeStruct((b, P_HEADS, STAT_ROWS, L), F32)],
        compiler_params=_cparams(("arbitrary", "arbitrary")),
        name="peer_route",
    )(x, g, scale, shift, wqt_bf16, keys_bf16)


def _peer_dense_kernel(h_ref, u_ref, vt_ref, s1_ref, c1_ref, s2_ref, st_ref, x_ref, ga_ref, o_ref,
                       acc_s, e2_s, ge_s, wa_s, wb_s, row_s):
    e = pl.program_id(2)
    last = pl.num_programs(2) - 1
    tm = h_ref.shape[1]
    n1 = PEER_TE // N_KEYS

    @pl.when(e == 0)
    def _():
        acc_s[...] = jnp.zeros_like(acc_s)
        wb_s[...] = jnp.zeros_like(wb_s)
        for hd in range(P_HEADS):
            e2_s[hd] = jnp.exp(s2_ref[0, hd] - st_ref[0, hd, 1:2, :])

    def step(w_prev, w_cur):
        acc_s[...] += jnp.dot(vt_ref[0], w_prev[...], preferred_element_type=F32)
        nt = (((1,), (1,)), ((), ()))
        act = lax.dot_general(u_ref[...], h_ref[0], nt, preferred_element_type=F32)
        ge_s[...] = 0.5 * act * (1.0 + lax.erf(act * (1.0 / math.sqrt(2.0))))
        tile = jnp.minimum(e, last - 1)
        for i in range(n1):
            for hd in range(P_HEADS):
                row_s[0, i, hd] = s1_ref[0, hd, pl.ds(tile * n1 + i, 1), :]
                row_s[1, i, hd] = c1_ref[0, hd, pl.ds(tile * n1 + i, 1), :]
        for ip in range(0, n1, PEER_PAIR):
            pair = range(ip, ip + PEER_PAIR)
            for strip in range(tm // LANES):
                cs = slice(strip * LANES, (strip + 1) * LANES)
                for sub in range(N_KEYS // PEER_SUB):
                    ks = slice(sub * PEER_SUB, (sub + 1) * PEER_SUB)
                    gsum = [None] * PEER_PAIR
                    for hd in range(P_HEADS):
                        s2t = s2_ref[0, hd, ks, cs]
                        e2t = e2_s[hd, ks, cs]
                        thr = st_ref[0, hd, 0:1, cs]
                        for q, i in enumerate(pair):
                            term = jnp.where((s2t + row_s[0, i, hd, :, cs]) >= thr,
                                             e2t * row_s[1, i, hd, :, cs], 0.0)
                            gsum[q] = term if gsum[q] is None else gsum[q] + term
                    for q, i in enumerate(pair):
                        ws = slice(i * N_KEYS + sub * PEER_SUB, i * N_KEYS + (sub + 1) * PEER_SUB)
                        w_cur[ws, cs] = (gsum[q] * ge_s[ws, cs]).astype(BF16)

    @pl.when(e % 2 == 0)
    def _():
        step(wb_s, wa_s)

    @pl.when(e % 2 == 1)
    def _():
        step(wa_s, wb_s)

    @pl.when(e == last)
    def _():
        o_ref[0] = x_ref[0] + ga_ref[0] * acc_s[...].T


def _peer_dense(h2, u_bf16, vt_bf16, s1, c1, s2, st, x, ga):
    b, L, d = x.shape
    tm = min(L, PEER_TM)
    tok = lambda: pl.BlockSpec((1, tm, d), lambda bi, i, e: (bi, i, 0))
    sspec = lambda: pl.BlockSpec((1, P_HEADS, N_KEYS, tm), lambda bi, i, e: (bi, 0, 0, i))
    ne = N_EXPERTS // PEER_TE
    return pl.pallas_call(
        _peer_dense_kernel,
        grid=(b, L // tm, ne + 1),
        in_specs=[tok(),
                  pl.BlockSpec((PEER_TE, d), lambda bi, i, e: (jnp.minimum(e, ne - 1), 0)),
                  pl.BlockSpec((1, d, PEER_TE), lambda bi, i, e: (jnp.maximum(e - 1, 0), 0, 0)),
                  sspec(), sspec(), sspec(),
                  pl.BlockSpec((1, P_HEADS, STAT_ROWS, tm), lambda bi, i, e: (bi, 0, 0, i)),
                  tok(),
                  pl.BlockSpec((1, 1, d), lambda bi, i, e: (bi, 0, 0))],
        out_specs=tok(),
        out_shape=jax.ShapeDtypeStruct((b, L, d), F32),
        scratch_shapes=[pltpu.VMEM((d, tm), F32), pltpu.VMEM((P_HEADS, N_KEYS, tm), F32),
                        pltpu.VMEM((PEER_TE, tm), F32), pltpu.VMEM((PEER_TE, tm), BF16),
                        pltpu.VMEM((PEER_TE, tm), BF16), pltpu.VMEM((2, PEER_TE // N_KEYS, P_HEADS, 1, tm), F32)],
        compiler_params=pltpu.CompilerParams(
            dimension_semantics=("arbitrary", "arbitrary", "arbitrary"),
            vmem_limit_bytes=56 * 1024 * 1024),
        name="peer_dense",
    )(h2, u_bf16, vt_bf16, s1, c1, s2, st, x, ga)


def _peer_block(x, g, scale, shift, ga, pw):
    h2, s1, c1, s2, st = _peer_route(x, g, scale, shift, pw["wqt"], pw["keys"])
    return _peer_dense(h2, pw["u"], pw["vt"], s1, c1, s2, st, x, ga)


def _final_norm_kernel(x_ref, g_ref, o_ref):
    x = x_ref[0]
    o_ref[0] = x * lax.rsqrt(jnp.mean(x * x, axis=-1, keepdims=True) + NORM_EPS) * g_ref[...]


def _final_norm(x, g):
    b, L, d = x.shape
    tm = 512
    tok = pl.BlockSpec((1, tm, d), lambda bi, i: (bi, i, 0))
    return pl.pallas_call(
        _final_norm_kernel,
        grid=(b, L // tm),
        in_specs=[tok, pl.BlockSpec((1, d), lambda bi, i: (0, 0))],
        out_specs=tok,
        out_shape=jax.ShapeDtypeStruct((b, L, d), F32),
        compiler_params=_cparams(("arbitrary", "arbitrary")),
        name="final_norm",
    )(x, g.reshape(1, d))


HY_N2 = 256
HY_CT = 1024
HY_S = 8
HY_OT = 512


def _hyena_filter(L, hy_p):
    w1, b1, w2, b2, w3, freq = hy_p
    t = jnp.linspace(0.0, 1.0, L, dtype=F32)[:, None]
    bands = jnp.linspace(1e-4, HY_BANDS - 1, HY_BANDS, dtype=F32)[None, :]
    ang = (2.0 * math.pi / L) * jnp.arange(L, dtype=F32)[:, None] * bands
    z = jnp.concatenate([t, jnp.cos(ang), -jnp.sin(ang)], axis=-1)
    hdn = jnp.sin(freq * (z @ w1 + b1))
    hdn = jnp.sin(freq * (hdn @ w2 + b2))
    pos = jnp.arange(L)
    posb = (L - pos) % L
    hdn2 = jnp.stack([hdn, hdn[posb]])
    t2 = jnp.stack([t, t[posb]])
    deltas = jnp.abs(jnp.linspace(math.log(HY_TARGET) / HY_SLOW_PCT, math.log(HY_TARGET) / HY_FAST_PCT,
                                  HY_D, dtype=F32))
    w3r = w3.reshape(HY_HIDDEN, HY_ORDER, 2, HY_D)
    first = (jnp.arange(L) == 0)[None, :, None]
    side = jnp.arange(2)[:, None, None]
    out = []
    for o in range(HY_ORDER):
        h = jnp.einsum("slh,hsc->slc", hdn2, w3r[:, o]) * jnp.exp(-t2 * deltas)
        h = h * lax.rsqrt(jnp.sum(h * h, axis=(0, 1), keepdims=True))
        g = h + jnp.where(first & (side == 0), h[1:2, 0:1, :], 0.0)
        g = jnp.where(first & (side == 1), 0.0, g)
        out.append(g.reshape(2 * L, HY_D))
    return out


def _cis(idx, n):
    ang = (2.0 * math.pi / n) * (idx % n).astype(F32)
    return jnp.cos(ang), -jnp.sin(ang)


def _block_complex(re, im):
    return jnp.concatenate([jnp.concatenate([re, -im], axis=-1),
                            jnp.concatenate([im, re], axis=-1)], axis=-2)


def _dft_mats(n1, n2, kin):
    k = jnp.arange(n1, dtype=jnp.int32)
    c1, s1 = _cis(k[:, None] * k[None, :], n1)
    half = max(n1 // 2, 1)
    eye = jnp.eye(HY_S, dtype=F32)
    outer = jnp.kron(_block_complex(c1[:, :half], s1[:, :half]), eye).astype(BF16)
    outer_real = jnp.kron(jnp.concatenate([c1, s1], axis=0), eye).astype(BF16)
    m2 = jnp.arange(n2, dtype=jnp.int32)
    freq = k[:, None, None] + n1 * m2[None, :, None]
    gr, gi = _cis(m2[None, None, :] * freq, n1 * n2)
    inner = _block_complex(gr[:, :, :kin], gi[:, :, :kin]).astype(BF16)
    inner_real = jnp.concatenate([gr, gi], axis=1).astype(BF16)
    return {"n1": n1, "outer": outer, "outer_real": outer_real, "inner": inner, "inner_real": inner_real}


def _hy_outer_fwd_kernel(x_ref, m_ref, o_ref):
    ct = x_ref.shape[-1]
    x = x_ref[0].reshape(-1, ct).astype(BF16)
    f = jnp.dot(m_ref[...], x, preferred_element_type=F32)
    o_ref[...] = f.reshape(o_ref.shape)


def _hy_outer_fwd(x, which, mat8):
    _, groups, r, n2, c = x.shape
    n1 = mat8.shape[0] // (2 * HY_S)
    ct = min(c, HY_OT)
    return pl.pallas_call(
        _hy_outer_fwd_kernel,
        grid=(n2 // HY_S, c // ct),
        in_specs=[pl.BlockSpec((1, groups, r, HY_S, ct), lambda q, j: (which, 0, 0, q, j)),
                  pl.BlockSpec(mat8.shape, lambda q, j: (0, 0))],
        out_specs=pl.BlockSpec((2, n1, HY_S, ct), lambda q, j: (0, 0, q, j)),
        out_shape=jax.ShapeDtypeStruct((2, n1, n2, c), F32),
        compiler_params=_cparams(("arbitrary", "arbitrary")),
        name="hy_outer_fwd",
    )(x, mat8)


def _hy_inner_kernel(*refs, conv, scale):
    if conv:
        a_ref, g_ref, s_ref, o_ref = refs
    else:
        a_ref, g_ref, o_ref = refs
    parts = [a_ref[p, 0] for p in range(a_ref.shape[0])]
    a = (jnp.concatenate(parts, axis=0) if len(parts) > 1 else parts[0]).astype(BF16)
    g = g_ref[0]
    f = jnp.dot(g, a, preferred_element_type=F32)
    ko = f.shape[0] // 2
    fr, fi = f[:ko], f[ko:]
    if not conv:
        o_ref[0, 0] = fr * scale
        o_ref[1, 0] = fi * scale
    else:
        sr, si = s_ref[0, 0], s_ref[1, 0]
        p = jnp.concatenate([fr * sr - fi * si, fr * si + fi * sr], axis=0).astype(BF16)
        d = lax.dot_general(g, p, (((0,), (0,)), ((), ())), preferred_element_type=F32)
        kin = d.shape[0] // 2
        o_ref[0, 0] = d[:kin].astype(o_ref.dtype)
        o_ref[1, 0] = d[kin:].astype(o_ref.dtype)


def _hy_inner(a, gmat, spec, out_dtype, scale=1.0):
    p, n1, kin, c = a.shape
    ko = gmat.shape[1] // 2
    conv = spec is not None
    ct = min(c, HY_CT)
    in_specs = [pl.BlockSpec((p, 1, kin, ct), lambda k, j: (0, k, 0, j)),
                pl.BlockSpec((1,) + gmat.shape[1:], lambda k, j: (k, 0, 0))]
    args = [a, gmat]
    if conv:
        in_specs.append(pl.BlockSpec((2, 1, ko, ct), lambda k, j: (0, k, 0, j)))
        args.append(spec)
    rows = kin if conv else ko
    return pl.pallas_call(
        functools.partial(_hy_inner_kernel, conv=conv, scale=scale),
        grid=(n1, c // ct),
        in_specs=in_specs,
        out_specs=pl.BlockSpec((2, 1, rows, ct), lambda k, j: (0, k, 0, j)),
        out_shape=jax.ShapeDtypeStruct((2, n1, rows, c), out_dtype),
        compiler_params=_cparams(("arbitrary", "arbitrary")),
        name="hy_inner_conv" if conv else "hy_inner_spec",
    )(*args)


def _hy_outer_inv_kernel(d_ref, m_ref, v_ref, x_ref, sk_ref, o_ref):
    ct = d_ref.shape[-1]
    d = d_ref[...].reshape(-1, ct).astype(BF16)
    y = lax.dot_general(m_ref[...], d, (((0,), (0,)), ((), ())), preferred_element_type=F32)
    y = y.reshape(o_ref.shape[1:])
    o_ref[0] = x_ref[0] * (y + v_ref[0] * sk_ref[...])


def _hy_outer_inv(d, mat8, src, si, gate, gi, skip_row):
    _, n1, n2, c = d.shape
    half = n1 // 2
    ct = min(c, HY_OT)
    blk = lambda which: pl.BlockSpec((1, 2, half, HY_S, ct), lambda q, j: (which, 0, 0, q, j))
    return pl.pallas_call(
        _hy_outer_inv_kernel,
        grid=(n2 // HY_S, c // ct),
        in_specs=[pl.BlockSpec((2, n1, HY_S, ct), lambda q, j: (0, 0, q, j)),
                  pl.BlockSpec(mat8.shape, lambda q, j: (0, 0)),
                  blk(si), blk(gi),
                  pl.BlockSpec((1, ct), lambda q, j: (0, j))],
        out_specs=blk(0),
        out_shape=jax.ShapeDtypeStruct((1, 2, half, n2, c), F32),
        compiler_params=_cparams(("arbitrary", "arbitrary")),
        name="hy_outer_inv",
    )(d, mat8, src, gate, skip_row)


def _hyena_spectrum(g, mats):
    n, c = g[0].shape
    n1 = mats["n1"]
    n2 = n // n1
    out = []
    for go in g:
        if n1 > 1:
            a = _hy_outer_fwd(go.reshape(1, 1, n1, n2, c), 0, mats["outer_real"])
            out.append(_hy_inner(a, mats["inner"], None, F32, 1.0 / n))
        else:
            out.append(_hy_inner(go.reshape(1, 1, n, c), mats["inner_real"], None, F32, 1.0 / n))
    return out


def _hy_gate_kernel(y_ref, v_ref, x_ref, sk_ref, o_ref):
    o_ref[...] = x_ref[...] * (y_ref[...] + v_ref[...] * sk_ref[...])


def _hy_gate(y, v, x, skip_row):
    b, L, c = y.shape
    tok = pl.BlockSpec((1, L, c), lambda bi: (bi, 0, 0))
    return pl.pallas_call(
        _hy_gate_kernel,
        grid=(b,),
        in_specs=[tok, tok, tok, pl.BlockSpec((1, c), lambda bi: (0, 0))],
        out_specs=tok,
        out_shape=jax.ShapeDtypeStruct((b, L, c), F32),
        compiler_params=_cparams(("arbitrary",)),
        name="hy_gate",
    )(y, v, x, skip_row)


def _hyena_mixer(hy, spec, skip, mats):
    _, b, L, c = hy.shape
    assert b == 2, "the two batch entries are packed as one complex signal"
    n1 = mats["n1"]
    if n1 == 1:
        z = hy[0]
        for o in range(HY_ORDER):
            y = _hy_inner(z.reshape(2, 1, L, c), mats["inner"], spec[o], F32).reshape(b, L, c)
            z = _hy_gate(y, z, hy[o + 1], skip[o].reshape(1, c))
        return z
    n2 = 2 * L // n1
    src, si = hy.reshape(3, b, n1 // 2, n2, c), 0
    gate = src
    for o in range(HY_ORDER):
        a = _hy_outer_fwd(src, si, mats["outer"])
        d = _hy_inner(a, mats["inner"], spec[o], F32)
        src, si = _hy_outer_inv(d, mats["outer"], src, si, gate, o + 1, skip[o].reshape(1, c)), 0
    return src.reshape(b, L, c)


def _token_mixing(x, g, scale, shift, ga, lw, n_seg, s0_f, s0_b, hy_p):
    b, L, _ = x.shape
    seg = L // n_seg
    proj = functools.partial(_in_proj, x, g, scale, shift, lw["w_in"], lw["conv_w"], seg)
    y_rw, s_f, s_b = _rwkv_mixer(proj("rw"), lw["rw"], s0_f, s0_b)
    if hy_p is None:
        return None, s_f, s_b
    n = 2 * L
    n1 = n // HY_N2 if n // HY_N2 >= 16 else 1
    mats = _dft_mats(n1, n // n1, L if n1 == 1 else n // n1)
    spec = _hyena_spectrum(_hyena_filter(L, hy_p), mats)
    y_hy = _hyena_mixer(proj("hy"), spec, lw["hy_skip"], mats)
    return _out_proj(y_rw, y_hy, proj("gate"), x, ga, lw["w_out"]), s_f, s_b


def kernel(x, c, ctx, c_ctx, ada_w, ada_b, norm1_g, norm2_g, w_in, conv_w, rw_w0, rw_w2, rw_a0, rw_a2, rw_g2, rw_k_k, rw_k_a, rw_r_k, rw_ln_w, rw_ln_b, hy_w1, hy_b1, hy_w2, hy_b2, hy_w3, hy_freq, hy_skip, w_out, peer_wq, peer_keys, peer_u, peer_v, final_g):
    b, L, d = x.shape
    depth = ada_w.shape[0]
    ctx_len = ctx.shape[1]
    s_zero = jnp.zeros((b, N_GROUPS, GROUP_W, GROUP_W), F32)
    cond = jnp.concatenate([c, c_ctx[None, :]], axis=0)
    cond = jnp.pad(jax.nn.silu(cond), ((0, 8 - (b + 1)), (0, 0)))
    for l in range(depth):
        last = l == depth - 1
        mod = _matmul(cond, ada_w[l], 1024) + ada_b[l]
        mx = [m[:, None, :] for m in jnp.split(mod[:b], 6, axis=-1)]
        mc = [jnp.broadcast_to(m[None, :, :], (b, 1, d)) for m in jnp.split(mod[b:b + 1], 6, axis=-1)]
        lw = {
            "w_in": _pack_cols(w_in[l]).astype(BF16),
            "conv_w": _pack_cols(conv_w[l]),
            "rw": _rwkv_params(l, rw_w0, rw_w2, rw_a0, rw_a2, rw_g2, rw_k_k, rw_k_a, rw_r_k, rw_ln_w, rw_ln_b),
            "hy_skip": hy_skip[l],
            "w_out": w_out[l].astype(BF16),
        }
        hy_p = (hy_w1[l], hy_b1[l], hy_w2[l], hy_b2[l], hy_w3[l], hy_freq[l])
        pw = {
            "wqt": peer_wq[l].T.astype(BF16),
            "keys": peer_keys[l].astype(BF16),
            "u": peer_u[l].astype(BF16),
            "vt": peer_v[l].astype(BF16).reshape(N_EXPERTS // PEER_TE, PEER_TE, d).transpose(0, 2, 1),
        }
        g1 = norm1_g[l].reshape(1, d)
        g2 = norm2_g[l].reshape(1, d)

        ctx_new, s_f, s_b = _token_mixing(ctx, g1, mc[1], mc[0], mc[2], lw, 1, s_zero, s_zero,
                                          None if last else hy_p)
        if not last:
            ctx = _peer_block(ctx_new, g2, mc[4], mc[3], mc[5], pw)

        x, _, _ = _token_mixing(x, g1, mx[1], mx[0], mx[2], lw, L // GRID_W, s_f, s_b, hy_p)
        x = _peer_block(x, g2, mx[4], mx[3], mx[5], pw)
    return _final_norm(x, final_g)
```

```python
import functools
import math

import numpy as np
import jax
import jax.numpy as jnp
from jax import lax
from jax.experimental import pallas as pl
from jax.experimental.pallas import tpu as pltpu

F32 = jnp.float32
BF16 = jnp.bfloat16

D_MODEL = 2048
GRID_W = 64
NORM_EPS = 1e-6
RW_N = 64
RW_H = D_MODEL // RW_N
RW_D = D_MODEL
W_LORA = 96
A_LORA = 96
G_LORA = 64
RW_GN_EPS = 64e-5
HY_D = D_MODEL
HY_ORDER = 2
HY_EMB = 33
HY_BANDS = (HY_EMB - 1) // 2
HY_HIDDEN = 64
HY_TARGET = 1e-2
HY_FAST_PCT = 0.3
HY_SLOW_PCT = 1.5
P_HEADS = 8
N_KEYS = 128
N_EXPERTS = N_KEYS * N_KEYS
P_TOPK = 16
D_KEY = 256

LANES = 128
MXU_DIM = 256
VMEM_LIMIT = 48 * 1024 * 1024

COL_K, COL_V, COL_R = 0, RW_D, 2 * RW_D
COL_SMALL = 3 * RW_D
SMALL_W = 1024
COL_GATE = COL_SMALL + SMALL_W
COL_HY = COL_GATE + 2 * D_MODEL
N_PACKED = COL_HY + 3 * HY_D
PROJ_TN = 512
PROJ_RC = 256

CHUNK = 64
HEADS_PER_GROUP = MXU_DIM // RW_N
GROUP_W = HEADS_PER_GROUP * RW_N
N_GROUPS = RW_D // GROUP_W
STACK = HEADS_PER_GROUP * CHUNK
GROUPS_PER_STEP = 8
STEP_W = GROUPS_PER_STEP * GROUP_W
RWKV_TB = 128


def _cparams(sem):
    return pltpu.CompilerParams(dimension_semantics=sem, vmem_limit_bytes=VMEM_LIMIT)


def _rwkv_consts(reverse):
    t = np.arange(CHUNK)
    tt, ss = t[:, None], t[None, :]
    strict = (tt < ss) if reverse else (tt > ss)
    eye = tt == ss
    levels = []
    m = 1
    while m < CHUNK:
        levels.append(strict & ((tt // (2 * m)) == (ss // (2 * m))) & ((tt // m) != (ss // m)))
        m *= 2
    lanes = lambda a: np.tile(a, (1,) * (a.ndim - 1) + (HEADS_PER_GROUP,))
    hb = np.arange(STACK) // CHUNK
    same = hb[:, None] == hb[None, :]
    tri = (tt <= ss) if reverse else (tt >= ss)
    f = lambda a: jnp.asarray(a.astype(np.float32))
    return (f(lanes(strict)), f(lanes(strict | eye)), f(same), f(lanes(eye)),
            f(lanes(np.stack(levels))), f(tri))


def _rwkv_kernel(k_ref, v_ref, r_ref, wd_ref, ad_ref, gd_ref, yf_ref,
                 w0_ref, w2_ref, a0_ref, a2_ref, g2_ref, kk_ref, ka_ref, rk_ref, lnw_ref, lnb_ref,
                 ms_ref, mi_ref, same_ref, eye_ref, lev_ref, tri_ref, s0_ref,
                 y_ref, sout_ref,
                 S_ref, lw_s, kt_s, kn_s, b_s, yacc_s, *, reverse, final, nchunk):
    step = pl.program_id(2)

    @pl.when(step == 0)
    def _():
        S_ref[...] = s0_ref[0]

    same = same_ref[...]
    gsl = [slice(g * GROUP_W, (g + 1) * GROUP_W) for g in range(GROUPS_PER_STEP)]

    same_b = same.astype(BF16)

    def split(x, pieces):
        out = []
        for _ in range(pieces):
            p = x.astype(BF16)
            out.append(p)
            x = x - p.astype(F32)
        return out

    def head_sum(x):
        cols = []
        for s in gsl:
            hi, lo = split(x[:, s], 2)
            cols.append(jnp.dot(hi, same_b, preferred_element_type=F32)
                        + jnp.dot(lo, same_b, preferred_element_type=F32))
        return jnp.concatenate(cols, axis=1)

    k = k_ref[0]
    wraw = w0_ref[...] + jnp.dot(jnp.tanh(wd_ref[0]).astype(BF16), w2_ref[...],
                                 preferred_element_type=F32)
    lw_s[...] = -jax.nn.sigmoid(wraw) * math.exp(-0.5)
    a = jax.nn.sigmoid(a0_ref[...] + jnp.dot(ad_ref[0].astype(BF16), a2_ref[...],
                                             preferred_element_type=F32))
    kt_s[...] = k * (1.0 + (a - 1.0) * ka_ref[...])
    kn = k * kk_ref[...]
    kn = kn * lax.rsqrt(jnp.maximum(head_sum(kn * kn), 1e-24))
    kn_s[...] = kn
    b_s[...] = kn * a

    ms = ms_ref[...]
    mi = mi_ref[...]
    tri = tri_ref[...].astype(BF16)
    mid = CHUNK // 2 if reverse else CHUNK // 2 - 1
    last = 0 if reverse else CHUNK - 1

    def bd(x):
        return jnp.concatenate([x.astype(BF16)] * HEADS_PER_GROUP, axis=0) * same_b

    def mm(a, w):
        return jnp.dot(a.astype(BF16), w, preferred_element_type=F32)

    nt = (((1,), (1,)), ((), ()))
    tn = (((0,), (0,)), ((), ()))

    def chunk_group(rows, g):
        cols = gsl[g]
        lw = lw_s[rows, cols]
        kt = kt_s[rows, cols]
        kn = kn_s[rows, cols]
        b = b_s[rows, cols]
        r = r_ref[0, rows, cols]
        v = v_ref[0, rows, cols]
        c = sum(jnp.dot(tri, p, preferred_element_type=F32) for p in split(lw, 3))
        yield
        cp = c - lw
        rho = c[mid:mid + 1, :]
        cend = c[last:last + 1, :]
        einv = jnp.exp(rho - c)
        eend = jnp.exp(cend - c)
        lhs = jnp.concatenate([kn * jnp.exp(cp - rho), r * jnp.exp(c - rho)], axis=0).astype(BF16)
        rhs = jnp.concatenate([bd(kt * einv), bd(b * einv)], axis=0)
        ag = lax.dot_general(lhs, rhs, nt, preferred_element_type=F32)
        yield
        ak = ag[:CHUNK, :STACK] * ms
        ab = ag[:CHUNK, STACK:] * ms
        gk = ag[CHUNK:, :STACK] * mi
        gb = ag[CHUNK:, STACK:] * mi
        tm = eye_ref[...] - ab * lev_ref[0]
        for lv in range(1, lev_ref.shape[0]):
            te = mm(tm, bd(ab * lev_ref[lv]))
            yield
            tm = tm - mm(te, bd(tm))
            yield
        S = S_ref[g]
        lhs_s = jnp.concatenate([kn * jnp.exp(cp), r * jnp.exp(c)], axis=0).astype(BF16)
        qs = lax.dot_general(lhs_s, S.astype(BF16), nt, preferred_element_type=F32)
        vs = bd(v)
        rhs_u = qs[:CHUNK] + mm(ak, vs)
        yield
        us = mm(tm, bd(rhs_u))
        yield
        y = qs[CHUNK:] + mm(jnp.concatenate([gk, gb], axis=1), jnp.concatenate([vs, bd(-us)], axis=0))
        yield
        if final:
            yacc_s[rows, cols] = y
        else:
            y_ref[0, rows, cols] = y
        vu = jnp.concatenate([v, -us], axis=0).astype(BF16)
        kb = jnp.concatenate([kt * eend, b * eend], axis=0).astype(BF16)
        ds = lax.dot_general(vu, kb, tn, preferred_element_type=F32)
        S_ref[g] = S * jnp.exp(cend) + ds * same

    def chunk(ci, carry):
        c_idx = (nchunk - 1 - ci) if reverse else ci
        rows = pl.ds(pl.multiple_of(c_idx * CHUNK, CHUNK), CHUNK)
        pending = [chunk_group(rows, g) for g in range(GROUPS_PER_STEP)]
        while pending:
            pending = [gen for gen in pending if next(gen, True) is None]
        return carry

    lax.fori_loop(0, nchunk, chunk, 0)
    sout_ref[0] = S_ref[...]

    if final:
        y = yf_ref[0] + yacc_s[...]
        inv_n = 1.0 / RW_N
        mu = head_sum(y) * inv_n
        d = y - mu
        var = head_sum(d * d) * inv_n
        yn = d * lax.rsqrt(var + RW_GN_EPS) * lnw_ref[...] + lnb_ref[...]
        bonus = head_sum(r_ref[0] * k * rk_ref[...]) * v_ref[0]
        g = jnp.dot(jax.nn.sigmoid(gd_ref[0]).astype(BF16), g2_ref[...], preferred_element_type=F32)
        y_ref[0] = (yn + bonus) * g


def _pad_rows(w, rows):
    return jnp.pad(w, ((0, rows - w.shape[0]), (0, 0)))


def _rwkv_direction(mix, yf, prm, s0, d, *, final):
    b, L, _ = mix.shape
    reverse = d == 1
    tb = min(L, RWKV_TB)
    nb = L // tb
    nchunk = tb // CHUNK
    ms, mi, same, eye, lev, tri = _rwkv_consts(reverse)
    blk = (lambda i: nb - 1 - i) if reverse else (lambda i: i)
    sm = COL_SMALL // LANES

    def col(c0):
        return pl.BlockSpec((1, tb, STEP_W), lambda bi, g, i: (bi, blk(i), c0 // STEP_W + g))

    def small(j):
        return pl.BlockSpec((1, tb, LANES), lambda bi, g, i: (bi, blk(i), sm + j))

    vec = pl.BlockSpec((1, STEP_W), lambda bi, g, i: (0, g))
    lora = pl.BlockSpec((LANES, STEP_W), lambda bi, g, i: (0, g))
    const2 = lambda shp: pl.BlockSpec(shp, lambda bi, g, i: (0,) * len(shp))
    state = pl.BlockSpec((1, GROUPS_PER_STEP, GROUP_W, GROUP_W), lambda bi, g, i: (bi, g, 0, 0))
    yspec = pl.BlockSpec((1, tb, STEP_W), lambda bi, g, i: (bi, blk(i), g))

    kern = functools.partial(_rwkv_kernel, reverse=reverse, final=final, nchunk=nchunk)
    if yf is None:
        yf, yf_spec = mix, col(COL_K)
    else:
        yf_spec = yspec
    y, s_fin = pl.pallas_call(
        kern,
        grid=(b, N_GROUPS // GROUPS_PER_STEP, nb),
        in_specs=[col(COL_K), col(COL_V), col(COL_R), small(d), small(2 + d), small(4), yf_spec,
                  vec, lora, vec, lora, lora, vec, vec, vec, vec, vec,
                  const2((CHUNK, STACK)), const2((CHUNK, STACK)), const2((STACK, STACK)),
                  const2((CHUNK, STACK)), const2(tuple(lev.shape)), const2((CHUNK, CHUNK)), state],
        out_specs=[yspec, state],
        out_shape=[jax.ShapeDtypeStruct((b, L, RW_D), F32),
                   jax.ShapeDtypeStruct((b, N_GROUPS, GROUP_W, GROUP_W), F32)],
        scratch_shapes=[pltpu.VMEM((GROUPS_PER_STEP, GROUP_W, GROUP_W), F32)]
        + [pltpu.VMEM((tb, STEP_W), F32)] * 5,
        compiler_params=_cparams(("arbitrary", "arbitrary", "arbitrary")),
        name="rwkv_bwd" if reverse else "rwkv_fwd",
    )(mix, mix, mix, mix, mix, mix, yf,
      prm["w0"][d], prm["w2"][d], prm["a0"][d], prm["a2"][d], prm["g2"], prm["k_k"], prm["k_a"],
      prm["r_k"], prm["ln_w"], prm["ln_b"],
      ms, mi, same, eye, lev, tri, s0)
    return y, s_fin


def _rwkv_params(l, rw_w0, rw_w2, rw_a0, rw_a2, rw_g2, rw_k_k, rw_k_a, rw_r_k, rw_ln_w, rw_ln_b):
    row = lambda v: v.reshape(1, RW_D)
    return {
        "w0": [row(rw_w0[l, d]) for d in range(2)],
        "w2": [_pad_rows(rw_w2[l, d], LANES).astype(BF16) for d in range(2)],
        "a0": [row(rw_a0[l, d]) for d in range(2)],
        "a2": [_pad_rows(rw_a2[l, d], LANES).astype(BF16) for d in range(2)],
        "g2": _pad_rows(rw_g2[l], LANES).astype(BF16),
        "k_k": row(rw_k_k[l]), "k_a": row(rw_k_a[l]), "r_k": row(rw_r_k[l]),
        "ln_w": row(rw_ln_w[l]), "ln_b": row(rw_ln_b[l]),
    }


def _rwkv_mixer(mix, prm, s0_f, s0_b):
    b, L, _ = mix.shape
    y_f, s_f = _rwkv_direction(mix, None, prm, s0_f, 0, final=False)
    y, s_b = _rwkv_direction(mix, y_f, prm, s0_b, 1, final=True)
    return y, s_f, s_b


def _matmul_kernel(a_ref, b_ref, o_ref):
    o_ref[...] = jnp.dot(a_ref[...].astype(BF16), b_ref[...].astype(BF16), preferred_element_type=F32)


def _matmul(a, b, tn):
    m, k = a.shape
    n = b.shape[1]
    return pl.pallas_call(
        _matmul_kernel,
        grid=(n // tn,),
        in_specs=[pl.BlockSpec((m, k), lambda j: (0, 0)), pl.BlockSpec((k, tn), lambda j: (0, j))],
        out_specs=pl.BlockSpec((m, tn), lambda j: (0, j)),
        out_shape=jax.ShapeDtypeStruct((m, n), F32),
        compiler_params=_cparams(("arbitrary",)),
        name="matmul",
    )(a, b)


def _modulate(x, g, scale, shift):
    y = x * lax.rsqrt(jnp.mean(x * x, axis=-1, keepdims=True) + NORM_EPS)
    return y * g * (1.0 + scale) + shift


def _pack_cols(m):
    z = lambda n: jnp.zeros(m.shape[:-1] + (n,), m.dtype)
    s = lambda a, b: m[..., a:b]
    o = 2 * RW_D
    lo = [s(o + j * W_LORA, o + (j + 1) * W_LORA) for j in range(4)]
    r0 = o + 2 * W_LORA + 2 * A_LORA
    g0 = r0 + RW_D
    h0 = g0 + G_LORA
    parts = [s(0, RW_D), s(RW_D, 2 * RW_D), s(r0, r0 + RW_D)]
    for p in lo:
        parts += [p, z(LANES - W_LORA)]
    parts += [s(g0, g0 + G_LORA), z(LANES - G_LORA), z(SMALL_W - 5 * LANES)]
    if m.shape[-1] > h0 + 3 * HY_D:
        parts.append(s(h0 + 3 * HY_D, h0 + 3 * HY_D + 2 * D_MODEL))
    else:
        parts.append(z(2 * D_MODEL))
    parts.append(s(h0, h0 + 3 * HY_D))
    return jnp.concatenate(parts, axis=-1)


def _inproj_kernel(x_ref, g_ref, sc_ref, sh_ref, w_ref, cw_ref, o_ref, h_s, *, seg, gate):
    j = pl.program_id(2)

    @pl.when(j == 0)
    def _():
        h_s[...] = _modulate(x_ref[0], g_ref[...], sc_ref[0], sh_ref[0]).astype(BF16)

    out_blk = (0,) * (len(o_ref.shape) - 2)
    tm = h_s.shape[0]
    rc = min(tm, max(seg, PROJ_RC))
    for r0 in range(0, tm, rc):
        p = jnp.dot(h_s[r0:r0 + rc, :], w_ref[...], preferred_element_type=F32)
        if gate:
            res = jax.nn.sigmoid(p)
        else:
            row = lax.broadcasted_iota(jnp.int32, p.shape, 0) % seg
            prev = jnp.where(row == 0, 0.0, pltpu.roll(p, 1, 0))
            nxt = jnp.where(row == seg - 1, 0.0, pltpu.roll(p, rc - 1, 0))
            res = cw_ref[0:1, :] * prev + cw_ref[1:2, :] * p + cw_ref[2:3, :] * nxt
        o_ref[out_blk + (slice(r0, r0 + rc), slice(None))] = res


def _in_proj(x, g, scale, shift, w_packed, cw_packed, seg, part):
    b, L, d = x.shape
    tm = min(L, 1024)
    c0, c1 = {"rw": (0, COL_GATE), "gate": (COL_GATE, COL_HY), "hy": (COL_HY, N_PACKED)}[part]
    t0, nt = c0 // PROJ_TN, (c1 - c0) // PROJ_TN
    if part == "hy":
        per = HY_D // PROJ_TN
        out_spec = pl.BlockSpec((1, 1, tm, PROJ_TN), lambda bi, i, j: (j // per, bi, i, j % per))
        out_shape = jax.ShapeDtypeStruct((3, b, L, HY_D), F32)
    else:
        out_spec = pl.BlockSpec((1, tm, PROJ_TN), lambda bi, i, j: (bi, i, j))
        out_shape = jax.ShapeDtypeStruct((b, L, c1 - c0), F32)
    return pl.pallas_call(
        functools.partial(_inproj_kernel, seg=seg, gate=part == "gate"),
        grid=(b, L // tm, nt),
        in_specs=[pl.BlockSpec((1, tm, d), lambda bi, i, j: (bi, i, 0)),
                  pl.BlockSpec((1, d), lambda bi, i, j: (0, 0)),
                  pl.BlockSpec((1, 1, d), lambda bi, i, j: (bi, 0, 0)),
                  pl.BlockSpec((1, 1, d), lambda bi, i, j: (bi, 0, 0)),
                  pl.BlockSpec((d, PROJ_TN), lambda bi, i, j: (0, t0 + j)),
                  pl.BlockSpec((3, PROJ_TN), lambda bi, i, j: (0, t0 + j))],
        out_specs=out_spec,
        out_shape=out_shape,
        scratch_shapes=[pltpu.VMEM((tm, d), BF16)],
        compiler_params=_cparams(("arbitrary", "arbitrary", "arbitrary")),
        name="in_proj_" + part,
    )(x, g, scale, shift, w_packed, cw_packed)


def _outproj_kernel(yrw_ref, yhy_ref, grw_ref, ghy_ref, x_ref, ga_ref, w_ref, o_ref):
    y = grw_ref[0] * yrw_ref[0] + ghy_ref[0] * yhy_ref[0]
    o_ref[0] = x_ref[0] + ga_ref[0] * jnp.dot(y.astype(BF16), w_ref[...], preferred_element_type=F32)


def _out_proj(y_rw, y_hy, gates, x, ga, w_out_bf16):
    b, L, d = x.shape
    tm = 256
    tok = pl.BlockSpec((1, tm, d), lambda bi, i: (bi, i, 0))
    return pl.pallas_call(
        _outproj_kernel,
        grid=(b, L // tm),
        in_specs=[tok, tok,
                  tok,
                  pl.BlockSpec((1, tm, d), lambda bi, i: (bi, i, 1)),
                  tok,
                  pl.BlockSpec((1, 1, d), lambda bi, i: (bi, 0, 0)),
                  pl.BlockSpec((d, d), lambda bi, i: (0, 0))],
        out_specs=tok,
        out_shape=jax.ShapeDtypeStruct((b, L, d), F32),
        compiler_params=_cparams(("arbitrary", "arbitrary")),
        name="out_proj",
    )(y_rw, y_hy, gates, gates, x, ga, w_out_bf16)


PEER_TM = 512
PEER_TE = 512
PEER_SUB = 64
PEER_PAIR = 2
STAT_ROWS = 8


def _top_rows(s, n):
    rows = lax.broadcasted_iota(jnp.int32, (n, s.shape[1]), 0)
    out = jnp.zeros((n, s.shape[1]), F32)
    cur = s
    for i in range(n):
        m = jnp.max(cur, axis=0, keepdims=True)
        out = jnp.where(rows == i, m, out)
        cur = jnp.where(cur == m, -jnp.inf, cur)
    return out


def _route_kernel(x_ref, g_ref, sc_ref, sh_ref, wqt_ref, keys_ref, h_ref, s1_ref, c1_ref, s2_ref, st_ref):
    hb = _modulate(x_ref[0], g_ref[...], sc_ref[0], sh_ref[0]).astype(BF16)
    h_ref[0] = hb
    nt = (((1,), (1,)), ((), ()))
    qt = lax.dot_general(wqt_ref[...], hb, nt, preferred_element_type=F32)
    half = D_KEY // 2
    for hd in range(P_HEADS):
        sc = []
        for p in range(2):
            r0 = (hd * 2 + p) * half
            sc.append(jnp.dot(keys_ref[hd, p], qt[r0:r0 + half, :].astype(BF16),
                              preferred_element_type=F32))
        s1_ref[0, hd] = sc[0]
        s2_ref[0, hd] = sc[1]
        t1 = _top_rows(sc[0], P_TOPK)
        t2 = _top_rows(sc[1], P_TOPK)
        r8 = lax.broadcasted_iota(jnp.int32, (8, t1.shape[1]), 0)
        tiles = [t1[0:1] + t2[0:8], t1[0:1] + t2[8:16], t1[1:2] + t2[0:8]]
        for i in range(2, 8):
            tiles.append(jnp.where(r8 < P_TOPK // (i + 1), t1[i:i + 1] + t2[0:8], -jnp.inf))
        tiles.append(t1[8:16] + t2[0:1])
        best = _top_rows(jnp.concatenate(tiles, axis=0), P_TOPK)
        z = jnp.sum(jnp.exp(best - best[0:1]), axis=0, keepdims=True)
        c1_ref[0, hd] = jnp.exp(sc[0] - t1[0:1]) * (1.0 / z)
        st = jnp.where(r8 == 0, best[P_TOPK - 1:P_TOPK], 0.0)
        st_ref[0, hd] = jnp.where(r8 == 1, t2[0:1], st)


def _peer_route(x, g, scale, shift, wqt_bf16, keys_bf16):
    b, L, d = x.shape
    tm = min(L, PEER_TM)
    tok = lambda: pl.BlockSpec((1, tm, d), lambda bi, i: (bi, i, 0))
    rowv = pl.BlockSpec((1, 1, d), lambda bi, i: (bi, 0, 0))
    sspec = pl.BlockSpec((1, P_HEADS, N_KEYS, tm), lambda bi, i: (bi, 0, 0, i))
    return pl.pallas_call(
        _route_kernel,
        grid=(b, L // tm),
        in_specs=[tok(), pl.BlockSpec((1, d), lambda bi, i: (0, 0)), rowv, rowv,
                  pl.BlockSpec(wqt_bf16.shape, lambda bi, i: (0, 0)),
                  pl.BlockSpec(keys_bf16.shape, lambda bi, i: (0, 0, 0, 0))],
        out_specs=[tok(), sspec, sspec, sspec,
                   pl.BlockSpec((1, P_HEADS, STAT_ROWS, tm), lambda bi, i: (bi, 0, 0, i))],
        out_shape=[jax.ShapeDtypeStruct((b, L, d), BF16),
                   jax.ShapeDtypeStruct((b, P_HEADS, N_KEYS, L), F32),
                   jax.ShapeDtypeStruct((b, P_HEADS, N_KEYS, L), F32),
                   jax.ShapeDtypeStruct((b, P_HEADS, N_KEYS, L), F32),
                   jax.ShapeDtypeStruct((b, P_HEADS, STAT_ROWS, L), F32)],
        compiler_params=_cparams(("arbitrary", "arbitrary")),
        name="peer_route",
    )(x, g, scale, shift, wqt_bf16, keys_bf16)


def _peer_dense_kernel(h_ref, u_ref, vt_ref, s1_ref, c1_ref, s2_ref, st_ref, x_ref, ga_ref, o_ref,
                       acc_s, e2_s, ge_s, wa_s, wb_s, row_s):
    e = pl.program_id(2)
    last = pl.num_programs(2) - 1
    tm = h_ref.shape[1]
    n1 = PEER_TE // N_KEYS

    @pl.when(e == 0)
    def _():
        acc_s[...] = jnp.zeros_like(acc_s)
        wb_s[...] = jnp.zeros_like(wb_s)
        for hd in range(P_HEADS):
            e2_s[hd] = jnp.exp(s2_ref[0, hd] - st_ref[0, hd, 1:2, :])

    def step(w_prev, w_cur):
        nt = (((1,), (1,)), ((), ()))
        act = lax.dot_general(u_ref[...], h_ref[0], nt, preferred_element_type=F32)
        ge_s[...] = 0.5 * act * (1.0 + lax.erf(act * (1.0 / math.sqrt(2.0))))
        tile = jnp.minimum(e, last - 1)
        for i in range(n1):
            for hd in range(P_HEADS):
                row_s[0, i, hd] = s1_ref[0, hd, pl.ds(tile * n1 + i, 1), :]
                row_s[1, i, hd] = c1_ref[0, hd, pl.ds(tile * n1 + i, 1), :]
        nblk = (n1 // PEER_PAIR) * (tm // LANES)
        half = tm // 2
        blk = 0
        anchor = None
        for ip in range(0, n1, PEER_PAIR):
            pair = range(ip, ip + PEER_PAIR)
            for strip in range(tm // LANES):
                if blk % (nblk // 2) == 0:
                    ts = slice((blk // (nblk // 2)) * half, (blk // (nblk // 2) + 1) * half)
                    part = jnp.dot(vt_ref[...], w_prev[:, ts], preferred_element_type=F32)
                    acc_s[:, ts] += part
                    anchor = jnp.minimum(jnp.abs(part[0:1, 0:LANES]), 0.0)
                blk += 1
                cs = slice(strip * LANES, (strip + 1) * LANES)
                for sub in range(N_KEYS // PEER_SUB):
                    ks = slice(sub * PEER_SUB, (sub + 1) * PEER_SUB)
                    gsum = [None] * PEER_PAIR
                    for hd in range(P_HEADS):
                        s2t = s2_ref[0, hd, ks, cs]
                        e2t = e2_s[hd, ks, cs]
                        thr = st_ref[0, hd, 0:1, cs] + anchor
                        for q, i in enumerate(pair):
                            term = jnp.where((s2t + row_s[0, i, hd, :, cs]) >= thr,
                                             e2t * row_s[1, i, hd, :, cs], 0.0)
                            gsum[q] = term if gsum[q] is None else gsum[q] + term
                    for q, i in enumerate(pair):
                        ws = slice(i * N_KEYS + sub * PEER_SUB, i * N_KEYS + (sub + 1) * PEER_SUB)
                        w_cur[ws, cs] = (gsum[q] * ge_s[ws, cs]).astype(BF16)

    @pl.when(e % 2 == 0)
    def _():
        step(wb_s, wa_s)

    @pl.when(e % 2 == 1)
    def _():
        step(wa_s, wb_s)

    @pl.when(e == last)
    def _():
        o_ref[0] = x_ref[0] + ga_ref[0] * acc_s[...].T


def _peer_dense(h2, u_bf16, vt_bf16, s1, c1, s2, st, x, ga):
    b, L, d = x.shape
    tm = min(L, PEER_TM)
    tok = lambda: pl.BlockSpec((1, tm, d), lambda bi, i, e: (bi, i, 0))
    sspec = lambda: pl.BlockSpec((1, P_HEADS, N_KEYS, tm), lambda bi, i, e: (bi, 0, 0, i))
    ne = N_EXPERTS // PEER_TE
    return pl.pallas_call(
        _peer_dense_kernel,
        grid=(b, L // tm, ne + 1),
        in_specs=[tok(),
                  pl.BlockSpec((PEER_TE, d), lambda bi, i, e: (jnp.minimum(e, ne - 1), 0)),
                  pl.BlockSpec((d, PEER_TE), lambda bi, i, e: (0, jnp.maximum(e - 1, 0))),
                  sspec(), sspec(), sspec(),
                  pl.BlockSpec((1, P_HEADS, STAT_ROWS, tm), lambda bi, i, e: (bi, 0, 0, i)),
                  tok(),
                  pl.BlockSpec((1, 1, d), lambda bi, i, e: (bi, 0, 0))],
        out_specs=tok(),
        out_shape=jax.ShapeDtypeStruct((b, L, d), F32),
        scratch_shapes=[pltpu.VMEM((d, tm), F32), pltpu.VMEM((P_HEADS, N_KEYS, tm), F32),
                        pltpu.VMEM((PEER_TE, tm), F32), pltpu.VMEM((PEER_TE, tm), BF16),
                        pltpu.VMEM((PEER_TE, tm), BF16), pltpu.VMEM((2, PEER_TE // N_KEYS, P_HEADS, 1, tm), F32)],
        compiler_params=pltpu.CompilerParams(
            dimension_semantics=("arbitrary", "arbitrary", "arbitrary"),
            vmem_limit_bytes=56 * 1024 * 1024),
        name="peer_dense",
    )(h2, u_bf16, vt_bf16, s1, c1, s2, st, x, ga)


def _peer_block(x, g, scale, shift, ga, pw):
    h2, s1, c1, s2, st = _peer_route(x, g, scale, shift, pw["wqt"], pw["keys"])
    return _peer_dense(h2, pw["u"], pw["vt"], s1, c1, s2, st, x, ga)


def _final_norm_kernel(x_ref, g_ref, o_ref):
    x = x_ref[0]
    o_ref[0] = x * lax.rsqrt(jnp.mean(x * x, axis=-1, keepdims=True) + NORM_EPS) * g_ref[...]


def _final_norm(x, g):
    b, L, d = x.shape
    tm = 512
    tok = pl.BlockSpec((1, tm, d), lambda bi, i: (bi, i, 0))
    return pl.pallas_call(
        _final_norm_kernel,
        grid=(b, L // tm),
        in_specs=[tok, pl.BlockSpec((1, d), lambda bi, i: (0, 0))],
        out_specs=tok,
        out_shape=jax.ShapeDtypeStruct((b, L, d), F32),
        compiler_params=_cparams(("arbitrary", "arbitrary")),
        name="final_norm",
    )(x, g.reshape(1, d))


HY_N2 = 256
HY_CT = 1024
HY_S = 8
HY_OT = 512


def _hyena_filter(L, hy_p):
    w1, b1, w2, b2, w3, freq = hy_p
    t = jnp.linspace(0.0, 1.0, L, dtype=F32)[:, None]
    bands = jnp.linspace(1e-4, HY_BANDS - 1, HY_BANDS, dtype=F32)[None, :]
    ang = (2.0 * math.pi / L) * jnp.arange(L, dtype=F32)[:, None] * bands
    z = jnp.concatenate([t, jnp.cos(ang), -jnp.sin(ang)], axis=-1)
    hdn = jnp.sin(freq * (z @ w1 + b1))
    hdn = jnp.sin(freq * (hdn @ w2 + b2))
    pos = jnp.arange(L)
    posb = (L - pos) % L
    hdn2 = jnp.stack([hdn, hdn[posb]])
    t2 = jnp.stack([t, t[posb]])
    deltas = jnp.abs(jnp.linspace(math.log(HY_TARGET) / HY_SLOW_PCT, math.log(HY_TARGET) / HY_FAST_PCT,
                                  HY_D, dtype=F32))
    w3r = w3.reshape(HY_HIDDEN, HY_ORDER, 2, HY_D)
    first = (jnp.arange(L) == 0)[None, :, None]
    side = jnp.arange(2)[:, None, None]
    out = []
    for o in range(HY_ORDER):
        h = jnp.einsum("slh,hsc->slc", hdn2, w3r[:, o]) * jnp.exp(-t2 * deltas)
        h = h * lax.rsqrt(jnp.sum(h * h, axis=(0, 1), keepdims=True))
        g = h + jnp.where(first & (side == 0), h[1:2, 0:1, :], 0.0)
        g = jnp.where(first & (side == 1), 0.0, g)
        out.append(g.reshape(2 * L, HY_D))
    return out


def _cis(idx, n):
    ang = (2.0 * math.pi / n) * (idx % n).astype(F32)
    return jnp.cos(ang), -jnp.sin(ang)


def _block_complex(re, im):
    return jnp.concatenate([jnp.concatenate([re, -im], axis=-1),
                            jnp.concatenate([im, re], axis=-1)], axis=-2)


def _dft_mats(n1, n2, kin):
    k = jnp.arange(n1, dtype=jnp.int32)
    c1, s1 = _cis(k[:, None] * k[None, :], n1)
    half = max(n1 // 2, 1)
    eye = jnp.eye(HY_S, dtype=F32)
    outer = jnp.kron(_block_complex(c1[:, :half], s1[:, :half]), eye).astype(BF16)
    outer_real = jnp.kron(jnp.concatenate([c1, s1], axis=0), eye).astype(BF16)
    m2 = jnp.arange(n2, dtype=jnp.int32)
    freq = k[:, None, None] + n1 * m2[None, :, None]
    gr, gi = _cis(m2[None, None, :] * freq, n1 * n2)
    inner = _block_complex(gr[:, :, :kin], gi[:, :, :kin]).astype(BF16)
    inner_real = jnp.concatenate([gr, gi], axis=1).astype(BF16)
    return {"n1": n1, "outer": outer, "outer_real": outer_real, "inner": inner, "inner_real": inner_real}


def _hy_outer_fwd_kernel(x_ref, m_ref, o_ref):
    ct = x_ref.shape[-1]
    x = x_ref[0].reshape(-1, ct).astype(BF16)
    f = jnp.dot(m_ref[...], x, preferred_element_type=F32)
    o_ref[...] = f.reshape(o_ref.shape)


def _hy_outer_fwd(x, which, mat8):
    _, groups, r, n2, c = x.shape
    n1 = mat8.shape[0] // (2 * HY_S)
    ct = min(c, HY_OT)
    return pl.pallas_call(
        _hy_outer_fwd_kernel,
        grid=(n2 // HY_S, c // ct),
        in_specs=[pl.BlockSpec((1, groups, r, HY_S, ct), lambda q, j: (which, 0, 0, q, j)),
                  pl.BlockSpec(mat8.shape, lambda q, j: (0, 0))],
        out_specs=pl.BlockSpec((2, n1, HY_S, ct), lambda q, j: (0, 0, q, j)),
        out_shape=jax.ShapeDtypeStruct((2, n1, n2, c), F32),
        compiler_params=_cparams(("arbitrary", "arbitrary")),
        name="hy_outer_fwd",
    )(x, mat8)


def _hy_inner_kernel(*refs, conv, scale):
    if conv:
        a_ref, g_ref, s_ref, o_ref = refs
    else:
        a_ref, g_ref, o_ref = refs
    parts = [a_ref[p, 0] for p in range(a_ref.shape[0])]
    a = (jnp.concatenate(parts, axis=0) if len(parts) > 1 else parts[0]).astype(BF16)
    g = g_ref[0]
    f = jnp.dot(g, a, preferred_element_type=F32)
    ko = f.shape[0] // 2
    fr, fi = f[:ko], f[ko:]
    if not conv:
        o_ref[0, 0] = fr * scale
        o_ref[1, 0] = fi * scale
    else:
        sr, si = s_ref[0, 0], s_ref[1, 0]
        p = jnp.concatenate([fr * sr - fi * si, fr * si + fi * sr], axis=0).astype(BF16)
        d = lax.dot_general(g, p, (((0,), (0,)), ((), ())), preferred_element_type=F32)
        kin = d.shape[0] // 2
        o_ref[0, 0] = d[:kin].astype(o_ref.dtype)
        o_ref[1, 0] = d[kin:].astype(o_ref.dtype)


def _hy_inner(a, gmat, spec, out_dtype, scale=1.0):
    p, n1, kin, c = a.shape
    ko = gmat.shape[1] // 2
    conv = spec is not None
    ct = min(c, HY_CT)
    in_specs = [pl.BlockSpec((p, 1, kin, ct), lambda k, j: (0, k, 0, j)),
                pl.BlockSpec((1,) + gmat.shape[1:], lambda k, j: (k, 0, 0))]
    args = [a, gmat]
    if conv:
        in_specs.append(pl.BlockSpec((2, 1, ko, ct), lambda k, j: (0, k, 0, j)))
        args.append(spec)
    rows = kin if conv else ko
    return pl.pallas_call(
        functools.partial(_hy_inner_kernel, conv=conv, scale=scale),
        grid=(n1, c // ct),
        in_specs=in_specs,
        out_specs=pl.BlockSpec((2, 1, rows, ct), lambda k, j: (0, k, 0, j)),
        out_shape=jax.ShapeDtypeStruct((2, n1, rows, c), out_dtype),
        compiler_params=_cparams(("arbitrary", "arbitrary")),
        name="hy_inner_conv" if conv else "hy_inner_spec",
    )(*args)


def _hy_outer_inv_kernel(d_ref, m_ref, v_ref, x_ref, sk_ref, o_ref):
    ct = d_ref.shape[-1]
    d = d_ref[...].reshape(-1, ct).astype(BF16)
    y = lax.dot_general(m_ref[...], d, (((0,), (0,)), ((), ())), preferred_element_type=F32)
    y = y.reshape(o_ref.shape[1:])
    o_ref[0] = x_ref[0] * (y + v_ref[0] * sk_ref[...])


def _hy_outer_inv(d, mat8, src, si, gate, gi, skip_row):
    _, n1, n2, c = d.shape
    half = n1 // 2
    ct = min(c, HY_OT)
    blk = lambda which: pl.BlockSpec((1, 2, half, HY_S, ct), lambda q, j: (which, 0, 0, q, j))
    return pl.pallas_call(
        _hy_outer_inv_kernel,
        grid=(n2 // HY_S, c // ct),
        in_specs=[pl.BlockSpec((2, n1, HY_S, ct), lambda q, j: (0, 0, q, j)),
                  pl.BlockSpec(mat8.shape, lambda q, j: (0, 0)),
                  blk(si), blk(gi),
                  pl.BlockSpec((1, ct), lambda q, j: (0, j))],
        out_specs=blk(0),
        out_shape=jax.ShapeDtypeStruct((1, 2, half, n2, c), F32),
        compiler_params=_cparams(("arbitrary", "arbitrary")),
        name="hy_outer_inv",
    )(d, mat8, src, gate, skip_row)


def _hyena_spectrum(g, mats):
    n, c = g[0].shape
    n1 = mats["n1"]
    n2 = n // n1
    out = []
    for go in g:
        if n1 > 1:
            a = _hy_outer_fwd(go.reshape(1, 1, n1, n2, c), 0, mats["outer_real"])
            out.append(_hy_inner(a, mats["inner"], None, F32, 1.0 / n))
        else:
            out.append(_hy_inner(go.reshape(1, 1, n, c), mats["inner_real"], None, F32, 1.0 / n))
    return out


def _hy_gate_kernel(y_ref, v_ref, x_ref, sk_ref, o_ref):
    o_ref[...] = x_ref[...] * (y_ref[...] + v_ref[...] * sk_ref[...])


def _hy_gate(y, v, x, skip_row):
    b, L, c = y.shape
    tok = pl.BlockSpec((1, L, c), lambda bi: (bi, 0, 0))
    return pl.pallas_call(
        _hy_gate_kernel,
        grid=(b,),
        in_specs=[tok, tok, tok, pl.BlockSpec((1, c), lambda bi: (0, 0))],
        out_specs=tok,
        out_shape=jax.ShapeDtypeStruct((b, L, c), F32),
        compiler_params=_cparams(("arbitrary",)),
        name="hy_gate",
    )(y, v, x, skip_row)


def _hyena_mixer(hy, spec, skip, mats):
    _, b, L, c = hy.shape
    assert b == 2, "the two batch entries are packed as one complex signal"
    n1 = mats["n1"]
    if n1 == 1:
        z = hy[0]
        for o in range(HY_ORDER):
            y = _hy_inner(z.reshape(2, 1, L, c), mats["inner"], spec[o], F32).reshape(b, L, c)
            z = _hy_gate(y, z, hy[o + 1], skip[o].reshape(1, c))
        return z
    n2 = 2 * L // n1
    src, si = hy.reshape(3, b, n1 // 2, n2, c), 0
    gate = src
    for o in range(HY_ORDER):
        a = _hy_outer_fwd(src, si, mats["outer"])
        d = _hy_inner(a, mats["inner"], spec[o], F32)
        src, si = _hy_outer_inv(d, mats["outer"], src, si, gate, o + 1, skip[o].reshape(1, c)), 0
    return src.reshape(b, L, c)


def _token_mixing(x, g, scale, shift, ga, lw, n_seg, s0_f, s0_b, hy_p):
    b, L, _ = x.shape
    seg = L // n_seg
    proj = functools.partial(_in_proj, x, g, scale, shift, lw["w_in"], lw["conv_w"], seg)
    y_rw, s_f, s_b = _rwkv_mixer(proj("rw"), lw["rw"], s0_f, s0_b)
    if hy_p is None:
        return None, s_f, s_b
    n = 2 * L
    n1 = n // HY_N2 if n // HY_N2 >= 16 else 1
    mats = _dft_mats(n1, n // n1, L if n1 == 1 else n // n1)
    spec = _hyena_spectrum(_hyena_filter(L, hy_p), mats)
    y_hy = _hyena_mixer(proj("hy"), spec, lw["hy_skip"], mats)
    return _out_proj(y_rw, y_hy, proj("gate"), x, ga, lw["w_out"]), s_f, s_b


def kernel(x, c, ctx, c_ctx, ada_w, ada_b, norm1_g, norm2_g, w_in, conv_w, rw_w0, rw_w2, rw_a0, rw_a2, rw_g2, rw_k_k, rw_k_a, rw_r_k, rw_ln_w, rw_ln_b, hy_w1, hy_b1, hy_w2, hy_b2, hy_w3, hy_freq, hy_skip, w_out, peer_wq, peer_keys, peer_u, peer_v, final_g):
    b, L, d = x.shape
    depth = ada_w.shape[0]
    ctx_len = ctx.shape[1]
    s_zero = jnp.zeros((b, N_GROUPS, GROUP_W, GROUP_W), F32)
    cond = jnp.concatenate([c, c_ctx[None, :]], axis=0)
    cond = jnp.pad(jax.nn.silu(cond), ((0, 8 - (b + 1)), (0, 0)))
    for l in range(depth):
        last = l == depth - 1
        mod = _matmul(cond, ada_w[l], 1024) + ada_b[l]
        mx = [m[:, None, :] for m in jnp.split(mod[:b], 6, axis=-1)]
        mc = [jnp.broadcast_to(m[None, :, :], (b, 1, d)) for m in jnp.split(mod[b:b + 1], 6, axis=-1)]
        lw = {
            "w_in": _pack_cols(w_in[l]).astype(BF16),
            "conv_w": _pack_cols(conv_w[l]),
            "rw": _rwkv_params(l, rw_w0, rw_w2, rw_a0, rw_a2, rw_g2, rw_k_k, rw_k_a, rw_r_k, rw_ln_w, rw_ln_b),
            "hy_skip": hy_skip[l],
            "w_out": w_out[l].astype(BF16),
        }
        hy_p = (hy_w1[l], hy_b1[l], hy_w2[l], hy_b2[l], hy_w3[l], hy_freq[l])
        pw = {
            "wqt": peer_wq[l].T.astype(BF16),
            "keys": peer_keys[l].astype(BF16),
            "u": peer_u[l].astype(BF16),
            "vt": peer_v[l].T.astype(BF16),
        }
        g1 = norm1_g[l].reshape(1, d)
        g2 = norm2_g[l].reshape(1, d)

        ctx_new, s_f, s_b = _token_mixing(ctx, g1, mc[1], mc[0], mc[2], lw, 1, s_zero, s_zero,
                                          None if last else hy_p)
        if not last:
            ctx = _peer_block(ctx_new, g2, mc[4], mc[3], mc[5], pw)

        x, _, _ = _token_mixing(x, g1, mx[1], mx[0], mx[2], lw, L // GRID_W, s_f, s_b, hy_p)
        x = _peer_block(x, g2, mx[4], mx[3], mx[5], pw)
    return _final_norm(x, final_g)
```

```python
import functools
import math

import numpy as np
import jax
import jax.numpy as jnp
from jax import lax
from jax.experimental import pallas as pl
from jax.experimental.pallas import tpu as pltpu

F32 = jnp.float32
BF16 = jnp.bfloat16

D_MODEL = 2048
GRID_W = 64
NORM_EPS = 1e-6
RW_N = 64
RW_H = D_MODEL // RW_N
RW_D = D_MODEL
W_LORA = 96
A_LORA = 96
G_LORA = 64
RW_GN_EPS = 64e-5
HY_D = D_MODEL
HY_ORDER = 2
HY_EMB = 33
HY_BANDS = (HY_EMB - 1) // 2
HY_HIDDEN = 64
HY_TARGET = 1e-2
HY_FAST_PCT = 0.3
HY_SLOW_PCT = 1.5
P_HEADS = 8
N_KEYS = 128
N_EXPERTS = N_KEYS * N_KEYS
P_TOPK = 16
D_KEY = 256

LANES = 128
MXU_DIM = 256
VMEM_LIMIT = 48 * 1024 * 1024

COL_K, COL_V, COL_R = 0, RW_D, 2 * RW_D
COL_SMALL = 3 * RW_D
SMALL_W = 1024
COL_GATE = COL_SMALL + SMALL_W
COL_HY = COL_GATE + 2 * D_MODEL
N_PACKED = COL_HY + 3 * HY_D
PROJ_TN = 512
PROJ_RC = 256

CHUNK = 64
HEADS_PER_GROUP = MXU_DIM // RW_N
GROUP_W = HEADS_PER_GROUP * RW_N
N_GROUPS = RW_D // GROUP_W
STACK = HEADS_PER_GROUP * CHUNK
GROUPS_PER_STEP = 8
STEP_W = GROUPS_PER_STEP * GROUP_W
RWKV_TB = 128


def _cparams(sem):
    return pltpu.CompilerParams(dimension_semantics=sem, vmem_limit_bytes=VMEM_LIMIT)


def _rwkv_consts(reverse):
    t = np.arange(CHUNK)
    tt, ss = t[:, None], t[None, :]
    strict = (tt < ss) if reverse else (tt > ss)
    eye = tt == ss
    levels = []
    m = 1
    while m < CHUNK:
        levels.append(strict & ((tt // (2 * m)) == (ss // (2 * m))) & ((tt // m) != (ss // m)))
        m *= 2
    lanes = lambda a: np.tile(a, (1,) * (a.ndim - 1) + (HEADS_PER_GROUP,))
    hb = np.arange(STACK) // CHUNK
    same = hb[:, None] == hb[None, :]
    tri = (tt <= ss) if reverse else (tt >= ss)
    f = lambda a: jnp.asarray(a.astype(np.float32))
    return (f(lanes(strict)), f(lanes(strict | eye)), f(same), f(lanes(eye)),
            f(lanes(np.stack(levels))), f(tri))


def _rwkv_kernel(k_ref, v_ref, r_ref, wd_ref, ad_ref, gd_ref, yf_ref,
                 w0_ref, w2_ref, a0_ref, a2_ref, g2_ref, kk_ref, ka_ref, rk_ref, lnw_ref, lnb_ref,
                 ms_ref, mi_ref, same_ref, eye_ref, lev_ref, tri_ref, s0_ref,
                 y_ref, sout_ref,
                 S_ref, lw_s, kt_s, kn_s, b_s, yacc_s, *, reverse, final, nchunk):
    step = pl.program_id(2)

    @pl.when(step == 0)
    def _():
        S_ref[...] = s0_ref[0]

    same = same_ref[...]
    gsl = [slice(g * GROUP_W, (g + 1) * GROUP_W) for g in range(GROUPS_PER_STEP)]

    same_b = same.astype(BF16)

    def split(x, pieces):
        out = []
        for _ in range(pieces):
            p = x.astype(BF16)
            out.append(p)
            x = x - p.astype(F32)
        return out

    def head_sum(x):
        cols = []
        for s in gsl:
            hi, lo = split(x[:, s], 2)
            cols.append(jnp.dot(hi, same_b, preferred_element_type=F32)
                        + jnp.dot(lo, same_b, preferred_element_type=F32))
        return jnp.concatenate(cols, axis=1)

    k = k_ref[0]
    wraw = w0_ref[...] + jnp.dot(jnp.tanh(wd_ref[0]).astype(BF16), w2_ref[...],
                                 preferred_element_type=F32)
    lw_s[...] = -jax.nn.sigmoid(wraw) * math.exp(-0.5)
    a = jax.nn.sigmoid(a0_ref[...] + jnp.dot(ad_ref[0].astype(BF16), a2_ref[...],
                                             preferred_element_type=F32))
    kt_s[...] = k * (1.0 + (a - 1.0) * ka_ref[...])
    kn = k * kk_ref[...]
    kn = kn * lax.rsqrt(jnp.maximum(head_sum(kn * kn), 1e-24))
    kn_s[...] = kn
    b_s[...] = kn * a

    ms = ms_ref[...]
    mi = mi_ref[...]
    tri = tri_ref[...].astype(BF16)
    mid = CHUNK // 2 if reverse else CHUNK // 2 - 1
    last = 0 if reverse else CHUNK - 1

    def bd(x):
        return jnp.concatenate([x.astype(BF16)] * HEADS_PER_GROUP, axis=0) * same_b

    def mm(a, w):
        return jnp.dot(a.astype(BF16), w, preferred_element_type=F32)

    nt = (((1,), (1,)), ((), ()))
    tn = (((0,), (0,)), ((), ()))

    def chunk_group(rows, g):
        cols = gsl[g]
        lw = lw_s[rows, cols]
        kt = kt_s[rows, cols]
        kn = kn_s[rows, cols]
        b = b_s[rows, cols]
        r = r_ref[0, rows, cols]
        v = v_ref[0, rows, cols]
        c = sum(jnp.dot(tri, p, preferred_element_type=F32) for p in split(lw, 3))
        yield
        cp = c - lw
        rho = c[mid:mid + 1, :]
        cend = c[last:last + 1, :]
        einv = jnp.exp(rho - c)
        eend = jnp.exp(cend - c)
        lhs = jnp.concatenate([kn * jnp.exp(cp - rho), r * jnp.exp(c - rho)], axis=0).astype(BF16)
        rhs = jnp.concatenate([bd(kt * einv), bd(b * einv)], axis=0)
        ag = lax.dot_general(lhs, rhs, nt, preferred_element_type=F32)
        yield
        ak = ag[:CHUNK, :STACK] * ms
        ab = ag[:CHUNK, STACK:] * ms
        gk = ag[CHUNK:, :STACK] * mi
        gb = ag[CHUNK:, STACK:] * mi
        tm = eye_ref[...] - ab * lev_ref[0]
        for lv in range(1, lev_ref.shape[0]):
            te = mm(tm, bd(ab * lev_ref[lv]))
            yield
            tm = tm - mm(te, bd(tm))
            yield
        S = S_ref[g]
        lhs_s = jnp.concatenate([kn * jnp.exp(cp), r * jnp.exp(c)], axis=0).astype(BF16)
        qs = lax.dot_general(lhs_s, S.astype(BF16), nt, preferred_element_type=F32)
        vs = bd(v)
        rhs_u = qs[:CHUNK] + mm(ak, vs)
        yield
        us = mm(tm, bd(rhs_u))
        yield
        y = qs[CHUNK:] + mm(jnp.concatenate([gk, gb], axis=1), jnp.concatenate([vs, bd(-us)], axis=0))
        yield
        if final:
            yacc_s[rows, cols] = y
        else:
            y_ref[0, rows, cols] = y
        vu = jnp.concatenate([v, -us], axis=0).astype(BF16)
        kb = jnp.concatenate([kt * eend, b * eend], axis=0).astype(BF16)
        ds = lax.dot_general(vu, kb, tn, preferred_element_type=F32)
        S_ref[g] = S * jnp.exp(cend) + ds * same

    def chunk(ci, carry):
        c_idx = (nchunk - 1 - ci) if reverse else ci
        rows = pl.ds(pl.multiple_of(c_idx * CHUNK, CHUNK), CHUNK)
        pending = [chunk_group(rows, g) for g in range(GROUPS_PER_STEP)]
        while pending:
            pending = [gen for gen in pending if next(gen, True) is None]
        return carry

    lax.fori_loop(0, nchunk, chunk, 0)
    sout_ref[0] = S_ref[...]

    if final:
        y = yf_ref[0] + yacc_s[...]
        inv_n = 1.0 / RW_N
        mu = head_sum(y) * inv_n
        d = y - mu
        var = head_sum(d * d) * inv_n
        yn = d * lax.rsqrt(var + RW_GN_EPS) * lnw_ref[...] + lnb_ref[...]
        bonus = head_sum(r_ref[0] * k * rk_ref[...]) * v_ref[0]
        g = jnp.dot(jax.nn.sigmoid(gd_ref[0]).astype(BF16), g2_ref[...], preferred_element_type=F32)
        y_ref[0] = (yn + bonus) * g


def _pad_rows(w, rows):
    return jnp.pad(w, ((0, rows - w.shape[0]), (0, 0)))


def _rwkv_direction(mix, yf, prm, s0, d, *, final):
    b, L, _ = mix.shape
    reverse = d == 1
    tb = min(L, RWKV_TB)
    nb = L // tb
    nchunk = tb // CHUNK
    ms, mi, same, eye, lev, tri = _rwkv_consts(reverse)
    blk = (lambda i: nb - 1 - i) if reverse else (lambda i: i)
    sm = COL_SMALL // LANES

    def col(c0):
        return pl.BlockSpec((1, tb, STEP_W), lambda bi, g, i: (bi, blk(i), c0 // STEP_W + g))

    def small(j):
        return pl.BlockSpec((1, tb, LANES), lambda bi, g, i: (bi, blk(i), sm + j))

    vec = pl.BlockSpec((1, STEP_W), lambda bi, g, i: (0, g))
    lora = pl.BlockSpec((LANES, STEP_W), lambda bi, g, i: (0, g))
    const2 = lambda shp: pl.BlockSpec(shp, lambda bi, g, i: (0,) * len(shp))
    state = pl.BlockSpec((1, GROUPS_PER_STEP, GROUP_W, GROUP_W), lambda bi, g, i: (bi, g, 0, 0))
    yspec = pl.BlockSpec((1, tb, STEP_W), lambda bi, g, i: (bi, blk(i), g))

    kern = functools.partial(_rwkv_kernel, reverse=reverse, final=final, nchunk=nchunk)
    if yf is None:
        yf, yf_spec = mix, col(COL_K)
    else:
        yf_spec = yspec
    y, s_fin = pl.pallas_call(
        kern,
        grid=(b, N_GROUPS // GROUPS_PER_STEP, nb),
        in_specs=[col(COL_K), col(COL_V), col(COL_R), small(d), small(2 + d), small(4), yf_spec,
                  vec, lora, vec, lora, lora, vec, vec, vec, vec, vec,
                  const2((CHUNK, STACK)), const2((CHUNK, STACK)), const2((STACK, STACK)),
                  const2((CHUNK, STACK)), const2(tuple(lev.shape)), const2((CHUNK, CHUNK)), state],
        out_specs=[yspec, state],
        out_shape=[jax.ShapeDtypeStruct((b, L, RW_D), F32),
                   jax.ShapeDtypeStruct((b, N_GROUPS, GROUP_W, GROUP_W), F32)],
        scratch_shapes=[pltpu.VMEM((GROUPS_PER_STEP, GROUP_W, GROUP_W), F32)]
        + [pltpu.VMEM((tb, STEP_W), F32)] * 5,
        compiler_params=_cparams(("arbitrary", "arbitrary", "arbitrary")),
        name="rwkv_bwd" if reverse else "rwkv_fwd",
    )(mix, mix, mix, mix, mix, mix, yf,
      prm["w0"][d], prm["w2"][d], prm["a0"][d], prm["a2"][d], prm["g2"], prm["k_k"], prm["k_a"],
      prm["r_k"], prm["ln_w"], prm["ln_b"],
      ms, mi, same, eye, lev, tri, s0)
    return y, s_fin


def _rwkv_params(l, rw_w0, rw_w2, rw_a0, rw_a2, rw_g2, rw_k_k, rw_k_a, rw_r_k, rw_ln_w, rw_ln_b):
    row = lambda v: v.reshape(1, RW_D)
    return {
        "w0": [row(rw_w0[l, d]) for d in range(2)],
        "w2": [_pad_rows(rw_w2[l, d], LANES).astype(BF16) for d in range(2)],
        "a0": [row(rw_a0[l, d]) for d in range(2)],
        "a2": [_pad_rows(rw_a2[l, d], LANES).astype(BF16) for d in range(2)],
        "g2": _pad_rows(rw_g2[l], LANES).astype(BF16),
        "k_k": row(rw_k_k[l]), "k_a": row(rw_k_a[l]), "r_k": row(rw_r_k[l]),
        "ln_w": row(rw_ln_w[l]), "ln_b": row(rw_ln_b[l]),
    }


def _rwkv_mixer(mix, prm, s0_f, s0_b):
    b, L, _ = mix.shape
    y_f, s_f = _rwkv_direction(mix, None, prm, s0_f, 0, final=False)
    y, s_b = _rwkv_direction(mix, y_f, prm, s0_b, 1, final=True)
    return y, s_f, s_b


def _matmul_kernel(a_ref, b_ref, o_ref):
    o_ref[...] = jnp.dot(a_ref[...].astype(BF16), b_ref[...].astype(BF16), preferred_element_type=F32)


def _matmul(a, b, tn):
    m, k = a.shape
    n = b.shape[1]
    return pl.pallas_call(
        _matmul_kernel,
        grid=(n // tn,),
        in_specs=[pl.BlockSpec((m, k), lambda j: (0, 0)), pl.BlockSpec((k, tn), lambda j: (0, j))],
        out_specs=pl.BlockSpec((m, tn), lambda j: (0, j)),
        out_shape=jax.ShapeDtypeStruct((m, n), F32),
        compiler_params=_cparams(("arbitrary",)),
        name="matmul",
    )(a, b)


def _modulate(x, g, scale, shift):
    y = x * lax.rsqrt(jnp.mean(x * x, axis=-1, keepdims=True) + NORM_EPS)
    return y * g * (1.0 + scale) + shift


def _pack_cols(m):
    z = lambda n: jnp.zeros(m.shape[:-1] + (n,), m.dtype)
    s = lambda a, b: m[..., a:b]
    o = 2 * RW_D
    lo = [s(o + j * W_LORA, o + (j + 1) * W_LORA) for j in range(4)]
    r0 = o + 2 * W_LORA + 2 * A_LORA
    g0 = r0 + RW_D
    h0 = g0 + G_LORA
    parts = [s(0, RW_D), s(RW_D, 2 * RW_D), s(r0, r0 + RW_D)]
    for p in lo:
        parts += [p, z(LANES - W_LORA)]
    parts += [s(g0, g0 + G_LORA), z(LANES - G_LORA), z(SMALL_W - 5 * LANES)]
    if m.shape[-1] > h0 + 3 * HY_D:
        parts.append(s(h0 + 3 * HY_D, h0 + 3 * HY_D + 2 * D_MODEL))
    else:
        parts.append(z(2 * D_MODEL))
    parts.append(s(h0, h0 + 3 * HY_D))
    return jnp.concatenate(parts, axis=-1)


def _inproj_kernel(x_ref, g_ref, sc_ref, sh_ref, w_ref, cw_ref, o_ref, h_s, *, seg, gate):
    j = pl.program_id(2)

    @pl.when(j == 0)
    def _():
        h_s[...] = _modulate(x_ref[0], g_ref[...], sc_ref[0], sh_ref[0]).astype(BF16)

    out_blk = (0,) * (len(o_ref.shape) - 2)
    tm = h_s.shape[0]
    rc = min(tm, max(seg, PROJ_RC))
    for r0 in range(0, tm, rc):
        p = jnp.dot(h_s[r0:r0 + rc, :], w_ref[...], preferred_element_type=F32)
        if gate:
            res = jax.nn.sigmoid(p)
        else:
            row = lax.broadcasted_iota(jnp.int32, p.shape, 0) % seg
            prev = jnp.where(row == 0, 0.0, pltpu.roll(p, 1, 0))
            nxt = jnp.where(row == seg - 1, 0.0, pltpu.roll(p, rc - 1, 0))
            res = cw_ref[0:1, :] * prev + cw_ref[1:2, :] * p + cw_ref[2:3, :] * nxt
        o_ref[out_blk + (slice(r0, r0 + rc), slice(None))] = res


def _in_proj(x, g, scale, shift, w_packed, cw_packed, seg, part):
    b, L, d = x.shape
    tm = min(L, 1024)
    c0, c1 = {"rw": (0, COL_GATE), "gate": (COL_GATE, COL_HY), "hy": (COL_HY, N_PACKED)}[part]
    t0, nt = c0 // PROJ_TN, (c1 - c0) // PROJ_TN
    if part == "hy":
        per = HY_D // PROJ_TN
        out_spec = pl.BlockSpec((1, 1, tm, PROJ_TN), lambda bi, i, j: (j // per, bi, i, j % per))
        out_shape = jax.ShapeDtypeStruct((3, b, L, HY_D), F32)
    else:
        out_spec = pl.BlockSpec((1, tm, PROJ_TN), lambda bi, i, j: (bi, i, j))
        out_shape = jax.ShapeDtypeStruct((b, L, c1 - c0), F32)
    return pl.pallas_call(
        functools.partial(_inproj_kernel, seg=seg, gate=part == "gate"),
        grid=(b, L // tm, nt),
        in_specs=[pl.BlockSpec((1, tm, d), lambda bi, i, j: (bi, i, 0)),
                  pl.BlockSpec((1, d), lambda bi, i, j: (0, 0)),
                  pl.BlockSpec((1, 1, d), lambda bi, i, j: (bi, 0, 0)),
                  pl.BlockSpec((1, 1, d), lambda bi, i, j: (bi, 0, 0)),
                  pl.BlockSpec((d, PROJ_TN), lambda bi, i, j: (0, t0 + j)),
                  pl.BlockSpec((3, PROJ_TN), lambda bi, i, j: (0, t0 + j))],
        out_specs=out_spec,
        out_shape=out_shape,
        scratch_shapes=[pltpu.VMEM((tm, d), BF16)],
        compiler_params=_cparams(("arbitrary", "arbitrary", "arbitrary")),
        name="in_proj_" + part,
    )(x, g, scale, shift, w_packed, cw_packed)


def _outproj_kernel(yrw_ref, yhy_ref, grw_ref, ghy_ref, x_ref, ga_ref, w_ref, o_ref):
    y = grw_ref[0] * yrw_ref[0] + ghy_ref[0] * yhy_ref[0]
    o_ref[0] = x_ref[0] + ga_ref[0] * jnp.dot(y.astype(BF16), w_ref[...], preferred_element_type=F32)


def _out_proj(y_rw, y_hy, gates, x, ga, w_out_bf16):
    b, L, d = x.shape
    tm = 256
    tok = pl.BlockSpec((1, tm, d), lambda bi, i: (bi, i, 0))
    return pl.pallas_call(
        _outproj_kernel,
        grid=(b, L // tm),
        in_specs=[tok, tok,
                  tok,
                  pl.BlockSpec((1, tm, d), lambda bi, i: (bi, i, 1)),
                  tok,
                  pl.BlockSpec((1, 1, d), lambda bi, i: (bi, 0, 0)),
                  pl.BlockSpec((d, d), lambda bi, i: (0, 0))],
        out_specs=tok,
        out_shape=jax.ShapeDtypeStruct((b, L, d), F32),
        compiler_params=_cparams(("arbitrary", "arbitrary")),
        name="out_proj",
    )(y_rw, y_hy, gates, gates, x, ga, w_out_bf16)


PEER_TM = 512
PEER_TE = 512
PEER_SUB = 64
PEER_PAIR = 2
STAT_ROWS = 8


def _top_rows(s, n):
    rows = lax.broadcasted_iota(jnp.int32, (n, s.shape[1]), 0)
    out = jnp.zeros((n, s.shape[1]), F32)
    cur = s
    for i in range(n):
        m = jnp.max(cur, axis=0, keepdims=True)
        out = jnp.where(rows == i, m, out)
        cur = jnp.where(cur == m, -jnp.inf, cur)
    return out


def _route_kernel(x_ref, g_ref, sc_ref, sh_ref, wqt_ref, keys_ref, h_ref, s1_ref, c1_ref, s2_ref, st_ref):
    hb = _modulate(x_ref[0], g_ref[...], sc_ref[0], sh_ref[0]).astype(BF16)
    h_ref[0] = hb
    nt = (((1,), (1,)), ((), ()))
    qt = lax.dot_general(wqt_ref[...], hb, nt, preferred_element_type=F32)
    half = D_KEY // 2
    for hd in range(P_HEADS):
        sc = []
        for p in range(2):
            r0 = (hd * 2 + p) * half
            sc.append(jnp.dot(keys_ref[hd, p], qt[r0:r0 + half, :].astype(BF16),
                              preferred_element_type=F32))
        s1_ref[0, hd] = sc[0]
        s2_ref[0, hd] = sc[1]
        t1 = _top_rows(sc[0], P_TOPK)
        t2 = _top_rows(sc[1], P_TOPK)
        r8 = lax.broadcasted_iota(jnp.int32, (8, t1.shape[1]), 0)
        tiles = [t1[0:1] + t2[0:8], t1[0:1] + t2[8:16], t1[1:2] + t2[0:8]]
        for i in range(2, 8):
            tiles.append(jnp.where(r8 < P_TOPK // (i + 1), t1[i:i + 1] + t2[0:8], -jnp.inf))
        tiles.append(t1[8:16] + t2[0:1])
        best = _top_rows(jnp.concatenate(tiles, axis=0), P_TOPK)
        z = jnp.sum(jnp.exp(best - best[0:1]), axis=0, keepdims=True)
        c1_ref[0, hd] = jnp.exp(sc[0] - t1[0:1]) * (1.0 / z)
        st = jnp.where(r8 == 0, best[P_TOPK - 1:P_TOPK], 0.0)
        st_ref[0, hd] = jnp.where(r8 == 1, t2[0:1], st)


def _peer_route(x, g, scale, shift, wqt_bf16, keys_bf16):
    b, L, d = x.shape
    tm = min(L, PEER_TM)
    tok = lambda: pl.BlockSpec((1, tm, d), lambda bi, i: (bi, i, 0))
    rowv = pl.BlockSpec((1, 1, d), lambda bi, i: (bi, 0, 0))
    sspec = pl.BlockSpec((1, P_HEADS, N_KEYS, tm), lambda bi, i: (bi, 0, 0, i))
    return pl.pallas_call(
        _route_kernel,
        grid=(b, L // tm),
        in_specs=[tok(), pl.BlockSpec((1, d), lambda bi, i: (0, 0)), rowv, rowv,
                  pl.BlockSpec(wqt_bf16.shape, lambda bi, i: (0, 0)),
                  pl.BlockSpec(keys_bf16.shape, lambda bi, i: (0, 0, 0, 0))],
        out_specs=[tok(), sspec, sspec, sspec,
                   pl.BlockSpec((1, P_HEADS, STAT_ROWS, tm), lambda bi, i: (bi, 0, 0, i))],
        out_shape=[jax.ShapeDtypeStruct((b, L, d), BF16),
                   jax.ShapeDtypeStruct((b, P_HEADS, N_KEYS, L), F32),
                   jax.ShapeDtypeStruct((b, P_HEADS, N_KEYS, L), F32),
                   jax.ShapeDtypeStruct((b, P_HEADS, N_KEYS, L), F32),
                   jax.ShapeDtypeStruct((b, P_HEADS, STAT_ROWS, L), F32)],
        compiler_params=_cparams(("arbitrary", "arbitrary")),
        name="peer_route",
    )(x, g, scale, shift, wqt_bf16, keys_bf16)


def _peer_dense_kernel(h_ref, u_ref, vt_ref, s1_ref, c1_ref, s2_ref, st_ref, x_ref, ga_ref, o_ref,
                       acc_s, e2_s, ge_s, wa_s, wb_s, row_s):
    e = pl.program_id(2)
    last = pl.num_programs(2) - 1
    tm = h_ref.shape[1]
    n1 = PEER_TE // N_KEYS

    @pl.when(e == 0)
    def _():
        acc_s[...] = jnp.zeros_like(acc_s)
        wb_s[...] = jnp.zeros_like(wb_s)
        for hd in range(P_HEADS):
            e2_s[hd] = jnp.exp(s2_ref[0, hd] - st_ref[0, hd, 1:2, :])

    def step(w_prev, w_cur):
        nt = (((1,), (1,)), ((), ()))
        act = lax.dot_general(u_ref[...], h_ref[0], nt, preferred_element_type=F32)
        ge_s[...] = 0.5 * act * (1.0 + lax.erf(act * (1.0 / math.sqrt(2.0))))
        tile = jnp.minimum(e, last - 1)
        for i in range(n1):
            for hd in range(P_HEADS):
                row_s[0, i, hd] = s1_ref[0, hd, pl.ds(tile * n1 + i, 1), :]
                row_s[1, i, hd] = c1_ref[0, hd, pl.ds(tile * n1 + i, 1), :]
        nblk = (n1 // PEER_PAIR) * (tm // LANES)
        half = tm // 2
        blk = 0
        anchor = None
        for ip in range(0, n1, PEER_PAIR):
            pair = range(ip, ip + PEER_PAIR)
            for strip in range(tm // LANES):
                if blk % (nblk // 2) == 0:
                    ts = slice((blk // (nblk // 2)) * half, (blk // (nblk // 2) + 1) * half)
                    part = jnp.dot(vt_ref[...], w_prev[:, ts], preferred_element_type=F32)
                    acc_s[:, ts] += part
                    anchor = jnp.minimum(jnp.abs(part[0:1, 0:LANES]), 0.0)
                blk += 1
                cs = slice(strip * LANES, (strip + 1) * LANES)
                for sub in range(N_KEYS // PEER_SUB):
                    ks = slice(sub * PEER_SUB, (sub + 1) * PEER_SUB)
                    gsum = [None] * PEER_PAIR
                    for hd in range(P_HEADS):
                        s2t = s2_ref[0, hd, ks, cs]
                        e2t = e2_s[hd, ks, cs]
                        thr = st_ref[0, hd, 0:1, cs] + anchor
                        for q, i in enumerate(pair):
                            term = jnp.where((s2t + row_s[0, i, hd, :, cs]) >= thr,
                                             e2t * row_s[1, i, hd, :, cs], 0.0)
                            gsum[q] = term if gsum[q] is None else gsum[q] + term
                    for q, i in enumerate(pair):
                        ws = slice(i * N_KEYS + sub * PEER_SUB, i * N_KEYS + (sub + 1) * PEER_SUB)
                        w_cur[ws, cs] = (gsum[q] * ge_s[ws, cs]).astype(BF16)

    @pl.when(e % 2 == 0)
    def _():
        step(wb_s, wa_s)

    @pl.when(e % 2 == 1)
    def _():
        step(wa_s, wb_s)

    @pl.when(e == last)
    def _():
        o_ref[0] = x_ref[0] + ga_ref[0] * acc_s[...].T


def _peer_dense(h2, u_bf16, vt_bf16, s1, c1, s2, st, x, ga):
    b, L, d = x.shape
    tm = min(L, PEER_TM)
    tok = lambda: pl.BlockSpec((1, tm, d), lambda bi, i, e: (bi, i, 0))
    sspec = lambda: pl.BlockSpec((1, P_HEADS, N_KEYS, tm), lambda bi, i, e: (bi, 0, 0, i))
    ne = N_EXPERTS // PEER_TE
    return pl.pallas_call(
        _peer_dense_kernel,
        grid=(b, L // tm, ne + 1),
        in_specs=[tok(),
                  pl.BlockSpec((PEER_TE, d), lambda bi, i, e: (jnp.minimum(e, ne - 1), 0)),
                  pl.BlockSpec((d, PEER_TE), lambda bi, i, e: (0, jnp.maximum(e - 1, 0))),
                  sspec(), sspec(), sspec(),
                  pl.BlockSpec((1, P_HEADS, STAT_ROWS, tm), lambda bi, i, e: (bi, 0, 0, i)),
                  tok(),
                  pl.BlockSpec((1, 1, d), lambda bi, i, e: (bi, 0, 0))],
        out_specs=tok(),
        out_shape=jax.ShapeDtypeStruct((b, L, d), F32),
        scratch_shapes=[pltpu.VMEM((d, tm), F32), pltpu.VMEM((P_HEADS, N_KEYS, tm), F32),
                        pltpu.VMEM((PEER_TE, tm), F32), pltpu.VMEM((PEER_TE, tm), BF16),
                        pltpu.VMEM((PEER_TE, tm), BF16), pltpu.VMEM((2, PEER_TE // N_KEYS, P_HEADS, 1, tm), F32)],
        compiler_params=pltpu.CompilerParams(
            dimension_semantics=("arbitrary", "arbitrary", "arbitrary"),
            vmem_limit_bytes=56 * 1024 * 1024),
        name="peer_dense",
    )(h2, u_bf16, vt_bf16, s1, c1, s2, st, x, ga)


def _peer_block(x, g, scale, shift, ga, pw):
    h2, s1, c1, s2, st = _peer_route(x, g, scale, shift, pw["wqt"], pw["keys"])
    return _peer_dense(h2, pw["u"], pw["vt"], s1, c1, s2, st, x, ga)


def _final_norm_kernel(x_ref, g_ref, o_ref):
    x = x_ref[0]
    o_ref[0] = x * lax.rsqrt(jnp.mean(x * x, axis=-1, keepdims=True) + NORM_EPS) * g_ref[...]


def _final_norm(x, g):
    b, L, d = x.shape
    tm = 512
    tok = pl.BlockSpec((1, tm, d), lambda bi, i: (bi, i, 0))
    return pl.pallas_call(
        _final_norm_kernel,
        grid=(b, L // tm),
        in_specs=[tok, pl.BlockSpec((1, d), lambda bi, i: (0, 0))],
        out_specs=tok,
        out_shape=jax.ShapeDtypeStruct((b, L, d), F32),
        compiler_params=_cparams(("arbitrary", "arbitrary")),
        name="final_norm",
    )(x, g.reshape(1, d))


HY_N2 = 256
HY_CT = 1024
HY_S = 8
HY_OT = 512


def _hyena_filter(L, hy_p):
    w1, b1, w2, b2, w3, freq = hy_p
    t = jnp.linspace(0.0, 1.0, L, dtype=F32)[:, None]
    bands = jnp.linspace(1e-4, HY_BANDS - 1, HY_BANDS, dtype=F32)[None, :]
    ang = (2.0 * math.pi / L) * jnp.arange(L, dtype=F32)[:, None] * bands
    z = jnp.concatenate([t, jnp.cos(ang), -jnp.sin(ang)], axis=-1)
    hdn = jnp.sin(freq * (z @ w1 + b1))
    hdn = jnp.sin(freq * (hdn @ w2 + b2))
    pos = jnp.arange(L)
    posb = (L - pos) % L
    hdn2 = jnp.stack([hdn, hdn[posb]])
    t2 = jnp.stack([t, t[posb]])
    deltas = jnp.abs(jnp.linspace(math.log(HY_TARGET) / HY_SLOW_PCT, math.log(HY_TARGET) / HY_FAST_PCT,
                                  HY_D, dtype=F32))
    w3r = w3.reshape(HY_HIDDEN, HY_ORDER, 2, HY_D)
    first = (jnp.arange(L) == 0)[None, :, None]
    side = jnp.arange(2)[:, None, None]
    out = []
    for o in range(HY_ORDER):
        h = jnp.einsum("slh,hsc->slc", hdn2, w3r[:, o]) * jnp.exp(-t2 * deltas)
        h = h * lax.rsqrt(jnp.sum(h * h, axis=(0, 1), keepdims=True))
        g = h + jnp.where(first & (side == 0), h[1:2, 0:1, :], 0.0)
        g = jnp.where(first & (side == 1), 0.0, g)
        out.append(g.reshape(2 * L, HY_D))
    return out


def _cis(idx, n):
    ang = (2.0 * math.pi / n) * (idx % n).astype(F32)
    return jnp.cos(ang), -jnp.sin(ang)


def _block_complex(re, im):
    return jnp.concatenate([jnp.concatenate([re, -im], axis=-1),
                            jnp.concatenate([im, re], axis=-1)], axis=-2)


def _dft_mats(n1, n2, kin):
    k = jnp.arange(n1, dtype=jnp.int32)
    c1, s1 = _cis(k[:, None] * k[None, :], n1)
    half = max(n1 // 2, 1)
    eye = jnp.eye(HY_S, dtype=F32)
    outer = jnp.kron(_block_complex(c1[:, :half], s1[:, :half]), eye).astype(BF16)
    outer_real = jnp.kron(jnp.concatenate([c1, s1], axis=0), eye).astype(BF16)
    m2 = jnp.arange(n2, dtype=jnp.int32)
    freq = k[:, None, None] + n1 * m2[None, :, None]
    gr, gi = _cis(m2[None, None, :] * freq, n1 * n2)
    inner = _block_complex(gr[:, :, :kin], gi[:, :, :kin]).astype(BF16)
    inner_real = jnp.concatenate([gr, gi], axis=1).astype(BF16)
    return {"n1": n1, "outer": outer, "outer_real": outer_real, "inner": inner, "inner_real": inner_real}


def _hy_outer_fwd_kernel(x_ref, m_ref, o_ref):
    ct = x_ref.shape[-1]
    x = x_ref[0].reshape(-1, ct).astype(BF16)
    f = jnp.dot(m_ref[...], x, preferred_element_type=F32)
    o_ref[...] = f.reshape(o_ref.shape)


def _hy_outer_fwd(x, which, mat8):
    _, groups, r, n2, c = x.shape
    n1 = mat8.shape[0] // (2 * HY_S)
    ct = min(c, HY_OT)
    return pl.pallas_call(
        _hy_outer_fwd_kernel,
        grid=(n2 // HY_S, c // ct),
        in_specs=[pl.BlockSpec((1, groups, r, HY_S, ct), lambda q, j: (which, 0, 0, q, j)),
                  pl.BlockSpec(mat8.shape, lambda q, j: (0, 0))],
        out_specs=pl.BlockSpec((2, n1, HY_S, ct), lambda q, j: (0, 0, q, j)),
        out_shape=jax.ShapeDtypeStruct((2, n1, n2, c), F32),
        compiler_params=_cparams(("arbitrary", "arbitrary")),
        name="hy_outer_fwd",
    )(x, mat8)


def _hy_inner_kernel(*refs, conv, scale):
    if conv:
        a_ref, g_ref, s_ref, o_ref = refs
    else:
        a_ref, g_ref, o_ref = refs
    parts = [a_ref[p, 0] for p in range(a_ref.shape[0])]
    a = (jnp.concatenate(parts, axis=0) if len(parts) > 1 else parts[0]).astype(BF16)
    g = g_ref[0]
    f = jnp.dot(g, a, preferred_element_type=F32)
    ko = f.shape[0] // 2
    fr, fi = f[:ko], f[ko:]
    if not conv:
        o_ref[0, 0] = (fr * scale).astype(o_ref.dtype)
        o_ref[1, 0] = (fi * scale).astype(o_ref.dtype)
    else:
        sr, si = s_ref[0, 0].astype(F32), s_ref[1, 0].astype(F32)
        p = jnp.concatenate([fr * sr - fi * si, fr * si + fi * sr], axis=0).astype(BF16)
        d = lax.dot_general(g, p, (((0,), (0,)), ((), ())), preferred_element_type=F32)
        kin = d.shape[0] // 2
        o_ref[0, 0] = d[:kin].astype(o_ref.dtype)
        o_ref[1, 0] = d[kin:].astype(o_ref.dtype)


def _hy_inner(a, gmat, spec, out_dtype, scale=1.0):
    p, n1, kin, c = a.shape
    ko = gmat.shape[1] // 2
    conv = spec is not None
    ct = min(c, HY_CT)
    in_specs = [pl.BlockSpec((p, 1, kin, ct), lambda k, j: (0, k, 0, j)),
                pl.BlockSpec((1,) + gmat.shape[1:], lambda k, j: (k, 0, 0))]
    args = [a, gmat]
    if conv:
        in_specs.append(pl.BlockSpec((2, 1, ko, ct), lambda k, j: (0, k, 0, j)))
        args.append(spec)
    rows = kin if conv else ko
    return pl.pallas_call(
        functools.partial(_hy_inner_kernel, conv=conv, scale=scale),
        grid=(n1, c // ct),
        in_specs=in_specs,
        out_specs=pl.BlockSpec((2, 1, rows, ct), lambda k, j: (0, k, 0, j)),
        out_shape=jax.ShapeDtypeStruct((2, n1, rows, c), out_dtype),
        compiler_params=_cparams(("arbitrary", "arbitrary")),
        name="hy_inner_conv" if conv else "hy_inner_spec",
    )(*args)


def _hy_outer_inv_kernel(d_ref, m_ref, v_ref, x_ref, sk_ref, o_ref):
    ct = d_ref.shape[-1]
    d = d_ref[...].reshape(-1, ct).astype(BF16)
    y = lax.dot_general(m_ref[...], d, (((0,), (0,)), ((), ())), preferred_element_type=F32)
    y = y.reshape(o_ref.shape[1:])
    o_ref[0] = x_ref[0] * (y + v_ref[0] * sk_ref[...])


def _hy_outer_inv(d, mat8, src, si, gate, gi, skip_row):
    _, n1, n2, c = d.shape
    half = n1 // 2
    ct = min(c, HY_OT)
    blk = lambda which: pl.BlockSpec((1, 2, half, HY_S, ct), lambda q, j: (which, 0, 0, q, j))
    return pl.pallas_call(
        _hy_outer_inv_kernel,
        grid=(n2 // HY_S, c // ct),
        in_specs=[pl.BlockSpec((2, n1, HY_S, ct), lambda q, j: (0, 0, q, j)),
                  pl.BlockSpec(mat8.shape, lambda q, j: (0, 0)),
                  blk(si), blk(gi),
                  pl.BlockSpec((1, ct), lambda q, j: (0, j))],
        out_specs=blk(0),
        out_shape=jax.ShapeDtypeStruct((1, 2, half, n2, c), F32),
        compiler_params=_cparams(("arbitrary", "arbitrary")),
        name="hy_outer_inv",
    )(d, mat8, src, gate, skip_row)


def _hyena_spectrum(g, mats):
    n, c = g[0].shape
    n1 = mats["n1"]
    n2 = n // n1
    out = []
    for go in g:
        if n1 > 1:
            a = _hy_outer_fwd(go.reshape(1, 1, n1, n2, c), 0, mats["outer_real"])
            out.append(_hy_inner(a, mats["inner"], None, BF16, 1.0 / n))
        else:
            out.append(_hy_inner(go.reshape(1, 1, n, c), mats["inner_real"], None, BF16, 1.0 / n))
    return out


def _hy_gate_kernel(y_ref, v_ref, x_ref, sk_ref, o_ref):
    o_ref[...] = x_ref[...] * (y_ref[...] + v_ref[...] * sk_ref[...])


def _hy_gate(y, v, x, skip_row):
    b, L, c = y.shape
    tok = pl.BlockSpec((1, L, c), lambda bi: (bi, 0, 0))
    return pl.pallas_call(
        _hy_gate_kernel,
        grid=(b,),
        in_specs=[tok, tok, tok, pl.BlockSpec((1, c), lambda bi: (0, 0))],
        out_specs=tok,
        out_shape=jax.ShapeDtypeStruct((b, L, c), F32),
        compiler_params=_cparams(("arbitrary",)),
        name="hy_gate",
    )(y, v, x, skip_row)


def _hyena_mixer(hy, spec, skip, mats):
    _, b, L, c = hy.shape
    assert b == 2, "the two batch entries are packed as one complex signal"
    n1 = mats["n1"]
    if n1 == 1:
        z = hy[0]
        for o in range(HY_ORDER):
            y = _hy_inner(z.reshape(2, 1, L, c), mats["inner"], spec[o], F32).reshape(b, L, c)
            z = _hy_gate(y, z, hy[o + 1], skip[o].reshape(1, c))
        return z
    n2 = 2 * L // n1
    src, si = hy.reshape(3, b, n1 // 2, n2, c), 0
    gate = src
    for o in range(HY_ORDER):
        a = _hy_outer_fwd(src, si, mats["outer"])
        d = _hy_inner(a, mats["inner"], spec[o], F32)
        src, si = _hy_outer_inv(d, mats["outer"], src, si, gate, o + 1, skip[o].reshape(1, c)), 0
    return src.reshape(b, L, c)


def _token_mixing(x, g, scale, shift, ga, lw, n_seg, s0_f, s0_b, hy_p):
    b, L, _ = x.shape
    seg = L // n_seg
    proj = functools.partial(_in_proj, x, g, scale, shift, lw["w_in"], lw["conv_w"], seg)
    y_rw, s_f, s_b = _rwkv_mixer(proj("rw"), lw["rw"], s0_f, s0_b)
    if hy_p is None:
        return None, s_f, s_b
    n = 2 * L
    n1 = n // HY_N2 if n // HY_N2 >= 16 else 1
    mats = _dft_mats(n1, n // n1, L if n1 == 1 else n // n1)
    spec = _hyena_spectrum(_hyena_filter(L, hy_p), mats)
    y_hy = _hyena_mixer(proj("hy"), spec, lw["hy_skip"], mats)
    return _out_proj(y_rw, y_hy, proj("gate"), x, ga, lw["w_out"]), s_f, s_b


def kernel(x, c, ctx, c_ctx, ada_w, ada_b, norm1_g, norm2_g, w_in, conv_w, rw_w0, rw_w2, rw_a0, rw_a2, rw_g2, rw_k_k, rw_k_a, rw_r_k, rw_ln_w, rw_ln_b, hy_w1, hy_b1, hy_w2, hy_b2, hy_w3, hy_freq, hy_skip, w_out, peer_wq, peer_keys, peer_u, peer_v, final_g):
    b, L, d = x.shape
    depth = ada_w.shape[0]
    ctx_len = ctx.shape[1]
    s_zero = jnp.zeros((b, N_GROUPS, GROUP_W, GROUP_W), F32)
    cond = jnp.concatenate([c, c_ctx[None, :]], axis=0)
    cond = jnp.pad(jax.nn.silu(cond), ((0, 8 - (b + 1)), (0, 0)))
    for l in range(depth):
        last = l == depth - 1
        mod = _matmul(cond, ada_w[l], 1024) + ada_b[l]
        mx = [m[:, None, :] for m in jnp.split(mod[:b], 6, axis=-1)]
        mc = [jnp.broadcast_to(m[None, :, :], (b, 1, d)) for m in jnp.split(mod[b:b + 1], 6, axis=-1)]
        lw = {
            "w_in": _pack_cols(w_in[l]).astype(BF16),
            "conv_w": _pack_cols(conv_w[l]),
            "rw": _rwkv_params(l, rw_w0, rw_w2, rw_a0, rw_a2, rw_g2, rw_k_k, rw_k_a, rw_r_k, rw_ln_w, rw_ln_b),
            "hy_skip": hy_skip[l],
            "w_out": w_out[l].astype(BF16),
        }
        hy_p = (hy_w1[l], hy_b1[l], hy_w2[l], hy_b2[l], hy_w3[l], hy_freq[l])
        pw = {
            "wqt": peer_wq[l].T.astype(BF16),
            "keys": peer_keys[l].astype(BF16),
            "u": peer_u[l].astype(BF16),
            "vt": peer_v[l].T.astype(BF16),
        }
        g1 = norm1_g[l].reshape(1, d)
        g2 = norm2_g[l].reshape(1, d)

        ctx_new, s_f, s_b = _token_mixing(ctx, g1, mc[1], mc[0], mc[2], lw, 1, s_zero, s_zero,
                                          None if last else hy_p)
        if not last:
            ctx = _peer_block(ctx_new, g2, mc[4], mc[3], mc[5], pw)

        x, _, _ = _token_mixing(x, g1, mx[1], mx[0], mx[2], lw, L // GRID_W, s_f, s_b, hy_p)
        x = _peer_block(x, g2, mx[4], mx[3], mx[5], pw)
    return _final_norm(x, final_g)
```

```python
import functools
import math

import numpy as np
import jax
import jax.numpy as jnp
from jax import lax
from jax.experimental import pallas as pl
from jax.experimental.pallas import tpu as pltpu

F32 = jnp.float32
BF16 = jnp.bfloat16

D_MODEL = 2048
GRID_W = 64
NORM_EPS = 1e-6
RW_N = 64
RW_H = D_MODEL // RW_N
RW_D = D_MODEL
W_LORA = 96
A_LORA = 96
G_LORA = 64
RW_GN_EPS = 64e-5
HY_D = D_MODEL
HY_ORDER = 2
HY_EMB = 33
HY_BANDS = (HY_EMB - 1) // 2
HY_HIDDEN = 64
HY_TARGET = 1e-2
HY_FAST_PCT = 0.3
HY_SLOW_PCT = 1.5
P_HEADS = 8
N_KEYS = 128
N_EXPERTS = N_KEYS * N_KEYS
P_TOPK = 16
D_KEY = 256

LANES = 128
MXU_DIM = 256
VMEM_LIMIT = 48 * 1024 * 1024

COL_K, COL_V, COL_R = 0, RW_D, 2 * RW_D
COL_SMALL = 3 * RW_D
SMALL_W = 1024
COL_GATE = COL_SMALL + SMALL_W
COL_HY = COL_GATE + 2 * D_MODEL
N_PACKED = COL_HY + 3 * HY_D
PROJ_TN = 512
PROJ_RC = 256

CHUNK = 64
HEADS_PER_GROUP = MXU_DIM // RW_N
GROUP_W = HEADS_PER_GROUP * RW_N
N_GROUPS = RW_D // GROUP_W
STACK = HEADS_PER_GROUP * CHUNK
GROUPS_PER_STEP = 8
STEP_W = GROUPS_PER_STEP * GROUP_W
RWKV_TB = 128


def _cparams(sem):
    return pltpu.CompilerParams(dimension_semantics=sem, vmem_limit_bytes=VMEM_LIMIT)


def _rwkv_consts(reverse):
    t = np.arange(CHUNK)
    tt, ss = t[:, None], t[None, :]
    strict = (tt < ss) if reverse else (tt > ss)
    eye = tt == ss
    levels = []
    m = 1
    while m < CHUNK:
        levels.append(strict & ((tt // (2 * m)) == (ss // (2 * m))) & ((tt // m) != (ss // m)))
        m *= 2
    lanes = lambda a: np.tile(a, (1,) * (a.ndim - 1) + (HEADS_PER_GROUP,))
    hb = np.arange(STACK) // CHUNK
    same = hb[:, None] == hb[None, :]
    tri = (tt <= ss) if reverse else (tt >= ss)
    f = lambda a: jnp.asarray(a.astype(np.float32))
    return (f(lanes(strict)), f(lanes(strict | eye)), f(same), f(lanes(eye)),
            f(lanes(np.stack(levels))), f(tri))


def _rwkv_kernel(k_ref, v_ref, r_ref, wd_ref, ad_ref, gd_ref, yf_ref,
                 w0_ref, w2_ref, a0_ref, a2_ref, g2_ref, kk_ref, ka_ref, rk_ref, lnw_ref, lnb_ref,
                 ms_ref, mi_ref, same_ref, eye_ref, lev_ref, tri_ref, s0_ref,
                 y_ref, sout_ref,
                 S_ref, lw_s, kt_s, kn_s, b_s, yacc_s, *, reverse, final, nchunk):
    step = pl.program_id(2)

    @pl.when(step == 0)
    def _():
        S_ref[...] = s0_ref[0]

    same = same_ref[...]
    gsl = [slice(g * GROUP_W, (g + 1) * GROUP_W) for g in range(GROUPS_PER_STEP)]

    same_b = same.astype(BF16)

    def split(x, pieces):
        out = []
        for _ in range(pieces):
            p = x.astype(BF16)
            out.append(p)
            x = x - p.astype(F32)
        return out

    def head_sum(x):
        cols = []
        for s in gsl:
            hi, lo = split(x[:, s], 2)
            cols.append(jnp.dot(hi, same_b, preferred_element_type=F32)
                        + jnp.dot(lo, same_b, preferred_element_type=F32))
        return jnp.concatenate(cols, axis=1)

    k = k_ref[0]
    wraw = w0_ref[...] + jnp.dot(jnp.tanh(wd_ref[0]).astype(BF16), w2_ref[...],
                                 preferred_element_type=F32)
    lw_s[...] = -jax.nn.sigmoid(wraw) * math.exp(-0.5)
    a = jax.nn.sigmoid(a0_ref[...] + jnp.dot(ad_ref[0].astype(BF16), a2_ref[...],
                                             preferred_element_type=F32))
    kt_s[...] = k * (1.0 + (a - 1.0) * ka_ref[...])
    kn = k * kk_ref[...]
    kn = kn * lax.rsqrt(jnp.maximum(head_sum(kn * kn), 1e-24))
    kn_s[...] = kn
    b_s[...] = kn * a

    ms = ms_ref[...]
    mi = mi_ref[...]
    tri = tri_ref[...].astype(BF16)
    mid = CHUNK // 2 if reverse else CHUNK // 2 - 1
    last = 0 if reverse else CHUNK - 1

    def bd(x):
        return jnp.concatenate([x.astype(BF16)] * HEADS_PER_GROUP, axis=0) * same_b

    def mm(a, w):
        return jnp.dot(a.astype(BF16), w, preferred_element_type=F32)

    nt = (((1,), (1,)), ((), ()))
    tn = (((0,), (0,)), ((), ()))

    def chunk_group(rows, g):
        cols = gsl[g]
        lw = lw_s[rows, cols]
        kt = kt_s[rows, cols]
        kn = kn_s[rows, cols]
        b = b_s[rows, cols]
        r = r_ref[0, rows, cols]
        v = v_ref[0, rows, cols]
        c = sum(jnp.dot(tri, p, preferred_element_type=F32) for p in split(lw, 3))
        yield
        cp = c - lw
        rho = c[mid:mid + 1, :]
        cend = c[last:last + 1, :]
        einv = jnp.exp(rho - c)
        eend = jnp.exp(cend - c)
        lhs = jnp.concatenate([kn * jnp.exp(cp - rho), r * jnp.exp(c - rho)], axis=0).astype(BF16)
        rhs = jnp.concatenate([bd(kt * einv), bd(b * einv)], axis=0)
        ag = lax.dot_general(lhs, rhs, nt, preferred_element_type=F32)
        yield
        ak = ag[:CHUNK, :STACK] * ms
        ab = ag[:CHUNK, STACK:] * ms
        gk = ag[CHUNK:, :STACK] * mi
        gb = ag[CHUNK:, STACK:] * mi
        tm = eye_ref[...] - ab * lev_ref[0]
        for lv in range(1, lev_ref.shape[0]):
            te = mm(tm, bd(ab * lev_ref[lv]))
            yield
            tm = tm - mm(te, bd(tm))
            yield
        S = S_ref[g]
        lhs_s = jnp.concatenate([kn * jnp.exp(cp), r * jnp.exp(c)], axis=0).astype(BF16)
        qs = lax.dot_general(lhs_s, S.astype(BF16), nt, preferred_element_type=F32)
        vs = bd(v)
        rhs_u = qs[:CHUNK] + mm(ak, vs)
        yield
        us = mm(tm, bd(rhs_u))
        yield
        y = qs[CHUNK:] + mm(jnp.concatenate([gk, gb], axis=1), jnp.concatenate([vs, bd(-us)], axis=0))
        yield
        if final:
            yacc_s[rows, cols] = y
        else:
            y_ref[0, rows, cols] = y
        vu = jnp.concatenate([v, -us], axis=0).astype(BF16)
        kb = jnp.concatenate([kt * eend, b * eend], axis=0).astype(BF16)
        ds = lax.dot_general(vu, kb, tn, preferred_element_type=F32)
        S_ref[g] = S * jnp.exp(cend) + ds * same

    def chunk(ci, carry):
        c_idx = (nchunk - 1 - ci) if reverse else ci
        rows = pl.ds(pl.multiple_of(c_idx * CHUNK, CHUNK), CHUNK)
        pending = [chunk_group(rows, g) for g in range(GROUPS_PER_STEP)]
        while pending:
            pending = [gen for gen in pending if next(gen, True) is None]
        return carry

    lax.fori_loop(0, nchunk, chunk, 0)
    sout_ref[0] = S_ref[...]

    if final:
        y = yf_ref[0] + yacc_s[...]
        inv_n = 1.0 / RW_N
        mu = head_sum(y) * inv_n
        d = y - mu
        var = head_sum(d * d) * inv_n
        yn = d * lax.rsqrt(var + RW_GN_EPS) * lnw_ref[...] + lnb_ref[...]
        bonus = head_sum(r_ref[0] * k * rk_ref[...]) * v_ref[0]
        g = jnp.dot(jax.nn.sigmoid(gd_ref[0]).astype(BF16), g2_ref[...], preferred_element_type=F32)
        y_ref[0] = (yn + bonus) * g


def _pad_rows(w, rows):
    return jnp.pad(w, ((0, rows - w.shape[0]), (0, 0)))


def _rwkv_direction(mix, yf, prm, s0, d, *, final):
    b, L, _ = mix.shape
    reverse = d == 1
    tb = min(L, RWKV_TB)
    nb = L // tb
    nchunk = tb // CHUNK
    ms, mi, same, eye, lev, tri = _rwkv_consts(reverse)
    blk = (lambda i: nb - 1 - i) if reverse else (lambda i: i)
    sm = COL_SMALL // LANES

    def col(c0):
        return pl.BlockSpec((1, tb, STEP_W), lambda bi, g, i: (bi, blk(i), c0 // STEP_W + g))

    def small(j):
        return pl.BlockSpec((1, tb, LANES), lambda bi, g, i: (bi, blk(i), sm + j))

    vec = pl.BlockSpec((1, STEP_W), lambda bi, g, i: (0, g))
    lora = pl.BlockSpec((LANES, STEP_W), lambda bi, g, i: (0, g))
    const2 = lambda shp: pl.BlockSpec(shp, lambda bi, g, i: (0,) * len(shp))
    state = pl.BlockSpec((1, GROUPS_PER_STEP, GROUP_W, GROUP_W), lambda bi, g, i: (bi, g, 0, 0))
    yspec = pl.BlockSpec((1, tb, STEP_W), lambda bi, g, i: (bi, blk(i), g))

    kern = functools.partial(_rwkv_kernel, reverse=reverse, final=final, nchunk=nchunk)
    if yf is None:
        yf, yf_spec = mix, col(COL_K)
    else:
        yf_spec = yspec
    y, s_fin = pl.pallas_call(
        kern,
        grid=(b, N_GROUPS // GROUPS_PER_STEP, nb),
        in_specs=[col(COL_K), col(COL_V), col(COL_R), small(d), small(2 + d), small(4), yf_spec,
                  vec, lora, vec, lora, lora, vec, vec, vec, vec, vec,
                  const2((CHUNK, STACK)), const2((CHUNK, STACK)), const2((STACK, STACK)),
                  const2((CHUNK, STACK)), const2(tuple(lev.shape)), const2((CHUNK, CHUNK)), state],
        out_specs=[yspec, state],
        out_shape=[jax.ShapeDtypeStruct((b, L, RW_D), F32),
                   jax.ShapeDtypeStruct((b, N_GROUPS, GROUP_W, GROUP_W), F32)],
        scratch_shapes=[pltpu.VMEM((GROUPS_PER_STEP, GROUP_W, GROUP_W), F32)]
        + [pltpu.VMEM((tb, STEP_W), F32)] * 5,
        compiler_params=_cparams(("arbitrary", "arbitrary", "arbitrary")),
        name="rwkv_bwd" if reverse else "rwkv_fwd",
    )(mix, mix, mix, mix, mix, mix, yf,
      prm["w0"][d], prm["w2"][d], prm["a0"][d], prm["a2"][d], prm["g2"], prm["k_k"], prm["k_a"],
      prm["r_k"], prm["ln_w"], prm["ln_b"],
      ms, mi, same, eye, lev, tri, s0)
    return y, s_fin


def _rwkv_params(l, rw_w0, rw_w2, rw_a0, rw_a2, rw_g2, rw_k_k, rw_k_a, rw_r_k, rw_ln_w, rw_ln_b):
    row = lambda v: v.reshape(1, RW_D)
    return {
        "w0": [row(rw_w0[l, d]) for d in range(2)],
        "w2": [_pad_rows(rw_w2[l, d], LANES).astype(BF16) for d in range(2)],
        "a0": [row(rw_a0[l, d]) for d in range(2)],
        "a2": [_pad_rows(rw_a2[l, d], LANES).astype(BF16) for d in range(2)],
        "g2": _pad_rows(rw_g2[l], LANES).astype(BF16),
        "k_k": row(rw_k_k[l]), "k_a": row(rw_k_a[l]), "r_k": row(rw_r_k[l]),
        "ln_w": row(rw_ln_w[l]), "ln_b": row(rw_ln_b[l]),
    }


def _rwkv_mixer(mix, prm, s0_f, s0_b):
    b, L, _ = mix.shape
    y_f, s_f = _rwkv_direction(mix, None, prm, s0_f, 0, final=False)
    y, s_b = _rwkv_direction(mix, y_f, prm, s0_b, 1, final=True)
    return y, s_f, s_b


def _matmul_kernel(a_ref, b_ref, o_ref):
    o_ref[...] = jnp.dot(a_ref[...].astype(BF16), b_ref[...].astype(BF16), preferred_element_type=F32)


def _matmul(a, b, tn):
    m, k = a.shape
    n = b.shape[1]
    return pl.pallas_call(
        _matmul_kernel,
        grid=(n // tn,),
        in_specs=[pl.BlockSpec((m, k), lambda j: (0, 0)), pl.BlockSpec((k, tn), lambda j: (0, j))],
        out_specs=pl.BlockSpec((m, tn), lambda j: (0, j)),
        out_shape=jax.ShapeDtypeStruct((m, n), F32),
        compiler_params=_cparams(("arbitrary",)),
        name="matmul",
    )(a, b)


def _modulate(x, g, scale, shift):
    y = x * lax.rsqrt(jnp.mean(x * x, axis=-1, keepdims=True) + NORM_EPS)
    return y * g * (1.0 + scale) + shift


def _pack_cols(m):
    z = lambda n: jnp.zeros(m.shape[:-1] + (n,), m.dtype)
    s = lambda a, b: m[..., a:b]
    o = 2 * RW_D
    lo = [s(o + j * W_LORA, o + (j + 1) * W_LORA) for j in range(4)]
    r0 = o + 2 * W_LORA + 2 * A_LORA
    g0 = r0 + RW_D
    h0 = g0 + G_LORA
    parts = [s(0, RW_D), s(RW_D, 2 * RW_D), s(r0, r0 + RW_D)]
    for p in lo:
        parts += [p, z(LANES - W_LORA)]
    parts += [s(g0, g0 + G_LORA), z(LANES - G_LORA), z(SMALL_W - 5 * LANES)]
    if m.shape[-1] > h0 + 3 * HY_D:
        parts.append(s(h0 + 3 * HY_D, h0 + 3 * HY_D + 2 * D_MODEL))
    else:
        parts.append(z(2 * D_MODEL))
    parts.append(s(h0, h0 + 3 * HY_D))
    return jnp.concatenate(parts, axis=-1)


def _inproj_kernel(x_ref, g_ref, sc_ref, sh_ref, w_ref, cw_ref, o_ref, h_s, *, seg, gate):
    j = pl.program_id(2)

    @pl.when(j == 0)
    def _():
        h_s[...] = _modulate(x_ref[0], g_ref[...], sc_ref[0], sh_ref[0]).astype(BF16)

    out_blk = (0,) * (len(o_ref.shape) - 2)
    tm = h_s.shape[0]
    rc = min(tm, max(seg, PROJ_RC))
    for r0 in range(0, tm, rc):
        p = jnp.dot(h_s[r0:r0 + rc, :], w_ref[...], preferred_element_type=F32)
        if gate:
            res = jax.nn.sigmoid(p)
        else:
            row = lax.broadcasted_iota(jnp.int32, p.shape, 0) % seg
            prev = jnp.where(row == 0, 0.0, pltpu.roll(p, 1, 0))
            nxt = jnp.where(row == seg - 1, 0.0, pltpu.roll(p, rc - 1, 0))
            res = cw_ref[0:1, :] * prev + cw_ref[1:2, :] * p + cw_ref[2:3, :] * nxt
        o_ref[out_blk + (slice(r0, r0 + rc), slice(None))] = res


def _in_proj(x, g, scale, shift, w_packed, cw_packed, seg, part):
    b, L, d = x.shape
    tm = min(L, 1024)
    c0, c1 = {"rw": (0, COL_GATE), "gate": (COL_GATE, COL_HY), "hy": (COL_HY, N_PACKED)}[part]
    t0, nt = c0 // PROJ_TN, (c1 - c0) // PROJ_TN
    if part == "hy":
        per = HY_D // PROJ_TN
        out_spec = pl.BlockSpec((1, 1, tm, PROJ_TN), lambda bi, i, j: (j // per, bi, i, j % per))
        out_shape = jax.ShapeDtypeStruct((3, b, L, HY_D), F32)
    else:
        out_spec = pl.BlockSpec((1, tm, PROJ_TN), lambda bi, i, j: (bi, i, j))
        out_shape = jax.ShapeDtypeStruct((b, L, c1 - c0), F32)
    return pl.pallas_call(
        functools.partial(_inproj_kernel, seg=seg, gate=part == "gate"),
        grid=(b, L // tm, nt),
        in_specs=[pl.BlockSpec((1, tm, d), lambda bi, i, j: (bi, i, 0)),
                  pl.BlockSpec((1, d), lambda bi, i, j: (0, 0)),
                  pl.BlockSpec((1, 1, d), lambda bi, i, j: (bi, 0, 0)),
                  pl.BlockSpec((1, 1, d), lambda bi, i, j: (bi, 0, 0)),
                  pl.BlockSpec((d, PROJ_TN), lambda bi, i, j: (0, t0 + j)),
                  pl.BlockSpec((3, PROJ_TN), lambda bi, i, j: (0, t0 + j))],
        out_specs=out_spec,
        out_shape=out_shape,
        scratch_shapes=[pltpu.VMEM((tm, d), BF16)],
        compiler_params=_cparams(("arbitrary", "arbitrary", "arbitrary")),
        name="in_proj_" + part,
    )(x, g, scale, shift, w_packed, cw_packed)


def _outproj_kernel(yrw_ref, yhy_ref, grw_ref, ghy_ref, x_ref, ga_ref, w_ref, o_ref):
    y = grw_ref[0] * yrw_ref[0] + ghy_ref[0] * yhy_ref[0]
    o_ref[0] = x_ref[0] + ga_ref[0] * jnp.dot(y.astype(BF16), w_ref[...], preferred_element_type=F32)


def _out_proj(y_rw, y_hy, gates, x, ga, w_out_bf16):
    b, L, d = x.shape
    tm = 256
    tok = pl.BlockSpec((1, tm, d), lambda bi, i: (bi, i, 0))
    return pl.pallas_call(
        _outproj_kernel,
        grid=(b, L // tm),
        in_specs=[tok, tok,
                  tok,
                  pl.BlockSpec((1, tm, d), lambda bi, i: (bi, i, 1)),
                  tok,
                  pl.BlockSpec((1, 1, d), lambda bi, i: (bi, 0, 0)),
                  pl.BlockSpec((d, d), lambda bi, i: (0, 0))],
        out_specs=tok,
        out_shape=jax.ShapeDtypeStruct((b, L, d), F32),
        compiler_params=_cparams(("arbitrary", "arbitrary")),
        name="out_proj",
    )(y_rw, y_hy, gates, gates, x, ga, w_out_bf16)


PEER_TM = 512
PEER_TE = 512
PEER_SUB = 64
PEER_PAIR = 2
STAT_ROWS = 8


def _top_rows(s, n):
    rows = lax.broadcasted_iota(jnp.int32, (n, s.shape[1]), 0)
    out = jnp.zeros((n, s.shape[1]), F32)
    cur = s
    for i in range(n):
        m = jnp.max(cur, axis=0, keepdims=True)
        out = jnp.where(rows == i, m, out)
        cur = jnp.where(cur == m, -jnp.inf, cur)
    return out


def _route_kernel(x_ref, g_ref, sc_ref, sh_ref, wqt_ref, keys_ref, h_ref, s1_ref, c1_ref, s2_ref, st_ref):
    hb = _modulate(x_ref[0], g_ref[...], sc_ref[0], sh_ref[0]).astype(BF16)
    h_ref[0] = hb
    nt = (((1,), (1,)), ((), ()))
    qt = lax.dot_general(wqt_ref[...], hb, nt, preferred_element_type=F32)
    half = D_KEY // 2
    for hd in range(P_HEADS):
        sc = []
        for p in range(2):
            r0 = (hd * 2 + p) * half
            sc.append(jnp.dot(keys_ref[hd, p], qt[r0:r0 + half, :].astype(BF16),
                              preferred_element_type=F32))
        s1_ref[0, hd] = sc[0]
        s2_ref[0, hd] = sc[1]
        t1 = _top_rows(sc[0], P_TOPK)
        t2 = _top_rows(sc[1], P_TOPK)
        r8 = lax.broadcasted_iota(jnp.int32, (8, t1.shape[1]), 0)
        tiles = [t1[0:1] + t2[0:8], t1[0:1] + t2[8:16], t1[1:2] + t2[0:8]]
        for i in range(2, 8):
            tiles.append(jnp.where(r8 < P_TOPK // (i + 1), t1[i:i + 1] + t2[0:8], -jnp.inf))
        tiles.append(t1[8:16] + t2[0:1])
        best = _top_rows(jnp.concatenate(tiles, axis=0), P_TOPK)
        z = jnp.sum(jnp.exp(best - best[0:1]), axis=0, keepdims=True)
        c1_ref[0, hd] = jnp.exp(sc[0] - t1[0:1]) * (1.0 / z)
        st = jnp.where(r8 == 0, best[P_TOPK - 1:P_TOPK], 0.0)
        st_ref[0, hd] = jnp.where(r8 == 1, t2[0:1], st)


def _peer_route(x, g, scale, shift, wqt_bf16, keys_bf16):
    b, L, d = x.shape
    tm = min(L, PEER_TM)
    tok = lambda: pl.BlockSpec((1, tm, d), lambda bi, i: (bi, i, 0))
    rowv = pl.BlockSpec((1, 1, d), lambda bi, i: (bi, 0, 0))
    sspec = pl.BlockSpec((1, P_HEADS, N_KEYS, tm), lambda bi, i: (bi, 0, 0, i))
    return pl.pallas_call(
        _route_kernel,
        grid=(b, L // tm),
        in_specs=[tok(), pl.BlockSpec((1, d), lambda bi, i: (0, 0)), rowv, rowv,
                  pl.BlockSpec(wqt_bf16.shape, lambda bi, i: (0, 0)),
                  pl.BlockSpec(keys_bf16.shape, lambda bi, i: (0, 0, 0, 0))],
        out_specs=[tok(), sspec, sspec, sspec,
                   pl.BlockSpec((1, P_HEADS, STAT_ROWS, tm), lambda bi, i: (bi, 0, 0, i))],
        out_shape=[jax.ShapeDtypeStruct((b, L, d), BF16),
                   jax.ShapeDtypeStruct((b, P_HEADS, N_KEYS, L), F32),
                   jax.ShapeDtypeStruct((b, P_HEADS, N_KEYS, L), F32),
                   jax.ShapeDtypeStruct((b, P_HEADS, N_KEYS, L), F32),
                   jax.ShapeDtypeStruct((b, P_HEADS, STAT_ROWS, L), F32)],
        compiler_params=_cparams(("arbitrary", "arbitrary")),
        name="peer_route",
    )(x, g, scale, shift, wqt_bf16, keys_bf16)


def _peer_dense_kernel(h_ref, u_ref, vt_ref, s1_ref, c1_ref, s2_ref, st_ref, x_ref, ga_ref, o_ref,
                       acc_s, e2_s, ge_s, wa_s, wb_s, row_s):
    e = pl.program_id(2)
    last = pl.num_programs(2) - 1
    tm = h_ref.shape[1]
    n1 = PEER_TE // N_KEYS

    @pl.when(e == 0)
    def _():
        acc_s[...] = jnp.zeros_like(acc_s)
        wb_s[...] = jnp.zeros_like(wb_s)
        for hd in range(P_HEADS):
            e2_s[hd] = jnp.exp(s2_ref[0, hd] - st_ref[0, hd, 1:2, :])

    def step(w_prev, w_cur):
        nt = (((1,), (1,)), ((), ()))
        act = lax.dot_general(u_ref[...], h_ref[0], nt, preferred_element_type=F32)
        ge_s[...] = 0.5 * act * (1.0 + lax.erf(act * (1.0 / math.sqrt(2.0))))
        tile = jnp.minimum(e, last - 1)
        for i in range(n1):
            for hd in range(P_HEADS):
                row_s[0, i, hd] = s1_ref[0, hd, pl.ds(tile * n1 + i, 1), :]
                row_s[1, i, hd] = c1_ref[0, hd, pl.ds(tile * n1 + i, 1), :]
        nblk = (n1 // PEER_PAIR) * (tm // LANES)
        half = tm // 2
        blk = 0
        anchor = None
        for ip in range(0, n1, PEER_PAIR):
            pair = range(ip, ip + PEER_PAIR)
            for strip in range(tm // LANES):
                if blk % (nblk // 2) == 0:
                    ts = slice((blk // (nblk // 2)) * half, (blk // (nblk // 2) + 1) * half)
                    part = jnp.dot(vt_ref[...], w_prev[:, ts], preferred_element_type=F32)
                    acc_s[:, ts] += part
                    anchor = jnp.minimum(jnp.abs(part[0:1, 0:LANES]), 0.0)
                blk += 1
                cs = slice(strip * LANES, (strip + 1) * LANES)
                for sub in range(N_KEYS // PEER_SUB):
                    ks = slice(sub * PEER_SUB, (sub + 1) * PEER_SUB)
                    gsum = [None] * PEER_PAIR
                    for hd in range(P_HEADS):
                        s2t = s2_ref[0, hd, ks, cs]
                        e2t = e2_s[hd, ks, cs]
                        thr = st_ref[0, hd, 0:1, cs] + anchor
                        for q, i in enumerate(pair):
                            term = jnp.where((s2t + row_s[0, i, hd, :, cs]) >= thr,
                                             e2t * row_s[1, i, hd, :, cs], 0.0)
                            gsum[q] = term if gsum[q] is None else gsum[q] + term
                    for q, i in enumerate(pair):
                        ws = slice(i * N_KEYS + sub * PEER_SUB, i * N_KEYS + (sub + 1) * PEER_SUB)
                        w_cur[ws, cs] = (gsum[q] * ge_s[ws, cs]).astype(BF16)

    @pl.when(e % 2 == 0)
    def _():
        step(wb_s, wa_s)

    @pl.when(e % 2 == 1)
    def _():
        step(wa_s, wb_s)

    @pl.when(e == last)
    def _():
        o_ref[0] = x_ref[0] + ga_ref[0] * acc_s[...].T


def _peer_dense(h2, u_bf16, vt_bf16, s1, c1, s2, st, x, ga):
    b, L, d = x.shape
    tm = min(L, PEER_TM)
    tok = lambda: pl.BlockSpec((1, tm, d), lambda bi, i, e: (bi, i, 0))
    sspec = lambda: pl.BlockSpec((1, P_HEADS, N_KEYS, tm), lambda bi, i, e: (bi, 0, 0, i))
    ne = N_EXPERTS // PEER_TE
    return pl.pallas_call(
        _peer_dense_kernel,
        grid=(b, L // tm, ne + 1),
        in_specs=[tok(),
                  pl.BlockSpec((PEER_TE, d), lambda bi, i, e: (jnp.minimum(e, ne - 1), 0)),
                  pl.BlockSpec((d, PEER_TE), lambda bi, i, e: (0, jnp.maximum(e - 1, 0))),
                  sspec(), sspec(), sspec(),
                  pl.BlockSpec((1, P_HEADS, STAT_ROWS, tm), lambda bi, i, e: (bi, 0, 0, i)),
                  tok(),
                  pl.BlockSpec((1, 1, d), lambda bi, i, e: (bi, 0, 0))],
        out_specs=tok(),
        out_shape=jax.ShapeDtypeStruct((b, L, d), F32),
        scratch_shapes=[pltpu.VMEM((d, tm), F32), pltpu.VMEM((P_HEADS, N_KEYS, tm), F32),
                        pltpu.VMEM((PEER_TE, tm), F32), pltpu.VMEM((PEER_TE, tm), BF16),
                        pltpu.VMEM((PEER_TE, tm), BF16), pltpu.VMEM((2, PEER_TE // N_KEYS, P_HEADS, 1, tm), F32)],
        compiler_params=pltpu.CompilerParams(
            dimension_semantics=("arbitrary", "arbitrary", "arbitrary"),
            vmem_limit_bytes=56 * 1024 * 1024),
        name="peer_dense",
    )(h2, u_bf16, vt_bf16, s1, c1, s2, st, x, ga)


def _peer_block(x, g, scale, shift, ga, pw):
    h2, s1, c1, s2, st = _peer_route(x, g, scale, shift, pw["wqt"], pw["keys"])
    return _peer_dense(h2, pw["u"], pw["vt"], s1, c1, s2, st, x, ga)


def _final_norm_kernel(x_ref, g_ref, o_ref):
    x = x_ref[0]
    o_ref[0] = x * lax.rsqrt(jnp.mean(x * x, axis=-1, keepdims=True) + NORM_EPS) * g_ref[...]


def _final_norm(x, g):
    b, L, d = x.shape
    tm = 512
    tok = pl.BlockSpec((1, tm, d), lambda bi, i: (bi, i, 0))
    return pl.pallas_call(
        _final_norm_kernel,
        grid=(b, L // tm),
        in_specs=[tok, pl.BlockSpec((1, d), lambda bi, i: (0, 0))],
        out_specs=tok,
        out_shape=jax.ShapeDtypeStruct((b, L, d), F32),
        compiler_params=_cparams(("arbitrary", "arbitrary")),
        name="final_norm",
    )(x, g.reshape(1, d))


HY_N2 = 256
HY_CT = 2048
HY_S = 8
HY_OT = 512


def _hyena_filter(L, hy_p):
    w1, b1, w2, b2, w3, freq = hy_p
    t = jnp.linspace(0.0, 1.0, L, dtype=F32)[:, None]
    bands = jnp.linspace(1e-4, HY_BANDS - 1, HY_BANDS, dtype=F32)[None, :]
    ang = (2.0 * math.pi / L) * jnp.arange(L, dtype=F32)[:, None] * bands
    z = jnp.concatenate([t, jnp.cos(ang), -jnp.sin(ang)], axis=-1)
    hdn = jnp.sin(freq * (z @ w1 + b1))
    hdn = jnp.sin(freq * (hdn @ w2 + b2))
    pos = jnp.arange(L)
    posb = (L - pos) % L
    hdn2 = jnp.stack([hdn, hdn[posb]])
    t2 = jnp.stack([t, t[posb]])
    deltas = jnp.abs(jnp.linspace(math.log(HY_TARGET) / HY_SLOW_PCT, math.log(HY_TARGET) / HY_FAST_PCT,
                                  HY_D, dtype=F32))
    w3r = w3.reshape(HY_HIDDEN, HY_ORDER, 2, HY_D)
    first = (jnp.arange(L) == 0)[None, :, None]
    side = jnp.arange(2)[:, None, None]
    out = []
    for o in range(HY_ORDER):
        h = jnp.einsum("slh,hsc->slc", hdn2, w3r[:, o]) * jnp.exp(-t2 * deltas)
        h = h * lax.rsqrt(jnp.sum(h * h, axis=(0, 1), keepdims=True))
        g = h + jnp.where(first & (side == 0), h[1:2, 0:1, :], 0.0)
        g = jnp.where(first & (side == 1), 0.0, g)
        out.append(g.reshape(2 * L, HY_D))
    return out


def _cis(idx, n):
    ang = (2.0 * math.pi / n) * (idx % n).astype(F32)
    return jnp.cos(ang), -jnp.sin(ang)


def _block_complex(re, im):
    return jnp.concatenate([jnp.concatenate([re, -im], axis=-1),
                            jnp.concatenate([im, re], axis=-1)], axis=-2)


def _dft_mats(n1, n2, kin):
    k = jnp.arange(n1, dtype=jnp.int32)
    c1, s1 = _cis(k[:, None] * k[None, :], n1)
    half = max(n1 // 2, 1)
    eye = jnp.eye(HY_S, dtype=F32)
    outer = jnp.kron(_block_complex(c1[:, :half], s1[:, :half]), eye).astype(BF16)
    outer_real = jnp.kron(jnp.concatenate([c1, s1], axis=0), eye).astype(BF16)
    m2 = jnp.arange(n2, dtype=jnp.int32)
    freq = k[:, None, None] + n1 * m2[None, :, None]
    gr, gi = _cis(m2[None, None, :] * freq, n1 * n2)
    inner = _block_complex(gr[:, :, :kin], gi[:, :, :kin]).astype(BF16)
    inner_real = jnp.concatenate([gr, gi], axis=1).astype(BF16)
    return {"n1": n1, "outer": outer, "outer_real": outer_real, "inner": inner, "inner_real": inner_real}


def _hy_outer_fwd_kernel(x_ref, m_ref, o_ref):
    ct = x_ref.shape[-1]
    x = x_ref[0].reshape(-1, ct).astype(BF16)
    f = jnp.dot(m_ref[...], x, preferred_element_type=F32)
    o_ref[...] = f.reshape(o_ref.shape)


def _hy_outer_fwd(x, which, mat8):
    _, groups, r, n2, c = x.shape
    n1 = mat8.shape[0] // (2 * HY_S)
    ct = min(c, HY_OT)
    return pl.pallas_call(
        _hy_outer_fwd_kernel,
        grid=(n2 // HY_S, c // ct),
        in_specs=[pl.BlockSpec((1, groups, r, HY_S, ct), lambda q, j: (which, 0, 0, q, j)),
                  pl.BlockSpec(mat8.shape, lambda q, j: (0, 0))],
        out_specs=pl.BlockSpec((2, n1, HY_S, ct), lambda q, j: (0, 0, q, j)),
        out_shape=jax.ShapeDtypeStruct((2, n1, n2, c), F32),
        compiler_params=_cparams(("arbitrary", "arbitrary")),
        name="hy_outer_fwd",
    )(x, mat8)


def _hy_inner_kernel(*refs, conv, scale):
    if conv:
        a_ref, g_ref, s_ref, o_ref = refs
    else:
        a_ref, g_ref, o_ref = refs
    parts = [a_ref[p, 0] for p in range(a_ref.shape[0])]
    a = (jnp.concatenate(parts, axis=0) if len(parts) > 1 else parts[0]).astype(BF16)
    g = g_ref[0]
    f = jnp.dot(g, a, preferred_element_type=F32)
    ko = f.shape[0] // 2
    fr, fi = f[:ko], f[ko:]
    if not conv:
        o_ref[0, 0] = (fr * scale).astype(o_ref.dtype)
        o_ref[1, 0] = (fi * scale).astype(o_ref.dtype)
    else:
        sr, si = s_ref[0, 0].astype(F32), s_ref[1, 0].astype(F32)
        p = jnp.concatenate([fr * sr - fi * si, fr * si + fi * sr], axis=0).astype(BF16)
        d = lax.dot_general(g, p, (((0,), (0,)), ((), ())), preferred_element_type=F32)
        kin = d.shape[0] // 2
        o_ref[0, 0] = d[:kin].astype(o_ref.dtype)
        o_ref[1, 0] = d[kin:].astype(o_ref.dtype)


def _hy_inner(a, gmat, spec, out_dtype, scale=1.0):
    p, n1, kin, c = a.shape
    ko = gmat.shape[1] // 2
    conv = spec is not None
    ct = min(c, HY_CT)
    in_specs = [pl.BlockSpec((p, 1, kin, ct), lambda k, j: (0, k, 0, j)),
                pl.BlockSpec((1,) + gmat.shape[1:], lambda k, j: (k, 0, 0))]
    args = [a, gmat]
    if conv:
        in_specs.append(pl.BlockSpec((2, 1, ko, ct), lambda k, j: (0, k, 0, j)))
        args.append(spec)
    rows = kin if conv else ko
    return pl.pallas_call(
        functools.partial(_hy_inner_kernel, conv=conv, scale=scale),
        grid=(n1, c // ct),
        in_specs=in_specs,
        out_specs=pl.BlockSpec((2, 1, rows, ct), lambda k, j: (0, k, 0, j)),
        out_shape=jax.ShapeDtypeStruct((2, n1, rows, c), out_dtype),
        compiler_params=_cparams(("arbitrary", "arbitrary")),
        name="hy_inner_conv" if conv else "hy_inner_spec",
    )(*args)


def _hy_outer_inv_kernel(d_ref, m_ref, v_ref, x_ref, sk_ref, o_ref):
    ct = d_ref.shape[-1]
    d = d_ref[...].reshape(-1, ct).astype(BF16)
    y = lax.dot_general(m_ref[...], d, (((0,), (0,)), ((), ())), preferred_element_type=F32)
    y = y.reshape(o_ref.shape[1:])
    o_ref[0] = x_ref[0] * (y + v_ref[0] * sk_ref[...])


def _hy_outer_inv(d, mat8, src, si, gate, gi, skip_row):
    _, n1, n2, c = d.shape
    half = n1 // 2
    ct = min(c, HY_OT)
    blk = lambda which: pl.BlockSpec((1, 2, half, HY_S, ct), lambda q, j: (which, 0, 0, q, j))
    return pl.pallas_call(
        _hy_outer_inv_kernel,
        grid=(n2 // HY_S, c // ct),
        in_specs=[pl.BlockSpec((2, n1, HY_S, ct), lambda q, j: (0, 0, q, j)),
                  pl.BlockSpec(mat8.shape, lambda q, j: (0, 0)),
                  blk(si), blk(gi),
                  pl.BlockSpec((1, ct), lambda q, j: (0, j))],
        out_specs=blk(0),
        out_shape=jax.ShapeDtypeStruct((1, 2, half, n2, c), F32),
        compiler_params=_cparams(("arbitrary", "arbitrary")),
        name="hy_outer_inv",
    )(d, mat8, src, gate, skip_row)


def _hyena_spectrum(g, mats):
    n, c = g[0].shape
    n1 = mats["n1"]
    n2 = n // n1
    out = []
    for go in g:
        if n1 > 1:
            a = _hy_outer_fwd(go.reshape(1, 1, n1, n2, c), 0, mats["outer_real"])
            out.append(_hy_inner(a, mats["inner"], None, BF16, 1.0 / n))
        else:
            out.append(_hy_inner(go.reshape(1, 1, n, c), mats["inner_real"], None, BF16, 1.0 / n))
    return out


def _hy_gate_kernel(y_ref, v_ref, x_ref, sk_ref, o_ref):
    o_ref[...] = x_ref[...] * (y_ref[...] + v_ref[...] * sk_ref[...])


def _hy_gate(y, v, x, skip_row):
    b, L, c = y.shape
    tok = pl.BlockSpec((1, L, c), lambda bi: (bi, 0, 0))
    return pl.pallas_call(
        _hy_gate_kernel,
        grid=(b,),
        in_specs=[tok, tok, tok, pl.BlockSpec((1, c), lambda bi: (0, 0))],
        out_specs=tok,
        out_shape=jax.ShapeDtypeStruct((b, L, c), F32),
        compiler_params=_cparams(("arbitrary",)),
        name="hy_gate",
    )(y, v, x, skip_row)


def _hyena_mixer(hy, spec, skip, mats):
    _, b, L, c = hy.shape
    assert b == 2, "the two batch entries are packed as one complex signal"
    n1 = mats["n1"]
    if n1 == 1:
        z = hy[0]
        for o in range(HY_ORDER):
            y = _hy_inner(z.reshape(2, 1, L, c), mats["inner"], spec[o], F32).reshape(b, L, c)
            z = _hy_gate(y, z, hy[o + 1], skip[o].reshape(1, c))
        return z
    n2 = 2 * L // n1
    src, si = hy.reshape(3, b, n1 // 2, n2, c), 0
    gate = src
    for o in range(HY_ORDER):
        a = _hy_outer_fwd(src, si, mats["outer"])
        d = _hy_inner(a, mats["inner"], spec[o], F32)
        src, si = _hy_outer_inv(d, mats["outer"], src, si, gate, o + 1, skip[o].reshape(1, c)), 0
    return src.reshape(b, L, c)


def _token_mixing(x, g, scale, shift, ga, lw, n_seg, s0_f, s0_b, hy_p):
    b, L, _ = x.shape
    seg = L // n_seg
    proj = functools.partial(_in_proj, x, g, scale, shift, lw["w_in"], lw["conv_w"], seg)
    y_rw, s_f, s_b = _rwkv_mixer(proj("rw"), lw["rw"], s0_f, s0_b)
    if hy_p is None:
        return None, s_f, s_b
    n = 2 * L
    n1 = n // HY_N2 if n // HY_N2 >= 16 else 1
    mats = _dft_mats(n1, n // n1, L if n1 == 1 else n // n1)
    spec = _hyena_spectrum(_hyena_filter(L, hy_p), mats)
    y_hy = _hyena_mixer(proj("hy"), spec, lw["hy_skip"], mats)
    return _out_proj(y_rw, y_hy, proj("gate"), x, ga, lw["w_out"]), s_f, s_b


def kernel(x, c, ctx, c_ctx, ada_w, ada_b, norm1_g, norm2_g, w_in, conv_w, rw_w0, rw_w2, rw_a0, rw_a2, rw_g2, rw_k_k, rw_k_a, rw_r_k, rw_ln_w, rw_ln_b, hy_w1, hy_b1, hy_w2, hy_b2, hy_w3, hy_freq, hy_skip, w_out, peer_wq, peer_keys, peer_u, peer_v, final_g):
    b, L, d = x.shape
    depth = ada_w.shape[0]
    ctx_len = ctx.shape[1]
    s_zero = jnp.zeros((b, N_GROUPS, GROUP_W, GROUP_W), F32)
    cond = jnp.concatenate([c, c_ctx[None, :]], axis=0)
    cond = jnp.pad(jax.nn.silu(cond), ((0, 8 - (b + 1)), (0, 0)))
    for l in range(depth):
        last = l == depth - 1
        mod = _matmul(cond, ada_w[l], 1024) + ada_b[l]
        mx = [m[:, None, :] for m in jnp.split(mod[:b], 6, axis=-1)]
        mc = [jnp.broadcast_to(m[None, :, :], (b, 1, d)) for m in jnp.split(mod[b:b + 1], 6, axis=-1)]
        lw = {
            "w_in": _pack_cols(w_in[l]).astype(BF16),
            "conv_w": _pack_cols(conv_w[l]),
            "rw": _rwkv_params(l, rw_w0, rw_w2, rw_a0, rw_a2, rw_g2, rw_k_k, rw_k_a, rw_r_k, rw_ln_w, rw_ln_b),
            "hy_skip": hy_skip[l],
            "w_out": w_out[l].astype(BF16),
        }
        hy_p = (hy_w1[l], hy_b1[l], hy_w2[l], hy_b2[l], hy_w3[l], hy_freq[l])
        pw = {
            "wqt": peer_wq[l].T.astype(BF16),
            "keys": peer_keys[l].astype(BF16),
            "u": peer_u[l].astype(BF16),
            "vt": peer_v[l].T.astype(BF16),
        }
        g1 = norm1_g[l].reshape(1, d)
        g2 = norm2_g[l].reshape(1, d)

        ctx_new, s_f, s_b = _token_mixing(ctx, g1, mc[1], mc[0], mc[2], lw, 1, s_zero, s_zero,
                                          None if last else hy_p)
        if not last:
            ctx = _peer_block(ctx_new, g2, mc[4], mc[3], mc[5], pw)

        x, _, _ = _token_mixing(x, g1, mx[1], mx[0], mx[2], lw, L // GRID_W, s_f, s_b, hy_p)
        x = _peer_block(x, g2, mx[4], mx[3], mx[5], pw)
    return _final_norm(x, final_g)
```
